```python
import jax, jax.numpy as jnp
from jax import lax
import numpy as np

D_MODEL = 1024
BATCH = 16
SEQ = 2048
DEPTH = 1
DEC_BATCH = 32
DEC_SEQ = 16
PAST_LEN = 4096

CHUNK = 64
Q_BLOCK = 128
N_MEM = 256
EPS = 1e-6
NEG_INF = -1e30
FOX_HEADS = 8
FOX_HEAD_DIM = 64
FOX_WIDTH = FOX_HEADS * FOX_HEAD_DIM
FOX_SCALE = FOX_HEAD_DIM ** -0.5
FORGET_BIAS_INIT = 2.0
MLA_HEADS = 8
MLA_NOPE = 64
MLA_ROPE = 32
MLA_V = 64
MLA_Q_LORA = 256
MLA_KV_LORA = 256
MLA_SCALE = (MLA_NOPE + MLA_ROPE) ** -0.5
ROPE_BASE = 10000.0
X_HEADS = 4
X_HEAD_DIM = 128
X_WIDTH = X_HEADS * X_HEAD_DIM
X_SCALE = X_HEAD_DIM ** -0.5
N_EXPERTS = 32
TOP_K = 4
EXPERT_FF = D_MODEL
SWIGLU_LIMIT = 7.0
SWIGLU_ALPHA = 1.702
MOE_BLOCK = 128
IN_SIZES = (FOX_WIDTH, FOX_WIDTH, FOX_WIDTH, FOX_HEADS, MLA_Q_LORA, MLA_KV_LORA, MLA_ROPE, D_MODEL, D_MODEL)
IN_DIM = 3 * FOX_WIDTH + FOX_HEADS + MLA_Q_LORA + MLA_KV_LORA + MLA_ROPE + 2 * D_MODEL

kernel_name = 'hybrid_fox_mla_moe_streaming_step'


def rmsnorm(x, g):
    xf = x.astype(jnp.float32)
    y = xf * lax.rsqrt(jnp.mean(xf * xf, axis=-1, keepdims=True) + EPS)
    return (y * g.astype(jnp.float32)).astype(x.dtype)


def rope(x, pos):
    half = x.shape[-1] // 2
    inv_freq = ROPE_BASE ** (-jnp.arange(half, dtype=jnp.float32) / half)
    ang = pos.astype(jnp.float32)[:, None] * inv_freq[None, :]
    ang = ang.reshape((ang.shape[0],) + (1,) * (x.ndim - 3) + (half,))
    cos, sin = jnp.cos(ang), jnp.sin(ang)
    xf = x.astype(jnp.float32)
    x1, x2 = xf[..., :half], xf[..., half:]
    return jnp.concatenate([x1 * cos - x2 * sin, x2 * cos + x1 * sin], axis=-1).astype(x.dtype)


def split_points():
    pts, acc = [], 0
    for s in IN_SIZES[:-1]:
        acc += s
        pts.append(acc)
    return pts


def mixer_inputs(h, pos, p):
    B, T = h.shape[:2]
    z = h @ p['w_in']
    fq, fk, fv, f_logit, q_c, kv_c, k_r, g_a, g_b = jnp.split(z, split_points(), axis=-1)
    fq = fq.reshape(B, T, FOX_HEADS, FOX_HEAD_DIM)
    fk = fk.reshape(B, T, FOX_HEADS, FOX_HEAD_DIM)
    fv = fv.reshape(B, T, FOX_HEADS, FOX_HEAD_DIM)
    logf = jax.nn.log_sigmoid((f_logit + p['b_f']).astype(jnp.float32))
    q = (rmsnorm(q_c, p['g_q']) @ p['w_uq']).reshape(B, T, MLA_HEADS, MLA_NOPE + MLA_ROPE)
    q_nope = q[..., :MLA_NOPE]
    q_rope = rope(q[..., MLA_NOPE:], pos)
    q_lat = jnp.einsum('bthn,chn->bthc', q_nope, p['w_uk'])
    ckv = rmsnorm(kv_c, p['g_kv'])
    krope = rope(k_r, pos)
    return fq, fk, fv, logf, q_lat, q_rope, ckv, krope, jax.nn.sigmoid(g_a), jax.nn.sigmoid(g_b)


def fox_attend(q, k, v, cum_q, cum_k, qpos, kpos):
    s = jnp.einsum('bqhd,bkhd->bhqk', q, k).astype(jnp.float32) * FOX_SCALE
    s = s + jnp.swapaxes(cum_q, 1, 2)[..., :, None] - jnp.swapaxes(cum_k, 1, 2)[..., None, :]
    mask = kpos[None, :] <= qpos[:, None]
    pr = jax.nn.softmax(jnp.where(mask, s, NEG_INF), axis=-1)
    return jnp.einsum('bhqk,bkhd->bqhd', pr.astype(v.dtype), v)


def mla_attend(q_lat, q_rope, ckv, krope, qpos, kpos):
    s = (jnp.einsum('bqhc,bkc->bhqk', q_lat, ckv) + jnp.einsum('bqhr,bkr->bhqk', q_rope, krope)).astype(jnp.float32) * MLA_SCALE
    mask = (kpos[None, :] // CHUNK) <= (qpos[:, None] // CHUNK)
    pr = jax.nn.softmax(jnp.where(mask, s, NEG_INF), axis=-1)
    return jnp.einsum('bhqk,bkc->bqhc', pr.astype(ckv.dtype), ckv)


def sweep_query_blocks(block_fn, seq_len):
    starts = jnp.arange(seq_len // Q_BLOCK, dtype=jnp.int32) * Q_BLOCK
    out = jnp.swapaxes(lax.map(block_fn, starts), 0, 1)
    return out.reshape((out.shape[0], seq_len) + out.shape[3:])


def qslice(a, s0):
    return lax.dynamic_slice_in_dim(a, s0, Q_BLOCK, axis=1)


def mixer_output(fox_o, mla_lat, g_fox, g_mla, p):
    B, T = fox_o.shape[:2]
    a = fox_o.reshape(B, T, FOX_WIDTH) @ p['w_o_fox']
    mla_o = jnp.einsum('bthc,chv->bthv', mla_lat, p['w_uv']).reshape(B, T, MLA_HEADS * MLA_V)
    b = mla_o @ p['w_o_mla']
    return (g_fox * a + g_mla * b) @ p['w_out']


def mem_kv(mem, p):
    B, M = mem.shape[:2]
    m = rmsnorm(mem, p['g_mem'])
    k = (m @ p['w_xk']).reshape(B, M, X_HEADS, X_HEAD_DIM)
    v = (m @ p['w_xv']).reshape(B, M, X_HEADS, X_HEAD_DIM)
    return k, v


def cross_attend(h, mk, mv, p):
    B, T = h.shape[:2]
    q = (h @ p['w_xq']).reshape(B, T, X_HEADS, X_HEAD_DIM)
    s = jnp.einsum('bthd,bmhd->bhtm', q, mk).astype(jnp.float32) * X_SCALE
    pr = jax.nn.softmax(s, axis=-1)
    o = jnp.einsum('bhtm,bmhd->bthd', pr.astype(mv.dtype), mv).reshape(B, T, X_WIDTH)
    return o @ p['w_xo']


def moe(h, p):
    T, D = h.shape
    logits = (h @ p['w_router']).astype(jnp.float32) + p['b_router'].astype(jnp.float32)
    top_val, top_idx = lax.top_k(logits, TOP_K)
    gate_w = jax.nn.softmax(top_val, axis=-1)
    n_assign = T * TOP_K
    flat_e = top_idx.reshape(-1).astype(jnp.int32)
    flat_tok = jnp.repeat(jnp.arange(T, dtype=jnp.int32), TOP_K)
    flat_w = gate_w.reshape(-1)
    order = jnp.argsort(flat_e)
    sorted_e = flat_e[order]
    counts = jnp.zeros((N_EXPERTS,), jnp.int32).at[flat_e].add(1)
    padded = (counts + MOE_BLOCK - 1) // MOE_BLOCK * MOE_BLOCK
    pad_end = jnp.cumsum(padded)
    pad_start = pad_end - padded
    grp_start = jnp.cumsum(counts) - counts
    dest = pad_start[sorted_e] + jnp.arange(n_assign, dtype=jnp.int32) - grp_start[sorted_e]
    n_blocks = -(-(n_assign + N_EXPERTS * (MOE_BLOCK - 1)) // MOE_BLOCK)
    n_rows = n_blocks * MOE_BLOCK
    row_tok = jnp.zeros((n_rows,), jnp.int32).at[dest].set(flat_tok[order])
    row_w = jnp.zeros((n_rows,), jnp.float32).at[dest].set(flat_w[order])
    block_e = jnp.minimum(jnp.searchsorted(pad_end, jnp.arange(n_blocks, dtype=jnp.int32) * MOE_BLOCK, side='right'), N_EXPERTS - 1)
    xb = h[row_tok].reshape(n_blocks, MOE_BLOCK, D)

    def expert_block(args):
        xblk, e = args
        gu = xblk @ p['w_gu'][e] + p['b_gu'][e]
        gate = jnp.minimum(gu[..., :EXPERT_FF], SWIGLU_LIMIT)
        up = jnp.clip(gu[..., EXPERT_FF:], -SWIGLU_LIMIT, SWIGLU_LIMIT)
        glu = gate * jax.nn.sigmoid(gate * SWIGLU_ALPHA)
        return ((up + 1.0) * glu) @ p['w_down'][e] + p['b_down'][e]

    yb = lax.map(expert_block, (xb, block_e)).reshape(n_rows, D)
    y = jnp.zeros((T, D), jnp.float32).at[row_tok].add(yb.astype(jnp.float32) * row_w[:, None])
    return y.astype(h.dtype)


def residual_tail(x, mix, mk, mv, p):
    x = x + mix
    x = x + cross_attend(rmsnorm(x, p['g_x']), mk, mv, p)
    h = rmsnorm(x, p['g_ffn'])
    B, T, D = h.shape
    return x + moe(h.reshape(B * T, D), p).reshape(B, T, D)


def prompt_layer(x, mem, p):
    S = x.shape[1]
    pos = jnp.arange(S, dtype=jnp.int32)
    h = rmsnorm(x, p['g_mix'])
    fq, fk, fv, logf, q_lat, q_rope, ckv, krope, g_a, g_b = mixer_inputs(h, pos, p)
    cum = jnp.cumsum(logf, axis=1)

    def fox_block(s0):
        qpos = s0 + jnp.arange(Q_BLOCK, dtype=jnp.int32)
        return fox_attend(qslice(fq, s0), fk, fv, qslice(cum, s0), cum, qpos, pos)

    def mla_block(s0):
        qpos = s0 + jnp.arange(Q_BLOCK, dtype=jnp.int32)
        return mla_attend(qslice(q_lat, s0), qslice(q_rope, s0), ckv, krope, qpos, pos)

    fox_o = sweep_query_blocks(fox_block, S)
    mla_lat = sweep_query_blocks(mla_block, S)
    mix = mixer_output(fox_o, mla_lat, g_a, g_b, p)
    mk, mv = mem_kv(mem, p)
    x = residual_tail(x, mix, mk, mv, p)
    return x, fk, fv, logf, ckv, krope, mk, mv


def sample_layer(x, c_fk, c_fv, c_logf, c_ckv, c_krope, c_mk, c_mv, p):
    past, T = c_fk.shape[1], x.shape[1]
    qpos = past + jnp.arange(T, dtype=jnp.int32)
    kpos = jnp.arange(past + T, dtype=jnp.int32)
    h = rmsnorm(x, p['g_mix'])
    fq, fk, fv, logf, q_lat, q_rope, ckv, krope, g_a, g_b = mixer_inputs(h, qpos, p)
    k_all = jnp.concatenate([c_fk.astype(fk.dtype), fk], axis=1)
    v_all = jnp.concatenate([c_fv.astype(fv.dtype), fv], axis=1)
    cum = jnp.cumsum(jnp.concatenate([c_logf.astype(jnp.float32), logf], axis=1), axis=1)
    fox_o = fox_attend(fq, k_all, v_all, cum[:, past:], cum, qpos, kpos)
    ckv_all = jnp.concatenate([c_ckv.astype(ckv.dtype), ckv], axis=1)
    krope_all = jnp.concatenate([c_krope.astype(krope.dtype), krope], axis=1)
    mla_lat = mla_attend(q_lat, q_rope, ckv_all, krope_all, qpos, kpos)
    mix = mixer_output(fox_o, mla_lat, g_a, g_b, p)
    x = residual_tail(x, mix, c_mk.astype(x.dtype), c_mv.astype(x.dtype), p)
    return x, fk, fv, logf, ckv, krope


def setup_inputs(seed: int = 0) -> dict:
    key = jax.random.key(seed)
    ks = jax.random.split(key, 40)
    f32 = jnp.float32

    def nrm(i, shape, scale=1.0):
        return scale * jax.random.normal(ks[i], shape, f32)

    def gain(i, shape):
        return 1.0 + 0.05 * jax.random.normal(ks[i], shape, f32)

    L = DEPTH
    return {
        'x_prompt': nrm(0, (BATCH, SEQ, D_MODEL)),
        'x_sample': nrm(1, (DEC_BATCH, DEC_SEQ, D_MODEL)),
        'mem_prompt': nrm(2, (BATCH, N_MEM, D_MODEL)),
        'cache_fox_k': nrm(3, (L, DEC_BATCH, PAST_LEN, FOX_HEADS, FOX_HEAD_DIM)),
        'cache_fox_v': nrm(4, (L, DEC_BATCH, PAST_LEN, FOX_HEADS, FOX_HEAD_DIM)),
        'cache_fox_logf': jax.nn.log_sigmoid(FORGET_BIAS_INIT + nrm(5, (L, DEC_BATCH, PAST_LEN, FOX_HEADS))),
        'cache_mla_ckv': nrm(6, (L, DEC_BATCH, PAST_LEN, MLA_KV_LORA)),
        'cache_mla_krope': nrm(7, (L, DEC_BATCH, PAST_LEN, MLA_ROPE)),
        'cache_mem_k': nrm(8, (L, DEC_BATCH, N_MEM, X_HEADS, X_HEAD_DIM)),
        'cache_mem_v': nrm(9, (L, DEC_BATCH, N_MEM, X_HEADS, X_HEAD_DIM)),
        'g_mix': gain(10, (L, D_MODEL)),
        'w_in': nrm(11, (L, D_MODEL, IN_DIM), D_MODEL ** -0.5),
        'b_f': FORGET_BIAS_INIT + nrm(12, (L, FOX_HEADS), 0.1),
        'g_q': gain(13, (L, MLA_Q_LORA)),
        'w_uq': nrm(14, (L, MLA_Q_LORA, MLA_HEADS * (MLA_NOPE + MLA_ROPE)), MLA_Q_LORA ** -0.5),
        'g_kv': gain(15, (L, MLA_KV_LORA)),
        'w_uk': nrm(16, (L, MLA_KV_LORA, MLA_HEADS, MLA_NOPE), MLA_KV_LORA ** -0.5),
        'w_uv': nrm(17, (L, MLA_KV_LORA, MLA_HEADS, MLA_V), MLA_KV_LORA ** -0.5),
        'w_o_fox': nrm(18, (L, FOX_WIDTH, D_MODEL), FOX_WIDTH ** -0.5),
        'w_o_mla': nrm(19, (L, MLA_HEADS * MLA_V, D_MODEL), (MLA_HEADS * MLA_V) ** -0.5),
        'w_out': nrm(20, (L, D_MODEL, D_MODEL), D_MODEL ** -0.5),
        'g_x': gain(21, (L, D_MODEL)),
        'g_mem': gain(22, (L, D_MODEL)),
        'w_xq': nrm(23, (L, D_MODEL, X_WIDTH), D_MODEL ** -0.5),
        'w_xk': nrm(24, (L, D_MODEL, X_WIDTH), D_MODEL ** -0.5),
        'w_xv': nrm(25, (L, D_MODEL, X_WIDTH), D_MODEL ** -0.5),
        'w_xo': nrm(26, (L, X_WIDTH, D_MODEL), X_WIDTH ** -0.5),
        'g_ffn': gain(27, (L, D_MODEL)),
        'w_router': nrm(28, (L, D_MODEL, N_EXPERTS), D_MODEL ** -0.5),
        'b_router': nrm(29, (L, N_EXPERTS), 0.01),
        'w_gu': nrm(30, (L, N_EXPERTS, D_MODEL, 2 * EXPERT_FF), D_MODEL ** -0.5),
        'b_gu': nrm(31, (L, N_EXPERTS, 2 * EXPERT_FF), 0.01),
        'w_down': nrm(32, (L, N_EXPERTS, EXPERT_FF, D_MODEL), EXPERT_FF ** -0.5),
        'b_down': nrm(33, (L, N_EXPERTS, D_MODEL), 0.01),
        'g_final': gain(34, (D_MODEL,)),
    }


def reference(x_prompt, x_sample, mem_prompt, cache_fox_k, cache_fox_v, cache_fox_logf, cache_mla_ckv,
              cache_mla_krope, cache_mem_k, cache_mem_v, g_mix, w_in, b_f, g_q, w_uq, g_kv, w_uk, w_uv,
              w_o_fox, w_o_mla, w_out, g_x, g_mem, w_xq, w_xk, w_xv, w_xo, g_ffn, w_router, b_router,
              w_gu, b_gu, w_down, b_down, g_final):
    xp, xs = x_prompt, x_sample
    pk, pv, pl, pc, pr, pmk, pmv = [], [], [], [], [], [], []
    sk, sv, sl, sc, sr = [], [], [], [], []
    for l in range(DEPTH):
        p = {'g_mix': g_mix[l], 'w_in': w_in[l], 'b_f': b_f[l], 'g_q': g_q[l], 'w_uq': w_uq[l],
             'g_kv': g_kv[l], 'w_uk': w_uk[l], 'w_uv': w_uv[l], 'w_o_fox': w_o_fox[l],
             'w_o_mla': w_o_mla[l], 'w_out': w_out[l], 'g_x': g_x[l], 'g_mem': g_mem[l],
             'w_xq': w_xq[l], 'w_xk': w_xk[l], 'w_xv': w_xv[l], 'w_xo': w_xo[l], 'g_ffn': g_ffn[l],
             'w_router': w_router[l], 'b_router': b_router[l], 'w_gu': w_gu[l], 'b_gu': b_gu[l],
             'w_down': w_down[l], 'b_down': b_down[l]}
        xp, fk, fv, lf, ck, kr, mk, mv = prompt_layer(xp, mem_prompt, p)
        pk.append(fk); pv.append(fv); pl.append(lf); pc.append(ck); pr.append(kr); pmk.append(mk); pmv.append(mv)
        xs, fk2, fv2, lf2, ck2, kr2 = sample_layer(xs, cache_fox_k[l], cache_fox_v[l], cache_fox_logf[l],
                                                   cache_mla_ckv[l], cache_mla_krope[l], cache_mem_k[l],
                                                   cache_mem_v[l], p)
        sk.append(fk2); sv.append(fv2); sl.append(lf2); sc.append(ck2); sr.append(kr2)
    y_prompt = rmsnorm(xp, g_final)
    y_sample = rmsnorm(xs, g_final)
    p_fox_k = jnp.stack(pk); p_fox_v = jnp.stack(pv); p_fox_logf = jnp.stack(pl)
    p_mla_ckv = jnp.stack(pc); p_mla_krope = jnp.stack(pr)
    p_mem_k = jnp.stack(pmk); p_mem_v = jnp.stack(pmv)
    s_fox_k = jnp.stack(sk); s_fox_v = jnp.stack(sv); s_fox_logf = jnp.stack(sl)
    s_mla_ckv = jnp.stack(sc); s_mla_krope = jnp.stack(sr)
    return (y_prompt, y_sample, p_fox_k, p_fox_v, p_fox_logf, p_mla_ckv, p_mla_krope, p_mem_k, p_mem_v,
            s_fox_k, s_fox_v, s_fox_logf, s_mla_ckv, s_mla_krope)
```

```python
import functools

import jax
import jax.numpy as jnp
from jax import lax
from jax.experimental import pallas as pl
from jax.experimental.pallas import tpu as pltpu

F32 = jnp.float32
BF16 = jnp.bfloat16
I32 = jnp.int32

CHUNK = 64
EPS = 1e-6
NEG_INF = -1e30
ROPE_BASE = 10000.0
TOP_K = 4
SWIGLU_LIMIT = 7.0
SWIGLU_ALPHA = 1.702

LANES = 128
SUBLANES = 8
VMEM_LIMIT_BYTES = 56 * 1024 * 1024

MISC_LOGF = 0
MISC_CUM = 8
MISC_ROPE = 16


def _cparams(*sem):
    return pltpu.CompilerParams(dimension_semantics=sem, vmem_limit_bytes=VMEM_LIMIT_BYTES)


def _const_spec(shape):
    nd = len(shape)
    return pl.BlockSpec(shape, lambda *_: (0,) * nd, pipeline_mode=pl.Buffered(1))


def _rms(x, g):
    return x * lax.rsqrt(jnp.mean(x * x, axis=-1, keepdims=True) + EPS) * g


def _sigmoid(x):
    return 1.0 / (1.0 + jnp.exp(-x))


def _log_sigmoid(x):
    return jnp.minimum(x, 0.0) - jnp.log1p(jnp.exp(-jnp.abs(x)))


def _prefix_sum(c, axis):
    n = c.shape[axis]
    pos = lax.broadcasted_iota(I32, c.shape, axis)
    s = 1
    while s < n:
        c = c + jnp.where(pos >= s, pltpu.roll(c, s, axis), 0.0)
        s *= 2
    return c


def _dot(a, b):
    return jnp.dot(a, b, preferred_element_type=F32)


def _dot_nt(a, b):
    return lax.dot_general(a, b, (((1,), (1,)), ((), ())), preferred_element_type=F32)


def _tile_lanes(t, n):
    return jnp.concatenate([t] * n, axis=1)


def _in_proj_kernel(x_ref, gmix_ref, wfox_ref, wma_ref, wmb_ref, wqc_ref, wkvc_ref, wga_ref, wgb_ref,
                    bf_ref, gq_ref, wqa_ref, wqb_ref, gkv_ref, wk_ref, wuv_ref,
                    ck_ref, sk_ref, cq_ref, sq_ref,
                    fq_ref, fk_ref, fv_ref, misc_ref, ckv_ref, qm_ref, ga_ref, gb_ref, *rest,
                    fox_width, n_heads, fox_scale, tiles_per_seq, prompt):
    if prompt:
        km_ref, vm_ref, carry_ref = rest
    tm = x_ref.shape[0]
    h = _rms(x_ref[...], gmix_ref[...]).astype(BF16)

    fq_ref[...] = (_dot(h, wfox_ref[:, 0:fox_width]) * fox_scale).astype(BF16)
    fk_ref[...] = _dot(h, wfox_ref[:, fox_width:2 * fox_width])
    fv_ref[...] = _dot(h, wfox_ref[:, 2 * fox_width:3 * fox_width])

    lane = lax.broadcasted_iota(I32, (tm, LANES), 1)
    pre = _dot(h, wma_ref[...]) * ck_ref[...] + _dot(h, wmb_ref[...]) * sk_ref[...]
    misc = jnp.where(lane < MISC_ROPE, _log_sigmoid(pre + bf_ref[...]), pre)
    if prompt:
        i = pl.program_id(0)

        @pl.when(i % tiles_per_seq == 0)
        def _():
            carry_ref[...] = jnp.zeros_like(carry_ref)

        in_cum = (lane >= MISC_CUM) & (lane < MISC_ROPE)
        cum = _prefix_sum(jnp.where(in_cum, misc, 0.0), 0) + carry_ref[...]
        carry_ref[...] = cum[tm - 1:tm, :]
        misc = jnp.where(in_cum, cum, misc)
    misc_ref[...] = misc

    ckv = _rms(_dot(h, wkvc_ref[...]), gkv_ref[...])
    ckv_ref[...] = ckv

    qn = _rms(_dot(h, wqc_ref[...]), gq_ref[...]).astype(BF16)
    cq = _tile_lanes(cq_ref[...], n_heads)
    sq = _tile_lanes(sq_ref[...], n_heads)
    qm_ref[...] = (_dot(qn, wqa_ref[...]) * cq + _dot(qn, wqb_ref[...]) * sq).astype(BF16)

    if prompt:
        ckv_b = ckv.astype(BF16)
        km_ref[...] = _dot(jnp.concatenate([ckv_b, misc.astype(BF16)], axis=1), wk_ref[...]).astype(BF16)
        vm_ref[...] = _dot(ckv_b, wuv_ref[...]).astype(BF16)

    ga_ref[...] = _sigmoid(_dot(h, wga_ref[...])).astype(BF16)
    gb_ref[...] = _sigmoid(_dot(h, wgb_ref[...])).astype(BF16)


def _in_proj(x, w, tables, *, tm, tiles_per_seq, prompt):
    T, D = x.shape
    fw = w['fox_width']
    nh = w['n_heads']
    lat = w['wkvc'].shape[1]
    hw = nh * LANES
    ck, sk, cq, sq = tables
    tok = lambda n: pl.BlockSpec((tm, n), lambda i: (i, 0))
    if prompt:
        tab = pl.BlockSpec((tm, LANES), lambda i: (i % tiles_per_seq, 0))
    else:
        tab = tok(LANES)
    weights = [w['gmix'], w['wfox'], w['wma'], w['wmb'], w['wqc'], w['wkvc'], w['wga'], w['wgb'],
               w['bf'], w['gq'], w['wqa'], w['wqb'], w['gkv'], w['wk'], w['wuv']]
    out_shape = [jax.ShapeDtypeStruct((T, fw), BF16), jax.ShapeDtypeStruct((T, fw), F32),
                 jax.ShapeDtypeStruct((T, fw), F32), jax.ShapeDtypeStruct((T, LANES), F32),
                 jax.ShapeDtypeStruct((T, lat), F32), jax.ShapeDtypeStruct((T, hw), BF16),
                 jax.ShapeDtypeStruct((T, D), BF16), jax.ShapeDtypeStruct((T, D), BF16)]
    out_specs = [tok(fw), tok(fw), tok(fw), tok(LANES), tok(lat), tok(hw), tok(D), tok(D)]
    scratch = []
    if prompt:
        out_shape += [jax.ShapeDtypeStruct((T, hw), BF16), jax.ShapeDtypeStruct((T, w['wuv'].shape[1]), BF16)]
        out_specs += [tok(hw), tok(w['wuv'].shape[1])]
        scratch = [pltpu.VMEM((1, LANES), F32)]
    kern = functools.partial(_in_proj_kernel, fox_width=fw, n_heads=nh, fox_scale=w['fox_scale'],
                             tiles_per_seq=tiles_per_seq, prompt=prompt)
    return pl.pallas_call(
        kern,
        grid=(T // tm,),
        in_specs=[tok(D)] + [_const_spec(a.shape) for a in weights] + [tab] * 4,
        out_specs=out_specs,
        out_shape=out_shape,
        scratch_shapes=scratch,
        compiler_params=_cparams("arbitrary"),
        name="in_proj_prompt" if prompt else "in_proj_sample",
    )(x, *weights, ck, sk, cq, sq)


def _prompt_attn_kernel(*refs, blk, n_heads, fox):
    if fox:
        q_ref, k_ref, v_ref, ccol_ref, crow_ref, o_ref = refs
    else:
        q_ref, k_ref, v_ref, o_ref = refs
    qi = pl.program_id(1)
    half = LANES // 2
    row = lax.broadcasted_iota(I32, (blk, blk), 0) + qi * blk
    col0 = lax.broadcasted_iota(I32, (blk, blk), 1)
    lane = lax.broadcasted_iota(I32, (blk, LANES), 1)
    for p in range(n_heads // 2):
        pair = slice(p * LANES, (p + 1) * LANES)
        outs = []
        for hh in range(2):
            hd = 2 * p + hh
            if fox:
                mine = (lane >= half) if hh else (lane < half)
                qm = jnp.where(mine, q_ref[0, :, pair], jnp.zeros((), BF16))
                cq = ccol_ref[0, :, MISC_CUM + hd:MISC_CUM + hd + 1]
            else:
                qm = q_ref[0, :, hd * LANES:(hd + 1) * LANES]

            def body(kb, carry, qm=qm, hd=hd, pair=pair):
                m, l, acc = carry
                ks = pl.multiple_of(kb * blk, blk)
                if fox:
                    kk = k_ref[0, pl.ds(ks, blk), pair].astype(BF16)
                    vv = v_ref[0, pl.ds(ks, blk), pair].astype(BF16)
                else:
                    kk = k_ref[0, pl.ds(ks, blk), hd * LANES:(hd + 1) * LANES]
                    vv = v_ref[0, pl.ds(ks, blk), pair]
                s = _dot_nt(qm, kk)
                col = col0 + ks
                if fox:
                    s = s + cq - crow_ref[0, kb, hd:hd + 1, :]
                    mask = col <= row
                else:
                    mask = (col // CHUNK) <= (row // CHUNK)
                s = jnp.where(mask, s, NEG_INF)
                m_new = jnp.maximum(m, jnp.max(s, axis=1, keepdims=True))
                alpha = jnp.exp(m - m_new)
                pe = jnp.exp(s - m_new)
                l = alpha * l + jnp.sum(pe, axis=1, keepdims=True)
                acc = alpha * acc + _dot(pe.astype(BF16), vv)
                return m_new, l, acc

            init = (jnp.full((blk, 1), NEG_INF, F32), jnp.zeros((blk, 1), F32), jnp.zeros((blk, LANES), F32))
            m, l, acc = lax.fori_loop(0, qi + 1, body, init)
            outs.append(acc / l)
        o_ref[0, :, pair] = jnp.where(lane < half, outs[0], outs[1]).astype(o_ref.dtype)


def _prompt_attn(q, k, v, cum_col, cum_row, *, blk, n_heads, fox):
    B, S, qw = q.shape
    vw = v.shape[2]
    in_specs = [pl.BlockSpec((1, blk, qw), lambda b, i: (b, i, 0)),
                pl.BlockSpec((1, S, k.shape[2]), lambda b, i: (b, 0, 0)),
                pl.BlockSpec((1, S, vw), lambda b, i: (b, 0, 0))]
    args = [q, k, v]
    if fox:
        in_specs += [pl.BlockSpec((1, blk, LANES), lambda b, i: (b, i, 0)),
                     pl.BlockSpec((1,) + cum_row.shape[1:], lambda b, i: (b, 0, 0, 0))]
        args += [cum_col, cum_row]
    return pl.pallas_call(
        functools.partial(_prompt_attn_kernel, blk=blk, n_heads=n_heads, fox=fox),
        grid=(B, S // blk),
        in_specs=in_specs,
        out_specs=pl.BlockSpec((1, blk, vw), lambda b, i: (b, i, 0)),
        out_shape=jax.ShapeDtypeStruct((B, S, vw), BF16),
        compiler_params=_cparams("parallel", "parallel"),
        name="fox_attn_prompt" if fox else "mla_attn_prompt",
    )(*args)


def _lane_cumsum_kernel(x_ref, o_ref):
    o_ref[...] = _prefix_sum(x_ref[...], 1)


def _lane_cumsum(x):
    return pl.pallas_call(
        _lane_cumsum_kernel,
        out_shape=jax.ShapeDtypeStruct(x.shape, F32),
        compiler_params=pltpu.CompilerParams(vmem_limit_bytes=VMEM_LIMIT_BYTES),
        name="logf_cumsum_sample",
    )(x)


def _head_rows(x, n_heads, head_lanes):
    t, w = x.shape
    xt = jnp.concatenate([x] * n_heads, axis=0)
    row = lax.broadcasted_iota(I32, xt.shape, 0)
    lane = lax.broadcasted_iota(I32, xt.shape, 1)
    return jnp.where(lane // head_lanes == row // t, xt, jnp.zeros((), x.dtype))


def _fold_head_rows(o, n_heads, t, head_lanes):
    row = lax.broadcasted_iota(I32, o.shape, 0)
    lane = lax.broadcasted_iota(I32, o.shape, 1)
    o = jnp.where(lane // head_lanes == row // t, o, 0.0)
    out = o[0:t]
    for hd in range(1, n_heads):
        out = out + o[hd * t:(hd + 1) * t]
    return out


def _repeat_rows(x, t):
    return jnp.concatenate([jnp.broadcast_to(x[r:r + 1], (t, x.shape[1])) for r in range(x.shape[0])], axis=0)


def _softmax_step(s, v, m_ref, l_ref, acc_ref):
    m = m_ref[...]
    m_new = jnp.maximum(m, jnp.max(s, axis=1, keepdims=True))
    alpha = jnp.exp(m - m_new)
    pe = jnp.exp(s - m_new)
    l_ref[...] = alpha * l_ref[...] + jnp.sum(pe, axis=1, keepdims=True)
    acc_ref[...] = alpha * acc_ref[...] + _dot(pe.astype(BF16), v)
    m_ref[...] = m_new


def _fox_sample_kernel(q_ref, k_ref, v_ref, kn_ref, vn_ref, cq_ref, ck_ref, ckn_ref, o_ref,
                       qbd_ref, m_ref, l_ref, acc_ref, *, n_heads, t_new):
    c = pl.program_id(1)
    head_lanes = q_ref.shape[2] // n_heads

    @pl.when(c == 0)
    def _():
        qbd_ref[...] = _head_rows(q_ref[0], n_heads, head_lanes)
        m_ref[...] = jnp.full_like(m_ref, NEG_INF)
        l_ref[...] = jnp.zeros_like(l_ref)
        acc_ref[...] = jnp.zeros_like(acc_ref)

    qbd = qbd_ref[...]
    cq = cq_ref[0]
    s = _dot_nt(qbd, k_ref[0].astype(BF16)) + cq - _repeat_rows(ck_ref[0], t_new)
    _softmax_step(s, v_ref[0].astype(BF16), m_ref, l_ref, acc_ref)

    @pl.when(c == pl.num_programs(1) - 1)
    def _():
        sn = _dot_nt(qbd, kn_ref[0].astype(BF16)) + cq - _repeat_rows(ckn_ref[0][:, 0:t_new], t_new)
        row = lax.broadcasted_iota(I32, sn.shape, 0)
        col = lax.broadcasted_iota(I32, sn.shape, 1)
        sn = jnp.where(col <= row % t_new, sn, NEG_INF)
        _softmax_step(sn, vn_ref[0].astype(BF16), m_ref, l_ref, acc_ref)
        o = acc_ref[...] / l_ref[...]
        o_ref[0] = _fold_head_rows(o, n_heads, t_new, head_lanes).astype(o_ref.dtype)


def _fox_sample_attn(q, k_cache, v_cache, k_new, v_new, cum_q, cum_all, *, n_heads, chunk):
    B, t_new, w = q.shape
    P = k_cache.shape[1]
    rows = n_heads * t_new
    return pl.pallas_call(
        functools.partial(_fox_sample_kernel, n_heads=n_heads, t_new=t_new),
        grid=(B, P // chunk),
        in_specs=[pl.BlockSpec((1, t_new, w), lambda b, c: (b, 0, 0)),
                  pl.BlockSpec((1, chunk, w), lambda b, c: (b, c, 0)),
                  pl.BlockSpec((1, chunk, w), lambda b, c: (b, c, 0)),
                  pl.BlockSpec((1, t_new, w), lambda b, c: (b, 0, 0)),
                  pl.BlockSpec((1, t_new, w), lambda b, c: (b, 0, 0)),
                  pl.BlockSpec((1, rows, 1), lambda b, c: (b, 0, 0)),
                  pl.BlockSpec((1, n_heads, chunk), lambda b, c: (b, 0, c)),
                  pl.BlockSpec((1, n_heads, LANES), lambda b, c: (b, 0, P // LANES))],
        out_specs=pl.BlockSpec((1, t_new, w), lambda b, c: (b, 0, 0)),
        out_shape=jax.ShapeDtypeStruct((B, t_new, w), BF16),
        scratch_shapes=[pltpu.VMEM((rows, w), BF16), pltpu.VMEM((rows, 1), F32),
                        pltpu.VMEM((rows, 1), F32), pltpu.VMEM((rows, w), F32)],
        compiler_params=_cparams("parallel", "arbitrary"),
        name="fox_attn_sample",
    )(q, k_cache, v_cache, k_new, v_new, cum_q, cum_all, cum_all)


def _mla_sample_kernel(q_ref, wuk_ref, ckv_ref, kr_ref, ckvn_ref, miscn_ref, wuv_ref, o_ref,
                       ql_ref, qr_ref, m_ref, l_ref, acc_ref, *, n_heads, t_new, nope, rope, past):
    c = pl.program_id(1)

    @pl.when(c == 0)
    def _():
        for hd in range(n_heads):
            qh = q_ref[0, :, hd * LANES:(hd + 1) * LANES]
            rows = slice(hd * t_new, (hd + 1) * t_new)
            ql_ref[rows, :] = _dot(qh, wuk_ref[hd]).astype(BF16)
            qr_ref[rows, :] = qh[:, nope:nope + rope]
        m_ref[...] = jnp.full_like(m_ref, NEG_INF)
        l_ref[...] = jnp.zeros_like(l_ref)
        acc_ref[...] = jnp.zeros_like(acc_ref)

    ql = ql_ref[...]
    qr = qr_ref[...]
    ckv = ckv_ref[0].astype(BF16)
    s = _dot_nt(ql, ckv) + _dot_nt(qr, kr_ref[0].astype(BF16))
    _softmax_step(s, ckv, m_ref, l_ref, acc_ref)

    @pl.when(c == pl.num_programs(1) - 1)
    def _():
        ckvn = ckvn_ref[0].astype(BF16)
        krn = miscn_ref[0][:, MISC_ROPE:MISC_ROPE + rope].astype(BF16)
        sn = _dot_nt(ql, ckvn) + _dot_nt(qr, krn)
        row = lax.broadcasted_iota(I32, sn.shape, 0)
        col = lax.broadcasted_iota(I32, sn.shape, 1)
        sn = jnp.where((past + col) // CHUNK <= (past + row % t_new) // CHUNK, sn, NEG_INF)
        _softmax_step(sn, ckvn, m_ref, l_ref, acc_ref)
        lat = (acc_ref[...] / l_ref[...]).astype(BF16)
        o = _dot(lat, wuv_ref[...])
        o_ref[0] = _fold_head_rows(o, n_heads, t_new, o.shape[1] // n_heads).astype(o_ref.dtype)


def _mla_sample_attn(qm, wukp, ckv_cache, kr_cache, ckv_new, misc_new, wuv, *, n_heads, chunk, nope, rope):
    B, t_new, qw = qm.shape
    P, lat = ckv_cache.shape[1:]
    rows = n_heads * t_new
    vw = wuv.shape[1]
    return pl.pallas_call(
        functools.partial(_mla_sample_kernel, n_heads=n_heads, t_new=t_new, nope=nope, rope=rope, past=P),
        grid=(B, P // chunk),
        in_specs=[pl.BlockSpec((1, t_new, qw), lambda b, c: (b, 0, 0)),
                  _const_spec(wukp.shape),
                  pl.BlockSpec((1, chunk, lat), lambda b, c: (b, c, 0)),
                  pl.BlockSpec((1, chunk, rope), lambda b, c: (b, c, 0)),
                  pl.BlockSpec((1, t_new, lat), lambda b, c: (b, 0, 0)),
                  pl.BlockSpec((1, t_new, LANES), lambda b, c: (b, 0, 0)),
                  _const_spec(wuv.shape)],
        out_specs=pl.BlockSpec((1, t_new, vw), lambda b, c: (b, 0, 0)),
        out_shape=jax.ShapeDtypeStruct((B, t_new, vw), BF16),
        scratch_shapes=[pltpu.VMEM((rows, lat), BF16), pltpu.VMEM((rows, rope), BF16),
                        pltpu.VMEM((rows, 1), F32), pltpu.VMEM((rows, 1), F32), pltpu.VMEM((rows, lat), F32)],
        compiler_params=_cparams("parallel", "arbitrary"),
        name="mla_attn_sample",
    )(qm, wukp, ckv_cache, kr_cache, ckv_new, misc_new, wuv)


def _mix_part(x, fo, mo, ga, gb, wof_ref, wom_ref, wout_ref, gx_ref, wxq_ref, x_scale):
    a = _dot(fo, wof_ref[...])
    b = _dot(mo, wom_ref[...])
    merged = (ga.astype(F32) * a + gb.astype(F32) * b).astype(BF16)
    x1 = x + _dot(merged, wout_ref[...])
    xq = (_dot(_rms(x1, gx_ref[...]).astype(BF16), wxq_ref[...]) * x_scale).astype(BF16)
    return x1, xq


def _cross_part(xq, mk, mv, x_heads):
    hd_w = xq.shape[1] // x_heads
    outs = []
    for hd in range(x_heads):
        sl = slice(hd * hd_w, (hd + 1) * hd_w)
        s = _dot_nt(xq[:, sl], mk[:, sl])
        pe = jnp.exp(s - jnp.max(s, axis=1, keepdims=True))
        o = _dot(pe.astype(BF16), mv[:, sl]) / jnp.sum(pe, axis=1, keepdims=True)
        outs.append(o.astype(BF16))
    return jnp.concatenate(outs, axis=1)


def _route_part(x1, ca, wxo_ref, gffn_ref, wrt_ref, br_ref, carry_ref,
                x2_ref, hf_ref, idx_ref, gate_ref, rank_ref, cnt_ref):
    tm = x1.shape[0]
    n_exp = wrt_ref.shape[1]
    x2 = x1 + _dot(ca, wxo_ref[...])
    x2_ref[...] = x2
    hf = _rms(x2, gffn_ref[...])
    hf_ref[...] = hf
    hf_hi = hf.astype(BF16)
    hf_lo = (hf - hf_hi.astype(F32)).astype(BF16)
    logits = (_dot_nt(wrt_ref[0], hf_hi) + _dot_nt(wrt_ref[0], hf_lo) + _dot_nt(wrt_ref[1], hf_hi)
              + br_ref[...])
    erow = lax.broadcasted_iota(I32, (n_exp, tm), 0).astype(F32)
    picked = jnp.zeros((n_exp, tm), F32)
    vals, onehots = [], []
    for k in range(TOP_K):
        mx = jnp.max(logits, axis=0, keepdims=True)
        idx = jnp.min(jnp.where(logits == mx, erow, float(n_exp)), axis=0, keepdims=True)
        sel = erow == idx
        idx_ref[k:k + 1, :] = idx.astype(I32)
        vals.append(mx)
        onehots.append(sel)
        picked = picked + sel.astype(F32)
        logits = jnp.where(sel, -jnp.inf, logits)
    ex = [jnp.exp(v - vals[0]) for v in vals]
    den = ex[0] + ex[1] + ex[2] + ex[3]
    for k in range(TOP_K):
        gate_ref[k:k + 1, :] = ex[k] / den
    r = lax.broadcasted_iota(I32, (tm, tm), 0)
    cidx = lax.broadcasted_iota(I32, (tm, tm), 1)
    upper = jnp.where(r < cidx, 1.0, 0.0).astype(BF16)
    before = _dot(picked.astype(BF16), upper) + carry_ref[...]
    for k in range(TOP_K):
        rank_ref[k:k + 1, :] = jnp.sum(jnp.where(onehots[k], before, 0.0), axis=0, keepdims=True).astype(I32)
    carry_ref[...] = carry_ref[...] + jnp.sum(picked, axis=1, keepdims=True)
    cnt_ref[...] = jnp.broadcast_to(carry_ref[...], cnt_ref.shape).astype(I32)


def _post_attn_prompt_kernel(x_ref, fo_ref, mo_ref, ga_ref, gb_ref, mk_ref, mv_ref,
                             wof_ref, wom_ref, wout_ref, gx_ref, wxq_ref, wxo_ref, gffn_ref, wrt_ref, br_ref,
                             x2_ref, hf_ref, idx_ref, gate_ref, rank_ref, cnt_ref, carry_ref,
                             *, x_heads, x_scale):
    @pl.when(pl.program_id(0) == 0)
    def _():
        carry_ref[...] = jnp.zeros_like(carry_ref)

    x1, xq = _mix_part(x_ref[...], fo_ref[...], mo_ref[...], ga_ref[...], gb_ref[...],
                       wof_ref, wom_ref, wout_ref, gx_ref, wxq_ref, x_scale)
    ca = _cross_part(xq, mk_ref[0].astype(BF16), mv_ref[0].astype(BF16), x_heads)
    _route_part(x1, ca, wxo_ref, gffn_ref, wrt_ref, br_ref, carry_ref,
                x2_ref, hf_ref, idx_ref, gate_ref, rank_ref, cnt_ref)


def _route_out(T, D, n_exp, tm):
    tok = lambda n: pl.BlockSpec((tm, n), lambda i: (i, 0))
    col = pl.BlockSpec((TOP_K, tm), lambda i: (0, i))
    shapes = [jax.ShapeDtypeStruct((T, D), F32), jax.ShapeDtypeStruct((T, D), F32),
              jax.ShapeDtypeStruct((TOP_K, T), I32), jax.ShapeDtypeStruct((TOP_K, T), F32),
              jax.ShapeDtypeStruct((TOP_K, T), I32), jax.ShapeDtypeStruct((n_exp, LANES), I32)]
    specs = [tok(D), tok(D), col, col, col, pl.BlockSpec((n_exp, LANES), lambda i: (0, 0))]
    return shapes, specs


def _post_attn_prompt(x, fo, mo, ga, gb, mk, mv, w, *, tm, seq):
    T, D = x.shape
    n_exp = w['wrt'].shape[1]
    tiles_per_seq = seq // tm
    tok = lambda n: pl.BlockSpec((tm, n), lambda i: (i, 0))
    mem = pl.BlockSpec((1,) + mk.shape[1:], lambda i: (i // tiles_per_seq, 0, 0))
    weights = [w['wof'], w['wom'], w['wout'], w['gx'], w['wxq'], w['wxo'], w['gffn'], w['wrt'], w['br']]
    shapes, specs = _route_out(T, D, n_exp, tm)
    return pl.pallas_call(
        functools.partial(_post_attn_prompt_kernel, x_heads=w['x_heads'], x_scale=w['x_scale']),
        grid=(T // tm,),
        in_specs=[tok(D), tok(fo.shape[1]), tok(mo.shape[1]), tok(D), tok(D), mem, mem]
        + [_const_spec(a.shape) for a in weights],
        out_specs=specs,
        out_shape=shapes,
        scratch_shapes=[pltpu.VMEM((n_exp, 1), F32)],
        compiler_params=_cparams("arbitrary"),
        name="post_attn_prompt",
    )(x, fo, mo, ga, gb, mk, mv, *weights)


def _mix_sample_kernel(x_ref, fo_ref, mo_ref, ga_ref, gb_ref, wof_ref, wom_ref, wout_ref, gx_ref, wxq_ref,
                       x1_ref, xq_ref, *, x_scale):
    x1, xq = _mix_part(x_ref[...], fo_ref[...], mo_ref[...], ga_ref[...], gb_ref[...],
                       wof_ref, wom_ref, wout_ref, gx_ref, wxq_ref, x_scale)
    x1_ref[...] = x1
    xq_ref[...] = xq


def _cross_sample_kernel(xq_ref, mk_ref, mv_ref, o_ref, *, x_heads):
    o_ref[0] = _cross_part(xq_ref[0], mk_ref[0].astype(BF16), mv_ref[0].astype(BF16), x_heads)


def _route_sample_kernel(x1_ref, ca_ref, wxo_ref, gffn_ref, wrt_ref, br_ref,
                         x2_ref, hf_ref, idx_ref, gate_ref, rank_ref, cnt_ref, carry_ref):
    carry_ref[...] = jnp.zeros_like(carry_ref)
    _route_part(x1_ref[...], ca_ref[...], wxo_ref, gffn_ref, wrt_ref, br_ref, carry_ref,
                x2_ref, hf_ref, idx_ref, gate_ref, rank_ref, cnt_ref)


def _post_attn_sample(x, fo, mo, ga, gb, mk, mv, w, *, t_new):
    T, D = x.shape
    B = T // t_new
    n_exp = w['wrt'].shape[1]
    xw = w['wxq'].shape[1]
    params = pltpu.CompilerParams(vmem_limit_bytes=VMEM_LIMIT_BYTES)
    x1, xq = pl.pallas_call(
        functools.partial(_mix_sample_kernel, x_scale=w['x_scale']),
        out_shape=[jax.ShapeDtypeStruct((T, D), F32), jax.ShapeDtypeStruct((T, xw), BF16)],
        compiler_params=params,
        name="mix_sample",
    )(x, fo, mo, ga, gb, w['wof'], w['wom'], w['wout'], w['gx'], w['wxq'])
    ca = pl.pallas_call(
        functools.partial(_cross_sample_kernel, x_heads=w['x_heads']),
        grid=(B,),
        in_specs=[pl.BlockSpec((1, t_new, xw), lambda b: (b, 0, 0)),
                  pl.BlockSpec((1,) + mk.shape[1:], lambda b: (b, 0, 0)),
                  pl.BlockSpec((1,) + mv.shape[1:], lambda b: (b, 0, 0))],
        out_specs=pl.BlockSpec((1, t_new, xw), lambda b: (b, 0, 0)),
        out_shape=jax.ShapeDtypeStruct((B, t_new, xw), BF16),
        compiler_params=_cparams("parallel"),
        name="cross_sample",
    )(xq.reshape(B, t_new, xw), mk, mv)
    shapes, _ = _route_out(T, D, n_exp, T)
    return pl.pallas_call(
        _route_sample_kernel,
        out_shape=shapes,
        scratch_shapes=[pltpu.VMEM((n_exp, 1), F32)],
        compiler_params=params,
        name="route_sample",
    )(x1, ca.reshape(T, xw), w['wxo'], w['gffn'], w['wrt'], w['br'])


def _mem_kv_kernel(m_ref, g_ref, wk_ref, wv_ref, k_ref, v_ref):
    m = _rms(m_ref[...], g_ref[...]).astype(BF16)
    k_ref[...] = _dot(m, wk_ref[...])
    v_ref[...] = _dot(m, wv_ref[...])


def _mem_kv(mem, w, *, tm):
    T, D = mem.shape
    xw = w['wxk'].shape[1]
    tok = lambda n: pl.BlockSpec((tm, n), lambda i: (i, 0))
    return pl.pallas_call(
        _mem_kv_kernel,
        grid=(T // tm,),
        in_specs=[tok(D), _const_spec(w['gmem'].shape), _const_spec(w['wxk'].shape), _const_spec(w['wxv'].shape)],
        out_specs=[tok(xw), tok(xw)],
        out_shape=[jax.ShapeDtypeStruct((T, xw), F32)] * 2,
        compiler_params=_cparams("parallel"),
        name="mem_kv",
    )(mem, w['gmem'], w['wxk'], w['wxv'])


def _dest_kernel(start_ref, idx_ref, rank_ref, dest_ref, *, n_exp):
    idx = idx_ref[...]
    dest = rank_ref[...]
    for e in range(n_exp):
        dest = dest + jnp.where(idx == e, start_ref[e], 0)
    dest_ref[...] = dest


def _dest_rows(pad_start, idx_t, rank_t):
    n_exp = pad_start.shape[0]
    return pl.pallas_call(
        functools.partial(_dest_kernel, n_exp=n_exp),
        grid_spec=pltpu.PrefetchScalarGridSpec(
            num_scalar_prefetch=1, grid=(1,),
            in_specs=[pl.BlockSpec(idx_t.shape, lambda i, s: (0, 0)), pl.BlockSpec(rank_t.shape, lambda i, s: (0, 0))],
            out_specs=pl.BlockSpec(idx_t.shape, lambda i, s: (0, 0))),
        out_shape=jax.ShapeDtypeStruct(idx_t.shape, I32),
        name="moe_dest",
    )(pad_start, idx_t, rank_t)


def _row_copy(src_ref, src_row, dst_ref, dst_row, sem):
    return pltpu.make_async_copy(src_ref.at[pl.ds(src_row, 1), :], dst_ref.at[pl.ds(dst_row, 1), :], sem)


def _dispatch_kernel(dest_ref, h_ref, xs_in_ref, xs_ref, sem):
    del xs_in_ref
    tm = h_ref.shape[0]

    def issue(t, _):
        for k in range(TOP_K):
            _row_copy(h_ref, t, xs_ref, dest_ref[k, t], sem).start()
        return 0

    lax.fori_loop(0, tm, issue, 0)

    def drain(t, _):
        for k in range(TOP_K):
            _row_copy(h_ref, 0, xs_ref, 0, sem).wait()
        return 0

    lax.fori_loop(0, tm, drain, 0)


def _dispatch(dest_t, h, n_rows, *, tm):
    T, D = h.shape
    xs0 = jnp.zeros((n_rows, D), h.dtype)
    return pl.pallas_call(
        _dispatch_kernel,
        grid=(T // tm,),
        in_specs=[pl.BlockSpec((TOP_K, tm), lambda i: (0, i), memory_space=pltpu.SMEM),
                  pl.BlockSpec((tm, D), lambda i: (i, 0)),
                  pl.BlockSpec(memory_space=pl.ANY)],
        out_specs=pl.BlockSpec(memory_space=pl.ANY),
        out_shape=jax.ShapeDtypeStruct((n_rows, D), h.dtype),
        scratch_shapes=[pltpu.SemaphoreType.DMA],
        input_output_aliases={2: 0},
        compiler_params=_cparams("arbitrary"),
        name="moe_dispatch",
    )(dest_t, h, xs0)


def _expert_kernel(be_ref, nu_ref, x_ref, wgu_ref, bgu_ref, wd_ref, bd_ref, y_ref, *, ff):
    i = pl.program_id(0)

    @pl.when(i < nu_ref[0])
    def _():
        gu = _dot(x_ref[...].astype(BF16), wgu_ref[0]) + bgu_ref[0]
        gate = jnp.minimum(gu[:, :ff], SWIGLU_LIMIT)
        up = jnp.clip(gu[:, ff:], -SWIGLU_LIMIT, SWIGLU_LIMIT)
        act = (up + 1.0) * (gate * _sigmoid(gate * SWIGLU_ALPHA))
        y_ref[...] = _dot(act.astype(BF16), wd_ref[0]) + bd_ref[0]

    @pl.when(i >= nu_ref[0])
    def _():
        y_ref[...] = jnp.zeros_like(y_ref)


def _expert_ffn(block_e, n_used, xs, w, *, bm):
    n_rows, D = xs.shape
    ff = w['wd'].shape[1]
    last = lambda i, be, nu: jnp.minimum(i, nu[0] - 1)
    return pl.pallas_call(
        functools.partial(_expert_kernel, ff=ff),
        grid_spec=pltpu.PrefetchScalarGridSpec(
            num_scalar_prefetch=2, grid=(n_rows // bm,),
            in_specs=[pl.BlockSpec((bm, D), lambda i, be, nu: (last(i, be, nu), 0)),
                      pl.BlockSpec((1, D, 2 * ff), lambda i, be, nu: (be[i], 0, 0)),
                      pl.BlockSpec((1, 1, 2 * ff), lambda i, be, nu: (be[i], 0, 0)),
                      pl.BlockSpec((1, ff, D), lambda i, be, nu: (be[i], 0, 0)),
                      pl.BlockSpec((1, 1, D), lambda i, be, nu: (be[i], 0, 0))],
            out_specs=pl.BlockSpec((bm, D), lambda i, be, nu: (i, 0))),
        out_shape=jax.ShapeDtypeStruct((n_rows, D), F32),
        compiler_params=_cparams("arbitrary"),
        name="moe_expert_ffn",
    )(block_e, n_used, xs, w['wgu'], w['bgu'], w['wd'], w['bd'])


def _combine_kernel(dest_ref, gate_ref, x2_ref, gfin_ref, yb_ref, y_ref, buf_ref, sem):
    tm = x2_ref.shape[0]

    def issue(t, _):
        for k in range(TOP_K):
            _row_copy(yb_ref, dest_ref[k, t], buf_ref.at[k], t, sem).start()
        return 0

    lax.fori_loop(0, tm, issue, 0)

    def drain(t, _):
        for k in range(TOP_K):
            _row_copy(yb_ref, 0, buf_ref.at[k], 0, sem).wait()
        return 0

    lax.fori_loop(0, tm, drain, 0)
    gate = gate_ref[...]
    y = x2_ref[...]
    for k in range(TOP_K):
        y = y + gate[:, k:k + 1] * buf_ref[k]
    y_ref[...] = _rms(y, gfin_ref[...])


def _combine(dest_t, gate_tok, x2, gfin, yb, *, tm):
    T, D = x2.shape
    return pl.pallas_call(
        _combine_kernel,
        grid=(T // tm,),
        in_specs=[pl.BlockSpec((TOP_K, tm), lambda i: (0, i), memory_space=pltpu.SMEM),
                  pl.BlockSpec((tm, TOP_K), lambda i: (i, 0)),
                  pl.BlockSpec((tm, D), lambda i: (i, 0)),
                  _const_spec(gfin.shape),
                  pl.BlockSpec(memory_space=pl.ANY)],
        out_specs=pl.BlockSpec((tm, D), lambda i: (i, 0)),
        out_shape=jax.ShapeDtypeStruct((T, D), F32),
        scratch_shapes=[pltpu.VMEM((TOP_K, tm, D), F32), pltpu.SemaphoreType.DMA],
        compiler_params=_cparams("arbitrary"),
        name="moe_combine",
    )(dest_t, gate_tok, x2, gfin, yb)


def _moe_and_final_norm(x2, hf, idx_t, gate_t, rank_t, counts, w, gfin, *, bm, tm):
    T, D = x2.shape
    n_exp = counts.shape[0]
    padded = (counts + bm - 1) // bm * bm
    pad_end = jnp.cumsum(padded)
    pad_start = pad_end - padded
    n_blocks = -(-(T * TOP_K + n_exp * (bm - 1)) // bm)
    block_first_row = jnp.arange(n_blocks, dtype=I32) * bm
    block_e = jnp.minimum(jnp.sum(block_first_row[:, None] >= pad_end[None, :], axis=1), n_exp - 1).astype(I32)
    n_used = (pad_end[n_exp - 1:] // bm).astype(I32)
    dest_t = _dest_rows(pad_start.astype(I32), idx_t, rank_t)
    xs = _dispatch(dest_t, hf, n_blocks * bm, tm=tm)
    yb = _expert_ffn(block_e, n_used, xs, w, bm=bm)
    return _combine(dest_t, gate_t.T, x2, gfin, yb, tm=tm)


def _rot_cols(wc):
    half = wc.shape[1] // 2
    return jnp.concatenate([-wc[:, half:], wc[:, :half]], axis=1)


def _prep_layer(p, dims):
    fw, nh, nope, rope, lat, vdim, x_heads, x_hd = dims
    D = p['w_in'].shape[0]
    w_in = p['w_in']
    o = 0
    wfox = w_in[:, o:o + 3 * fw]; o += 3 * fw
    wfl = w_in[:, o:o + nh]; o += nh
    qlora = p['g_q'].shape[0]
    wqc = w_in[:, o:o + qlora]; o += qlora
    wkvc = w_in[:, o:o + lat]; o += lat
    wkr = w_in[:, o:o + rope]; o += rope
    wga = w_in[:, o:o + D]; o += D
    wgb = w_in[:, o:o + D]

    def misc_cols(parts):
        out = jnp.zeros((D, LANES), F32)
        for off, cols in parts:
            out = out.at[:, off:off + cols.shape[1]].set(cols)
        return out

    wma = misc_cols([(MISC_LOGF, wfl), (MISC_CUM, wfl), (MISC_ROPE, wkr)])
    wmb = misc_cols([(MISC_ROPE, _rot_cols(wkr))])
    bf = jnp.zeros((1, LANES), F32).at[0, MISC_LOGF:MISC_LOGF + nh].set(p['b_f']).at[0, MISC_CUM:MISC_CUM + nh].set(p['b_f'])

    wuq = p['w_uq'].reshape(qlora, nh, nope + rope)
    wqa = jnp.zeros((qlora, nh, LANES), F32).at[:, :, :nope + rope].set(wuq)
    rot = jnp.concatenate([-wuq[:, :, nope + rope // 2:], wuq[:, :, nope:nope + rope // 2]], axis=2)
    wqb = jnp.zeros((qlora, nh, LANES), F32).at[:, :, nope:nope + rope].set(rot)
    wk = jnp.zeros((lat + LANES, nh, LANES), F32).at[:lat, :, :nope].set(p['w_uk'])
    place = jnp.zeros((LANES, nh, LANES), F32)
    j = jnp.arange(rope)
    place = place.at[MISC_ROPE + j, :, nope + j].set(1.0)
    wk = wk.at[lat:, :, :].set(place)
    wukp = jnp.zeros((nh, LANES, lat), F32).at[:, :nope, :].set(jnp.transpose(p['w_uk'], (1, 2, 0)))

    b = lambda a: a.astype(BF16)
    row = lambda a: a.reshape(1, -1).astype(F32)
    n_exp = p['w_router'].shape[1]
    wrt = p['w_router'].T.astype(F32)
    wrt_hi = b(wrt)
    wrt = jnp.stack([wrt_hi, b(wrt - wrt_hi.astype(F32))])
    return {
        'fox_width': fw, 'n_heads': nh, 'fox_scale': float(fw // nh) ** -0.5,
        'x_heads': x_heads, 'x_scale': float(x_hd) ** -0.5,
        'gmix': row(p['g_mix']), 'wfox': b(wfox), 'wma': b(wma), 'wmb': b(wmb), 'wqc': b(wqc), 'wkvc': b(wkvc),
        'wga': b(wga), 'wgb': b(wgb), 'bf': bf, 'gq': row(p['g_q']),
        'wqa': b(wqa.reshape(qlora, nh * LANES)), 'wqb': b(wqb.reshape(qlora, nh * LANES)),
        'gkv': row(p['g_kv']), 'wk': b(wk.reshape(lat + LANES, nh * LANES)),
        'wuv': b(p['w_uv'].reshape(lat, nh * vdim)), 'wukp': b(wukp),
        'wof': b(p['w_o_fox']), 'wom': b(p['w_o_mla']), 'wout': b(p['w_out']), 'gx': row(p['g_x']),
        'wxq': b(p['w_xq']), 'wxo': b(p['w_xo']), 'gffn': row(p['g_ffn']),
        'wrt': wrt, 'br': p['b_router'].reshape(n_exp, 1).astype(F32),
        'gmem': row(p['g_mem']), 'wxk': b(p['w_xk']), 'wxv': b(p['w_xv']),
        'wgu': b(p['w_gu']), 'bgu': p['b_gu'].reshape(n_exp, 1, -1).astype(F32),
        'wd': b(p['w_down']), 'bd': p['b_down'].reshape(n_exp, 1, -1).astype(F32),
    }


def _rope_tables(pos, nope, rope, q_scale):
    half = rope // 2
    inv_freq = ROPE_BASE ** (-jnp.arange(half, dtype=F32) / half)
    ang = pos.astype(F32)[:, None] * inv_freq[None, :]
    cos = jnp.concatenate([jnp.cos(ang)] * 2, axis=1)
    sin = jnp.concatenate([jnp.sin(ang)] * 2, axis=1)
    n = pos.shape[0]
    ck = jnp.zeros((n, LANES), F32).at[:, :MISC_ROPE].set(1.0).at[:, MISC_ROPE:MISC_ROPE + rope].set(cos)
    sk = jnp.zeros((n, LANES), F32).at[:, MISC_ROPE:MISC_ROPE + rope].set(sin)
    cq = jnp.zeros((n, LANES), F32).at[:, :nope].set(1.0).at[:, nope:nope + rope].set(cos) * q_scale
    sq = jnp.zeros((n, LANES), F32).at[:, nope:nope + rope].set(sin) * q_scale
    return ck, sk, cq, sq


def _pick_tile(n, target):
    t = min(n, target)
    while n % t:
        t //= 2
    return t


def _prompt_layer(x, mem, w, dims):
    fw, nh, nope, rope, lat, vdim, x_heads, x_hd = dims
    B, S, D = x.shape
    T = B * S
    tm = _pick_tile(S, 256)
    tables = _rope_tables(jnp.arange(S, dtype=I32), nope, rope, float(nope + rope) ** -0.5)
    fq, fk, fv, misc, ckv, qm, ga, gb, km, vm = _in_proj(
        x.reshape(T, D), w, tables, tm=tm, tiles_per_seq=S // tm, prompt=True)

    blk = _pick_tile(S, 256)
    cum = misc[:, MISC_CUM:MISC_CUM + nh].reshape(B, S // blk, blk, nh)
    cum_row = jnp.swapaxes(cum, 2, 3)
    r3 = lambda a: a.reshape(B, S, a.shape[1])
    fox_o = _prompt_attn(r3(fq), r3(fk), r3(fv), r3(misc), cum_row, blk=blk, n_heads=nh, fox=True)
    mla_o = _prompt_attn(r3(qm), r3(km), r3(vm), None, None, blk=blk, n_heads=nh, fox=False)

    n_mem = mem.shape[1]
    mk, mv = _mem_kv(mem.reshape(B * n_mem, D), w, tm=_pick_tile(B * n_mem, 512))
    xw = mk.shape[1]
    x2, hf, idx_t, gate_t, rank_t, cnt = _post_attn_prompt(
        x.reshape(T, D), fox_o.reshape(T, fw), mla_o.reshape(T, nh * vdim), ga, gb,
        mk.reshape(B, n_mem, xw), mv.reshape(B, n_mem, xw), w, tm=tm, seq=S)
    caches = (fk.reshape(B, S, nh, fw // nh), fv.reshape(B, S, nh, fw // nh),
              misc[:, MISC_LOGF:MISC_LOGF + nh].reshape(B, S, nh), ckv.reshape(B, S, lat),
              misc[:, MISC_ROPE:MISC_ROPE + rope].reshape(B, S, rope),
              mk.reshape(B, n_mem, x_heads, x_hd), mv.reshape(B, n_mem, x_heads, x_hd))
    return (x2, hf, idx_t, gate_t, rank_t, cnt[:, 0]), caches


def _sample_layer(x, c_fk, c_fv, c_logf, c_ckv, c_kr, c_mk, c_mv, w, dims):
    fw, nh, nope, rope, lat, vdim, x_heads, x_hd = dims
    B, t_new, D = x.shape
    P = c_fk.shape[1]
    T = B * t_new
    pos = P + jnp.tile(jnp.arange(t_new, dtype=I32), B)
    tables = _rope_tables(pos, nope, rope, float(nope + rope) ** -0.5)
    fq, fk, fv, misc, ckv, qm, ga, gb = _in_proj(x.reshape(T, D), w, tables, tm=T, tiles_per_seq=1, prompt=False)

    logf_new = misc[:, MISC_LOGF:MISC_LOGF + nh].reshape(B, t_new, nh)
    lf = jnp.concatenate([c_logf.astype(F32), logf_new, jnp.zeros((B, LANES - t_new, nh), F32)], axis=1)
    cum_all = _lane_cumsum(jnp.swapaxes(lf, 1, 2).reshape(B * nh, P + LANES)).reshape(B, nh, P + LANES)
    cum_q = cum_all[:, :, P:P + t_new].reshape(B, nh * t_new, 1)

    chunk = _pick_tile(P, 1024)
    r3 = lambda a: a.reshape(B, t_new, a.shape[1])
    fox_o = _fox_sample_attn(r3(fq), c_fk.reshape(B, P, fw), c_fv.reshape(B, P, fw), r3(fk), r3(fv),
                             cum_q, cum_all, n_heads=nh, chunk=chunk)
    mla_o = _mla_sample_attn(r3(qm), w['wukp'], c_ckv, c_kr, r3(ckv), r3(misc), w['wuv'],
                             n_heads=nh, chunk=chunk, nope=nope, rope=rope)
    n_mem = c_mk.shape[1]
    x2, hf, idx_t, gate_t, rank_t, cnt = _post_attn_sample(
        x.reshape(T, D), fox_o.reshape(T, fw), mla_o.reshape(T, nh * vdim), ga, gb,
        c_mk.reshape(B, n_mem, x_heads * x_hd), c_mv.reshape(B, n_mem, x_heads * x_hd), w, t_new=t_new)
    caches = (fk.reshape(B, t_new, nh, fw // nh), fv.reshape(B, t_new, nh, fw // nh), logf_new,
              ckv.reshape(B, t_new, lat), misc[:, MISC_ROPE:MISC_ROPE + rope].reshape(B, t_new, rope))
    return (x2, hf, idx_t, gate_t, rank_t, cnt[:, 0]), caches


def kernel(x_prompt, x_sample, mem_prompt, cache_fox_k, cache_fox_v, cache_fox_logf, cache_mla_ckv,
           cache_mla_krope, cache_mem_k, cache_mem_v, g_mix, w_in, b_f, g_q, w_uq, g_kv, w_uk, w_uv,
           w_o_fox, w_o_mla, w_out, g_x, g_mem, w_xq, w_xk, w_xv, w_xo, g_ffn, w_router, b_router,
           w_gu, b_gu, w_down, b_down, g_final):
    depth = w_in.shape[0]
    assert depth == 1, "the fused final norm assumes a single layer"
    nh, fhd = cache_fox_k.shape[3:]
    lat = cache_mla_ckv.shape[3]
    rope = cache_mla_krope.shape[3]
    nope, vdim = w_uk.shape[3], w_uv.shape[3]
    x_heads, x_hd = cache_mem_k.shape[3:]
    dims = (nh * fhd, nh, nope, rope, lat, vdim, x_heads, x_hd)
    B, S, D = x_prompt.shape
    Bs, t_new, _ = x_sample.shape
    gfin = g_final.reshape(1, D).astype(F32)

    l = 0
    p = {'g_mix': g_mix[l], 'w_in': w_in[l], 'b_f': b_f[l], 'g_q': g_q[l], 'w_uq': w_uq[l],
         'g_kv': g_kv[l], 'w_uk': w_uk[l], 'w_uv': w_uv[l], 'w_o_fox': w_o_fox[l],
         'w_o_mla': w_o_mla[l], 'w_out': w_out[l], 'g_x': g_x[l], 'g_mem': g_mem[l],
         'w_xq': w_xq[l], 'w_xk': w_xk[l], 'w_xv': w_xv[l], 'w_xo': w_xo[l], 'g_ffn': g_ffn[l],
         'w_router': w_router[l], 'b_router': b_router[l], 'w_gu': w_gu[l], 'b_gu': b_gu[l],
         'w_down': w_down[l], 'b_down': b_down[l]}
    w = _prep_layer(p, dims)

    routed_p, pc = _prompt_layer(x_prompt, mem_prompt, w, dims)
    routed_s, sc = _sample_layer(x_sample, cache_fox_k[l], cache_fox_v[l], cache_fox_logf[l], cache_mla_ckv[l],
                                 cache_mla_krope[l], cache_mem_k[l], cache_mem_v[l], w, dims)
    y_prompt = _moe_and_final_norm(*routed_p, w, gfin, bm=512, tm=256).reshape(B, S, D)
    y_sample = _moe_and_final_norm(*routed_s, w, gfin, bm=128, tm=_pick_tile(Bs * t_new, 256)).reshape(Bs, t_new, D)
    return (y_prompt, y_sample) + tuple(a[None] for a in pc) + tuple(a[None] for a in sc)
```

```python
import functools
import math

import jax
import jax.numpy as jnp
from jax import lax
from jax.experimental import pallas as pl
from jax.experimental.pallas import tpu as pltpu

F32 = jnp.float32
BF16 = jnp.bfloat16
I32 = jnp.int32

CHUNK = 64
EPS = 1e-6
NEG_INF = -1e30
ROPE_BASE = 10000.0
TOP_K = 4
SWIGLU_LIMIT = 7.0
SWIGLU_ALPHA = 1.702
LOG2E = math.log2(math.e)

LANES = 128
SUBLANES = 8
VMEM_LIMIT_BYTES = 56 * 1024 * 1024

MISC_LOGF = 0
MISC_CUM = 8
MISC_ROPE = 16

TOKEN_TILE = 256
ATTN_BLOCK = 256
CACHE_CHUNK = 1024
EXPERT_ROWS_PROMPT = 512
EXPERT_ROWS_SAMPLE = 128
ROW_DMA_UNROLL = 4


def _cparams(*sem):
    return pltpu.CompilerParams(dimension_semantics=sem, vmem_limit_bytes=VMEM_LIMIT_BYTES)


def _const_spec(shape):
    nd = len(shape)
    return pl.BlockSpec(shape, lambda *_: (0,) * nd, pipeline_mode=pl.Buffered(1))


def _rms(x, g):
    return x * lax.rsqrt(jnp.mean(x * x, axis=-1, keepdims=True) + EPS) * g


def _sigmoid(x):
    return 1.0 / (1.0 + jnp.exp(-x))


def _log_sigmoid(x):
    return jnp.minimum(x, 0.0) - jnp.log1p(jnp.exp(-jnp.abs(x)))


def _prefix_sum(c, axis):
    n = c.shape[axis]
    pos = lax.broadcasted_iota(I32, c.shape, axis)
    s = 1
    while s < n:
        c = c + jnp.where(pos >= s, pltpu.roll(c, s, axis), 0.0)
        s *= 2
    return c


def _dot(a, b):
    return jnp.dot(a, b, preferred_element_type=F32)


def _dot_nt(a, b):
    return lax.dot_general(a, b, (((1,), (1,)), ((), ())), preferred_element_type=F32)


def _tile_lanes(t, n):
    return jnp.concatenate([t] * n, axis=1)


def _in_proj_kernel(x_ref, gmix_ref, wfox_ref, wma_ref, wmb_ref, wqc_ref, wkvc_ref, wga_ref, wgb_ref,
                    bf_ref, gq_ref, wqa_ref, wqb_ref, gkv_ref, wk_ref, wuv_ref,
                    ck_ref, sk_ref, cq_ref, sq_ref,
                    fq_ref, fk_ref, fv_ref, fkb_ref, fvb_ref, misc_ref, ckv_ref, qm_ref, ga_ref, gb_ref, *rest,
                    fox_width, n_heads, fox_scale, tiles_per_seq, prompt):
    if prompt:
        km_ref, vm_ref, carry_ref = rest
    tm = x_ref.shape[0]
    h = _rms(x_ref[...], gmix_ref[...]).astype(BF16)

    fq_ref[...] = (_dot(h, wfox_ref[:, 0:fox_width]) * fox_scale).astype(BF16)
    fk = _dot(h, wfox_ref[:, fox_width:2 * fox_width])
    fv = _dot(h, wfox_ref[:, 2 * fox_width:3 * fox_width])
    fk_ref[...] = fk
    fv_ref[...] = fv
    fkb_ref[...] = fk.astype(BF16)
    fvb_ref[...] = fv.astype(BF16)

    lane = lax.broadcasted_iota(I32, (tm, LANES), 1)
    pre = _dot(h, wma_ref[...]) * ck_ref[...] + _dot(h, wmb_ref[...]) * sk_ref[...]
    misc = jnp.where(lane < MISC_ROPE, _log_sigmoid(pre + bf_ref[...]), pre)
    if prompt:
        i = pl.program_id(0)

        @pl.when(i % tiles_per_seq == 0)
        def _():
            carry_ref[...] = jnp.zeros_like(carry_ref)

        in_cum = (lane >= MISC_CUM) & (lane < MISC_ROPE)
        cum = _prefix_sum(jnp.where(in_cum, misc * LOG2E, 0.0), 0) + carry_ref[...]
        carry_ref[...] = cum[tm - 1:tm, :]
        misc = jnp.where(in_cum, cum, misc)
    misc_ref[...] = misc

    ckv = _rms(_dot(h, wkvc_ref[...]), gkv_ref[...])
    ckv_ref[...] = ckv

    qn = _rms(_dot(h, wqc_ref[...]), gq_ref[...]).astype(BF16)
    cq = _tile_lanes(cq_ref[...], n_heads)
    sq = _tile_lanes(sq_ref[...], n_heads)
    qm_ref[...] = (_dot(qn, wqa_ref[...]) * cq + _dot(qn, wqb_ref[...]) * sq).astype(BF16)

    if prompt:
        ckv_b = ckv.astype(BF16)
        km_ref[...] = _dot(jnp.concatenate([ckv_b, misc.astype(BF16)], axis=1), wk_ref[...]).astype(BF16)
        vm_ref[...] = _dot(ckv_b, wuv_ref[...]).astype(BF16)

    ga_ref[...] = _sigmoid(_dot(h, wga_ref[...])).astype(BF16)
    gb_ref[...] = _sigmoid(_dot(h, wgb_ref[...])).astype(BF16)


def _in_proj(x, w, tables, *, tm, tiles_per_seq, prompt):
    T, D = x.shape
    fw = w['fox_width']
    nh = w['n_heads']
    lat = w['wkvc'].shape[1]
    hw = nh * LANES
    ck, sk, cq, sq = tables
    tok = lambda n: pl.BlockSpec((tm, n), lambda i: (i, 0))
    if prompt:
        tab = pl.BlockSpec((tm, LANES), lambda i: (i % tiles_per_seq, 0))
    else:
        tab = tok(LANES)
    weights = [w['gmix'], w['wfox'], w['wma'], w['wmb'], w['wqc'], w['wkvc'], w['wga'], w['wgb'],
               w['bf'], w['gq'], w['wqa'], w['wqb'], w['gkv'], w['wk'], w['wuv']]
    sds = jax.ShapeDtypeStruct
    out_shape = [sds((T, fw), BF16), sds((T, fw), F32), sds((T, fw), F32),
                 sds((T, fw), BF16), sds((T, fw), BF16), sds((T, LANES), F32),
                 sds((T, lat), F32), sds((T, hw), BF16), sds((T, D), BF16), sds((T, D), BF16)]
    out_specs = [tok(fw), tok(fw), tok(fw), tok(fw), tok(fw), tok(LANES), tok(lat), tok(hw), tok(D), tok(D)]
    scratch = []
    if prompt:
        out_shape += [sds((T, hw), BF16), sds((T, w['wuv'].shape[1]), BF16)]
        out_specs += [tok(hw), tok(w['wuv'].shape[1])]
        scratch = [pltpu.VMEM((1, LANES), F32)]
    kern = functools.partial(_in_proj_kernel, fox_width=fw, n_heads=nh, fox_scale=w['fox_scale'],
                             tiles_per_seq=tiles_per_seq, prompt=prompt)
    return pl.pallas_call(
        kern,
        grid=(T // tm,),
        in_specs=[tok(D)] + [_const_spec(a.shape) for a in weights] + [tab] * 4,
        out_specs=out_specs,
        out_shape=out_shape,
        scratch_shapes=scratch,
        compiler_params=_cparams("arbitrary"),
        name="in_proj_prompt" if prompt else "in_proj_sample",
    )(x, *weights, ck, sk, cq, sq)


def _fold_lanes(x, op):
    out = x[:, 0:LANES]
    for j in range(1, x.shape[1] // LANES):
        out = op(out, x[:, j * LANES:(j + 1) * LANES])
    return out


def _prompt_attn_kernel(*refs, blk, n_heads, fox):
    if fox:
        q_ref, k_ref, v_ref, ccol_ref, crow_ref, o_ref, qs_ref, cq_ref, s_ref, m_ref, l_ref, acc_ref = refs
    else:
        q_ref, k_ref, v_ref, o_ref, s_ref, m_ref, l_ref, acc_ref = refs
    qi = pl.program_id(1)
    n_pairs = n_heads // 2
    half = LANES // 2
    lane = lax.broadcasted_iota(I32, (blk, LANES), 1)

    m_ref[...] = jnp.full_like(m_ref, NEG_INF)
    if fox:
        zero = jnp.zeros((), BF16)
        for p in range(n_pairs):
            q2 = q_ref[0, :, p * LANES:(p + 1) * LANES]
            qs_ref[p, 0:blk, :] = jnp.where(lane < half, q2, zero)
            qs_ref[p, blk:2 * blk, :] = jnp.where(lane >= half, q2, zero)
            for hh in range(2):
                col = MISC_CUM + 2 * p + hh
                cq_ref[p, hh * blk:(hh + 1) * blk, :] = jnp.broadcast_to(ccol_ref[0, :, col:col + 1], (blk, LANES))

    def scores(kb, diagonal):
        ks = pl.multiple_of(kb * blk, blk)
        for p in range(n_pairs):
            if fox:
                s = _dot_nt(qs_ref[p], k_ref[0, pl.ds(ks, blk), p * LANES:(p + 1) * LANES])
                ck = jnp.concatenate([jnp.broadcast_to(crow_ref[0, kb, 2 * p + hh:2 * p + hh + 1, :], (blk, blk))
                                      for hh in range(2)], axis=0)
                s = s + (_tile_lanes(cq_ref[p], blk // LANES) - ck)
            else:
                s = jnp.concatenate(
                    [_dot_nt(q_ref[0, :, hd * LANES:(hd + 1) * LANES], k_ref[0, pl.ds(ks, blk), hd * LANES:(hd + 1) * LANES])
                     for hd in (2 * p, 2 * p + 1)], axis=0)
            if diagonal:
                r = lax.broadcasted_iota(I32, (2 * blk, blk), 0)
                r = jnp.where(r >= blk, r - blk, r)
                c = lax.broadcasted_iota(I32, (2 * blk, blk), 1)
                mask = (c <= r) if fox else ((c // CHUNK) <= (r // CHUNK))
                s = jnp.where(mask, s, NEG_INF)
            s_ref[p, kb] = s
            m_ref[p] = jnp.maximum(m_ref[p], _fold_lanes(s, jnp.maximum))

    def score_body(kb, carry):
        scores(kb, False)
        return carry

    lax.fori_loop(0, qi, score_body, 0)
    scores(qi, True)

    for p in range(n_pairs):
        m_ref[p] = jnp.broadcast_to(jnp.max(m_ref[p], axis=1, keepdims=True), (2 * blk, LANES))
    l_ref[...] = jnp.zeros_like(l_ref)
    acc_ref[...] = jnp.zeros_like(acc_ref)

    def weigh_body(kb, carry):
        ks = pl.multiple_of(kb * blk, blk)
        for p in range(n_pairs):
            pe = jnp.exp2(s_ref[p, kb] - _tile_lanes(m_ref[p], blk // LANES))
            l_ref[p] = l_ref[p] + _fold_lanes(pe, jnp.add)
            acc_ref[p] = acc_ref[p] + _dot(pe.astype(BF16), v_ref[0, pl.ds(ks, blk), p * LANES:(p + 1) * LANES])
        return carry

    lax.fori_loop(0, qi + 1, weigh_body, 0)
    for p in range(n_pairs):
        o = acc_ref[p] / jnp.sum(l_ref[p], axis=1, keepdims=True)
        o_ref[0, :, p * LANES:(p + 1) * LANES] = jnp.where(lane < half, o[0:blk], o[blk:2 * blk]).astype(o_ref.dtype)


def _prompt_attn(q, k, v, cum_col, cum_row, *, blk, n_heads, fox):
    B, S, qw = q.shape
    vw = v.shape[2]
    n_pairs = n_heads // 2
    in_specs = [pl.BlockSpec((1, blk, qw), lambda b, i: (b, i, 0)),
                pl.BlockSpec((1, S, k.shape[2]), lambda b, i: (b, 0, 0)),
                pl.BlockSpec((1, S, vw), lambda b, i: (b, 0, 0))]
    args = [q, k, v]
    stat = pltpu.VMEM((n_pairs, 2 * blk, LANES), F32)
    scratch = []
    if fox:
        in_specs += [pl.BlockSpec((1, blk, LANES), lambda b, i: (b, i, 0)),
                     pl.BlockSpec((1,) + cum_row.shape[1:], lambda b, i: (b, 0, 0, 0))]
        args += [cum_col, cum_row]
        scratch = [pltpu.VMEM((n_pairs, 2 * blk, LANES), BF16), stat]
    scratch += [pltpu.VMEM((n_pairs, S // blk, 2 * blk, blk), F32), stat, stat, stat]
    return pl.pallas_call(
        functools.partial(_prompt_attn_kernel, blk=blk, n_heads=n_heads, fox=fox),
        grid=(B, S // blk),
        in_specs=in_specs,
        out_specs=pl.BlockSpec((1, blk, vw), lambda b, i: (b, i, 0)),
        out_shape=jax.ShapeDtypeStruct((B, S, vw), BF16),
        scratch_shapes=scratch,
        compiler_params=_cparams("parallel", "parallel"),
        name="fox_attn_prompt" if fox else "mla_attn_prompt",
    )(*args)


def _lane_cumsum_kernel(x_ref, o_ref):
    o_ref[...] = _prefix_sum(x_ref[...], 1) * LOG2E


def _lane_cumsum(x):
    return pl.pallas_call(
        _lane_cumsum_kernel,
        out_shape=jax.ShapeDtypeStruct(x.shape, F32),
        compiler_params=pltpu.CompilerParams(vmem_limit_bytes=VMEM_LIMIT_BYTES),
        name="logf_cumsum_sample",
    )(x)


def _head_rows(x, n_heads, head_lanes):
    t, w = x.shape
    xt = jnp.concatenate([x] * n_heads, axis=0)
    row = lax.broadcasted_iota(I32, xt.shape, 0)
    lane = lax.broadcasted_iota(I32, xt.shape, 1)
    return jnp.where(lane // head_lanes == row // t, xt, jnp.zeros((), x.dtype))


def _fold_head_rows(o, n_heads, t, head_lanes):
    row = lax.broadcasted_iota(I32, o.shape, 0)
    lane = lax.broadcasted_iota(I32, o.shape, 1)
    o = jnp.where(lane // head_lanes == row // t, o, 0.0)
    out = o[0:t]
    for hd in range(1, n_heads):
        out = out + o[hd * t:(hd + 1) * t]
    return out


def _repeat_rows(x, t):
    return jnp.concatenate([jnp.broadcast_to(x[r:r + 1], (t, x.shape[1])) for r in range(x.shape[0])], axis=0)


def _softmax_step(s, v, m_ref, l_ref, acc_ref):
    m = m_ref[...]
    m_new = jnp.maximum(m, jnp.max(s, axis=1, keepdims=True))
    alpha = jnp.exp2(m - m_new)
    pe = jnp.exp2(s - m_new)
    l_ref[...] = alpha * l_ref[...] + jnp.sum(pe, axis=1, keepdims=True)
    acc_ref[...] = alpha * acc_ref[...] + _dot(pe.astype(BF16), v)
    m_ref[...] = m_new


def _fox_sample_kernel(q_ref, k_ref, v_ref, kn_ref, vn_ref, cq_ref, ck_ref, ckn_ref, o_ref,
                       qbd_ref, m_ref, l_ref, acc_ref, *, n_heads, t_new):
    c = pl.program_id(1)
    head_lanes = q_ref.shape[2] // n_heads

    @pl.when(c == 0)
    def _():
        qbd_ref[...] = _head_rows(q_ref[0], n_heads, head_lanes)
        m_ref[...] = jnp.full_like(m_ref, NEG_INF)
        l_ref[...] = jnp.zeros_like(l_ref)
        acc_ref[...] = jnp.zeros_like(acc_ref)

    qbd = qbd_ref[...]
    cq = cq_ref[0]
    s = _dot_nt(qbd, k_ref[0]) + cq - _repeat_rows(ck_ref[0], t_new)
    _softmax_step(s, v_ref[0], m_ref, l_ref, acc_ref)

    @pl.when(c == pl.num_programs(1) - 1)
    def _():
        sn = _dot_nt(qbd, kn_ref[0]) + cq - _repeat_rows(ckn_ref[0][:, 0:t_new], t_new)
        row = lax.broadcasted_iota(I32, sn.shape, 0)
        col = lax.broadcasted_iota(I32, sn.shape, 1)
        sn = jnp.where(col <= row % t_new, sn, NEG_INF)
        _softmax_step(sn, vn_ref[0], m_ref, l_ref, acc_ref)
        o = acc_ref[...] / l_ref[...]
        o_ref[0] = _fold_head_rows(o, n_heads, t_new, head_lanes).astype(o_ref.dtype)


def _fox_sample_attn(q, k_cache, v_cache, k_new, v_new, cum_q, cum_all, *, n_heads, chunk):
    B, t_new, w = q.shape
    P = k_cache.shape[1]
    rows = n_heads * t_new
    cache = pl.BlockSpec((1, chunk, w), lambda b, c: (b, c, 0))
    new = pl.BlockSpec((1, t_new, w), lambda b, c: (b, 0, 0))
    return pl.pallas_call(
        functools.partial(_fox_sample_kernel, n_heads=n_heads, t_new=t_new),
        grid=(B, P // chunk),
        in_specs=[new, cache, cache, new, new,
                  pl.BlockSpec((1, rows, 1), lambda b, c: (b, 0, 0)),
                  pl.BlockSpec((1, n_heads, chunk), lambda b, c: (b, 0, c)),
                  pl.BlockSpec((1, n_heads, LANES), lambda b, c: (b, 0, P // LANES))],
        out_specs=new,
        out_shape=jax.ShapeDtypeStruct((B, t_new, w), BF16),
        scratch_shapes=[pltpu.VMEM((rows, w), BF16), pltpu.VMEM((rows, 1), F32),
                        pltpu.VMEM((rows, 1), F32), pltpu.VMEM((rows, w), F32)],
        compiler_params=_cparams("parallel", "arbitrary"),
        name="fox_attn_sample",
    )(q, k_cache, v_cache, k_new, v_new, cum_q, cum_all, cum_all)


def _mla_sample_kernel(q_ref, wuk_ref, ckv_ref, kr_ref, ckvn_ref, miscn_ref, wuv_ref, o_ref,
                       ql_ref, qr_ref, m_ref, l_ref, acc_ref, *, n_heads, t_new, nope, rope, past):
    c = pl.program_id(1)

    @pl.when(c == 0)
    def _():
        for hd in range(n_heads):
            qh = q_ref[0, :, hd * LANES:(hd + 1) * LANES]
            rows = slice(hd * t_new, (hd + 1) * t_new)
            ql_ref[rows, :] = _dot(qh, wuk_ref[hd]).astype(BF16)
            qr_ref[rows, :] = qh[:, nope:nope + rope]
        m_ref[...] = jnp.full_like(m_ref, NEG_INF)
        l_ref[...] = jnp.zeros_like(l_ref)
        acc_ref[...] = jnp.zeros_like(acc_ref)

    ql = ql_ref[...]
    qr = qr_ref[...]
    ckv = ckv_ref[0].astype(BF16)
    s = _dot_nt(ql, ckv) + _dot_nt(qr, kr_ref[0].astype(BF16))
    _softmax_step(s, ckv, m_ref, l_ref, acc_ref)

    @pl.when(c == pl.num_programs(1) - 1)
    def _():
        ckvn = ckvn_ref[0].astype(BF16)
        krn = miscn_ref[0][:, MISC_ROPE:MISC_ROPE + rope].astype(BF16)
        sn = _dot_nt(ql, ckvn) + _dot_nt(qr, krn)
        row = lax.broadcasted_iota(I32, sn.shape, 0)
        col = lax.broadcasted_iota(I32, sn.shape, 1)
        sn = jnp.where((past + col) // CHUNK <= (past + row % t_new) // CHUNK, sn, NEG_INF)
        _softmax_step(sn, ckvn, m_ref, l_ref, acc_ref)
        lat = (acc_ref[...] / l_ref[...]).astype(BF16)
        o = _dot(lat, wuv_ref[...])
        o_ref[0] = _fold_head_rows(o, n_heads, t_new, o.shape[1] // n_heads).astype(o_ref.dtype)


def _mla_sample_attn(qm, wukp, ckv_cache, kr_cache, ckv_new, misc_new, wuv, *, n_heads, chunk, nope, rope):
    B, t_new, qw = qm.shape
    P, lat = ckv_cache.shape[1:]
    rows = n_heads * t_new
    vw = wuv.shape[1]
    return pl.pallas_call(
        functools.partial(_mla_sample_kernel, n_heads=n_heads, t_new=t_new, nope=nope, rope=rope, past=P),
        grid=(B, P // chunk),
        in_specs=[pl.BlockSpec((1, t_new, qw), lambda b, c: (b, 0, 0)),
                  _const_spec(wukp.shape),
                  pl.BlockSpec((1, chunk, lat), lambda b, c: (b, c, 0)),
                  pl.BlockSpec((1, chunk, rope), lambda b, c: (b, c, 0)),
                  pl.BlockSpec((1, t_new, lat), lambda b, c: (b, 0, 0)),
                  pl.BlockSpec((1, t_new, LANES), lambda b, c: (b, 0, 0)),
                  _const_spec(wuv.shape)],
        out_specs=pl.BlockSpec((1, t_new, vw), lambda b, c: (b, 0, 0)),
        out_shape=jax.ShapeDtypeStruct((B, t_new, vw), BF16),
        scratch_shapes=[pltpu.VMEM((rows, lat), BF16), pltpu.VMEM((rows, rope), BF16),
                        pltpu.VMEM((rows, 1), F32), pltpu.VMEM((rows, 1), F32), pltpu.VMEM((rows, lat), F32)],
        compiler_params=_cparams("parallel", "arbitrary"),
        name="mla_attn_sample",
    )(qm, wukp, ckv_cache, kr_cache, ckv_new, misc_new, wuv)


def _mix_part(x, fo, mo, ga, gb, wof_ref, wom_ref, wout_ref, gx_ref, wxq_ref, x_scale):
    a = _dot(fo, wof_ref[...])
    b = _dot(mo, wom_ref[...])
    merged = (ga.astype(F32) * a + gb.astype(F32) * b).astype(BF16)
    x1 = x + _dot(merged, wout_ref[...])
    xq = (_dot(_rms(x1, gx_ref[...]).astype(BF16), wxq_ref[...]) * x_scale).astype(BF16)
    return x1, xq


def _cross_part(xq, mk, mv, x_heads):
    hd_w = xq.shape[1] // x_heads
    outs = []
    for hd in range(x_heads):
        sl = slice(hd * hd_w, (hd + 1) * hd_w)
        s = _dot_nt(xq[:, sl], mk[:, sl])
        pe = jnp.exp2(s - jnp.max(s, axis=1, keepdims=True))
        o = _dot(pe.astype(BF16), mv[:, sl]) / jnp.sum(pe, axis=1, keepdims=True)
        outs.append(o.astype(BF16))
    return jnp.concatenate(outs, axis=1)


def _route_part(x1, ca, wxo_ref, gffn_ref, wrt_ref, br_ref, carry_ref,
                x2_ref, hf_ref, idx_ref, rank_ref, gate_ref, cnt_ref):
    tm = x1.shape[0]
    n_exp = wrt_ref.shape[1]
    x2 = x1 + _dot(ca, wxo_ref[...])
    x2_ref[...] = x2
    hf = _rms(x2, gffn_ref[...])
    hf_ref[...] = hf
    hf_hi = hf.astype(BF16)
    hf_lo = (hf - hf_hi.astype(F32)).astype(BF16)
    logits = (_dot_nt(wrt_ref[0], hf_hi) + _dot_nt(wrt_ref[0], hf_lo) + _dot_nt(wrt_ref[1], hf_hi)
              + br_ref[...])
    erow = lax.broadcasted_iota(I32, (n_exp, tm), 0).astype(F32)
    picked = jnp.zeros((n_exp, tm), F32)
    vals, onehots, idxs = [], [], []
    for k in range(TOP_K):
        mx = jnp.max(logits, axis=0, keepdims=True)
        idx = jnp.min(jnp.where(logits == mx, erow, float(n_exp)), axis=0, keepdims=True)
        sel = erow == idx
        vals.append(mx)
        onehots.append(sel)
        idxs.append(idx)
        picked = picked + sel.astype(F32)
        logits = jnp.where(sel, -jnp.inf, logits)
    ex = [jnp.exp(v - vals[0]) for v in vals]
    den = ex[0] + ex[1] + ex[2] + ex[3]
    for k in range(TOP_K):
        gate_ref[k:k + 1, :] = ex[k] / den
    r = lax.broadcasted_iota(I32, (tm, tm), 0)
    cidx = lax.broadcasted_iota(I32, (tm, tm), 1)
    upper = jnp.where(r < cidx, 1.0, 0.0).astype(BF16)
    before = _dot(picked.astype(BF16), upper) + carry_ref[...]
    for k in range(TOP_K):
        rank = jnp.sum(jnp.where(onehots[k], before, 0.0), axis=0, keepdims=True)
        idx_ref[k:k + 1, :] = idxs[k].astype(I32)
        rank_ref[k:k + 1, :] = rank.astype(I32)
    carry_ref[...] = carry_ref[...] + jnp.sum(picked, axis=1, keepdims=True)
    cnt_ref[...] = jnp.broadcast_to(carry_ref[...], cnt_ref.shape).astype(I32)


def _post_attn_prompt_kernel(x_ref, fo_ref, mo_ref, ga_ref, gb_ref, mk_ref, mv_ref,
                             wof_ref, wom_ref, wout_ref, gx_ref, wxq_ref, wxo_ref, gffn_ref, wrt_ref, br_ref,
                             x2_ref, hf_ref, idx_ref, rank_ref, gate_ref, cnt_ref, carry_ref,
                             *, x_heads, x_scale):
    @pl.when(pl.program_id(0) == 0)
    def _():
        carry_ref[...] = jnp.zeros_like(carry_ref)

    x1, xq = _mix_part(x_ref[...], fo_ref[...], mo_ref[...], ga_ref[...], gb_ref[...],
                       wof_ref, wom_ref, wout_ref, gx_ref, wxq_ref, x_scale)
    ca = _cross_part(xq, mk_ref[0].astype(BF16), mv_ref[0].astype(BF16), x_heads)
    _route_part(x1, ca, wxo_ref, gffn_ref, wrt_ref, br_ref, carry_ref,
                x2_ref, hf_ref, idx_ref, rank_ref, gate_ref, cnt_ref)


def _route_out(T, D, n_exp, tm):
    tok = lambda n: pl.BlockSpec((tm, n), lambda i: (i, 0))
    col = pl.BlockSpec((TOP_K, tm), lambda i: (0, i))
    sds = jax.ShapeDtypeStruct
    shapes = [sds((T, D), F32), sds((T, D), F32), sds((TOP_K, T), I32), sds((TOP_K, T), I32),
              sds((TOP_K, T), F32), sds((n_exp, LANES), I32)]
    specs = [tok(D), tok(D), col, col, col, pl.BlockSpec((n_exp, LANES), lambda i: (0, 0))]
    return shapes, specs


def _post_attn_prompt(x, fo, mo, ga, gb, mk, mv, w, *, tm, seq):
    T, D = x.shape
    n_exp = w['wrt'].shape[1]
    tiles_per_seq = seq // tm
    tok = lambda n: pl.BlockSpec((tm, n), lambda i: (i, 0))
    mem = pl.BlockSpec((1,) + mk.shape[1:], lambda i: (i // tiles_per_seq, 0, 0))
    weights = [w['wof'], w['wom'], w['wout'], w['gx'], w['wxq'], w['wxo'], w['gffn'], w['wrt'], w['br']]
    shapes, specs = _route_out(T, D, n_exp, tm)
    return pl.pallas_call(
        functools.partial(_post_attn_prompt_kernel, x_heads=w['x_heads'], x_scale=w['x_scale']),
        grid=(T // tm,),
        in_specs=[tok(D), tok(fo.shape[1]), tok(mo.shape[1]), tok(D), tok(D), mem, mem]
        + [_const_spec(a.shape) for a in weights],
        out_specs=specs,
        out_shape=shapes,
        scratch_shapes=[pltpu.VMEM((n_exp, 1), F32)],
        compiler_params=_cparams("arbitrary"),
        name="post_attn_prompt",
    )(x, fo, mo, ga, gb, mk, mv, *weights)


def _mix_sample_kernel(x_ref, fo_ref, mo_ref, ga_ref, gb_ref, wof_ref, wom_ref, wout_ref, gx_ref, wxq_ref,
                       x1_ref, xq_ref, *, x_scale):
    x1, xq = _mix_part(x_ref[...], fo_ref[...], mo_ref[...], ga_ref[...], gb_ref[...],
                       wof_ref, wom_ref, wout_ref, gx_ref, wxq_ref, x_scale)
    x1_ref[...] = x1
    xq_ref[...] = xq


def _cross_sample_kernel(xq_ref, mk_ref, mv_ref, o_ref, *, x_heads):
    o_ref[0] = _cross_part(xq_ref[0], mk_ref[0].astype(BF16), mv_ref[0].astype(BF16), x_heads)


def _route_sample_kernel(x1_ref, ca_ref, wxo_ref, gffn_ref, wrt_ref, br_ref,
                         x2_ref, hf_ref, idx_ref, rank_ref, gate_ref, cnt_ref, carry_ref):
    carry_ref[...] = jnp.zeros_like(carry_ref)
    _route_part(x1_ref[...], ca_ref[...], wxo_ref, gffn_ref, wrt_ref, br_ref, carry_ref,
                x2_ref, hf_ref, idx_ref, rank_ref, gate_ref, cnt_ref)


def _post_attn_sample(x, fo, mo, ga, gb, mk, mv, w, *, t_new):
    T, D = x.shape
    B = T // t_new
    n_exp = w['wrt'].shape[1]
    xw = w['wxq'].shape[1]
    params = pltpu.CompilerParams(vmem_limit_bytes=VMEM_LIMIT_BYTES)
    x1, xq = pl.pallas_call(
        functools.partial(_mix_sample_kernel, x_scale=w['x_scale']),
        out_shape=[jax.ShapeDtypeStruct((T, D), F32), jax.ShapeDtypeStruct((T, xw), BF16)],
        compiler_params=params,
        name="mix_sample",
    )(x, fo, mo, ga, gb, w['wof'], w['wom'], w['wout'], w['gx'], w['wxq'])
    ca = pl.pallas_call(
        functools.partial(_cross_sample_kernel, x_heads=w['x_heads']),
        grid=(B,),
        in_specs=[pl.BlockSpec((1, t_new, xw), lambda b: (b, 0, 0)),
                  pl.BlockSpec((1,) + mk.shape[1:], lambda b: (b, 0, 0)),
                  pl.BlockSpec((1,) + mv.shape[1:], lambda b: (b, 0, 0))],
        out_specs=pl.BlockSpec((1, t_new, xw), lambda b: (b, 0, 0)),
        out_shape=jax.ShapeDtypeStruct((B, t_new, xw), BF16),
        compiler_params=_cparams("parallel"),
        name="cross_sample",
    )(xq.reshape(B, t_new, xw), mk, mv)
    shapes, _ = _route_out(T, D, n_exp, T)
    return pl.pallas_call(
        _route_sample_kernel,
        out_shape=shapes,
        scratch_shapes=[pltpu.VMEM((n_exp, 1), F32)],
        compiler_params=params,
        name="route_sample",
    )(x1, ca.reshape(T, xw), w['wxo'], w['gffn'], w['wrt'], w['br'])


def _mem_kv_kernel(m_ref, g_ref, wk_ref, wv_ref, k_ref, v_ref):
    m = _rms(m_ref[...], g_ref[...]).astype(BF16)
    k_ref[...] = _dot(m, wk_ref[...])
    v_ref[...] = _dot(m, wv_ref[...])


def _mem_kv(mem, w, *, tm):
    T, D = mem.shape
    xw = w['wxk'].shape[1]
    tok = lambda n: pl.BlockSpec((tm, n), lambda i: (i, 0))
    return pl.pallas_call(
        _mem_kv_kernel,
        grid=(T // tm,),
        in_specs=[tok(D), _const_spec(w['gmem'].shape), _const_spec(w['wxk'].shape), _const_spec(w['wxv'].shape)],
        out_specs=[tok(xw), tok(xw)],
        out_shape=[jax.ShapeDtypeStruct((T, xw), F32)] * 2,
        compiler_params=_cparams("parallel"),
        name="mem_kv",
    )(mem, w['gmem'], w['wxk'], w['wxv'])


def _dest_kernel(start_ref, idx_ref, rank_ref, dest_ref):
    idx = idx_ref[...]
    dest = rank_ref[...]
    for e in range(start_ref.shape[0]):
        dest = dest + jnp.where(idx == e, start_ref[e], 0)
    dest_ref[...] = dest


def _dest_rows(pad_start, idx_t, rank_t):
    whole = pl.BlockSpec(idx_t.shape, lambda i, s: (0, 0))
    return pl.pallas_call(
        _dest_kernel,
        grid_spec=pltpu.PrefetchScalarGridSpec(num_scalar_prefetch=1, grid=(1,), in_specs=[whole, whole],
                                               out_specs=whole),
        out_shape=jax.ShapeDtypeStruct(idx_t.shape, I32),
        name="moe_dest",
    )(pad_start, idx_t, rank_t)


def _row_copy(src_ref, src_row, dst_ref, dst_row, sem):
    return pltpu.make_async_copy(src_ref.at[pl.ds(src_row, 1), :], dst_ref.at[pl.ds(dst_row, 1), :], sem)


def _zero_unassigned_rows(cnt_ref, start_ref, nu_ref, xs_ref, zero_ref, sem, bm):
    zero_ref[...] = jnp.zeros_like(zero_ref)
    half = bm // 2
    sizes = [SUBLANES << b for b in range((half // SUBLANES).bit_length())]

    def zero_rows(base, n):
        cp = pltpu.make_async_copy(zero_ref.at[pl.ds(0, n)], xs_ref.at[pl.ds(pl.multiple_of(base, SUBLANES), n)], sem)
        cp.start()
        cp.wait()

    def tail_block(j, _):
        zero_rows(j * bm, half)
        zero_rows(j * bm + half, half)
        return 0

    lax.fori_loop(nu_ref[0], xs_ref.shape[0] // bm, tail_block, 0)

    def per_expert(e, _):
        cnt = cnt_ref[e]
        pad = (bm - (cnt & (bm - 1))) & (bm - 1)
        first = start_ref[e] + cnt
        n_single = jnp.minimum(pad, (SUBLANES - (first & (SUBLANES - 1))) & (SUBLANES - 1))

        def single(r, _):
            cp = _row_copy(zero_ref, 0, xs_ref, first + r, sem)
            cp.start()
            cp.wait()
            return 0

        lax.fori_loop(0, n_single, single, 0)
        base = first + n_single
        pad = pad - n_single
        for p in sizes:
            take = (pad & p) != 0

            @pl.when(take)
            def _(base=base, p=p):
                zero_rows(base, p)

            base = base + jnp.where(take, p, 0)
        return 0

    lax.fori_loop(0, cnt_ref.shape[0], per_expert, 0)


def _dispatch_kernel(cnt_ref, start_ref, nu_ref, dest_ref, h_ref, xs_ref, zero_ref, sem, zero_sem, *, bm):
    tm = h_ref.shape[0]

    @pl.when(pl.program_id(0) == 0)
    def _():
        _zero_unassigned_rows(cnt_ref, start_ref, nu_ref, xs_ref, zero_ref, zero_sem, bm)

    def issue(t, _):
        for k in range(TOP_K):
            _row_copy(h_ref, t, xs_ref, dest_ref[k, t], sem).start()
        return 0

    lax.fori_loop(0, tm, issue, 0, unroll=ROW_DMA_UNROLL)

    def drain(t, _):
        for k in range(TOP_K):
            _row_copy(h_ref, 0, xs_ref, 0, sem).wait()
        return 0

    lax.fori_loop(0, tm, drain, 0, unroll=2 * ROW_DMA_UNROLL)


def _dispatch(counts, pad_start, n_used, dest_t, h, n_rows, *, tm, bm):
    T, D = h.shape
    return pl.pallas_call(
        functools.partial(_dispatch_kernel, bm=bm),
        grid_spec=pltpu.PrefetchScalarGridSpec(
            num_scalar_prefetch=3, grid=(T // tm,),
            in_specs=[pl.BlockSpec((TOP_K, tm), lambda i, c, s, n: (0, i), memory_space=pltpu.SMEM),
                      pl.BlockSpec((tm, D), lambda i, c, s, n: (i, 0))],
            out_specs=pl.BlockSpec(memory_space=pl.ANY),
            scratch_shapes=[pltpu.VMEM((bm // 2, D), h.dtype), pltpu.SemaphoreType.DMA,
                            pltpu.SemaphoreType.DMA]),
        out_shape=jax.ShapeDtypeStruct((n_rows, D), h.dtype),
        compiler_params=_cparams("arbitrary"),
        name="moe_dispatch",
    )(counts, pad_start, n_used, dest_t, h)


def _expert_kernel(be_ref, nu_ref, x_ref, wgu_ref, bgu_ref, wd_ref, bd_ref, y_ref, *, ff):
    used = pl.program_id(0) < nu_ref[0]

    @pl.when(jnp.logical_not(used))
    def _():
        y_ref[...] = jnp.zeros_like(y_ref)

    @pl.when(used)
    def _():
        gu = _dot(x_ref[...].astype(BF16), wgu_ref[0]) + bgu_ref[0]
        gate = jnp.minimum(gu[:, :ff], SWIGLU_LIMIT)
        up = jnp.clip(gu[:, ff:], -SWIGLU_LIMIT, SWIGLU_LIMIT)
        act = (up + 1.0) * (gate * _sigmoid(gate * SWIGLU_ALPHA))
        y_ref[...] = _dot(act.astype(BF16), wd_ref[0]) + bd_ref[0]


def _expert_ffn(block_e, n_used, xs, w, *, bm):
    D = xs.shape[1]
    ff = w['wd'].shape[1]
    rows = pl.BlockSpec((bm, D), lambda i, be, nu: (i, 0))
    exp = lambda shape: pl.BlockSpec((1,) + shape, lambda i, be, nu: (be[i], 0, 0))
    return pl.pallas_call(
        functools.partial(_expert_kernel, ff=ff),
        grid_spec=pltpu.PrefetchScalarGridSpec(
            num_scalar_prefetch=2, grid=(xs.shape[0] // bm,),
            in_specs=[rows, exp((D, 2 * ff)), exp((1, 2 * ff)), exp((ff, D)), exp((1, D))],
            out_specs=rows),
        out_shape=jax.ShapeDtypeStruct(xs.shape, F32),
        compiler_params=_cparams("arbitrary"),
        name="moe_expert_ffn",
    )(block_e, n_used, xs, w['wgu'], w['bgu'], w['wd'], w['bd'])


def _combine_kernel(dest_ref, gate_ref, x2_ref, gfin_ref, yb_ref, y_ref, buf_ref, sem):
    tm = x2_ref.shape[0]

    def issue(t, _):
        for k in range(TOP_K):
            _row_copy(yb_ref, dest_ref[k, t], buf_ref.at[k], t, sem).start()
        return 0

    lax.fori_loop(0, tm, issue, 0, unroll=ROW_DMA_UNROLL)

    def drain(t, _):
        for k in range(TOP_K):
            _row_copy(yb_ref, 0, buf_ref.at[k], 0, sem).wait()
        return 0

    lax.fori_loop(0, tm, drain, 0, unroll=2 * ROW_DMA_UNROLL)
    gate = gate_ref[...]
    y = x2_ref[...]
    for k in range(TOP_K):
        y = y + gate[:, k:k + 1] * buf_ref[k]
    y_ref[...] = _rms(y, gfin_ref[...])


def _combine(dest_t, gate_tok, x2, gfin, yb, *, tm):
    T, D = x2.shape
    return pl.pallas_call(
        _combine_kernel,
        grid=(T // tm,),
        in_specs=[pl.BlockSpec((TOP_K, tm), lambda i: (0, i), memory_space=pltpu.SMEM),
                  pl.BlockSpec((tm, TOP_K), lambda i: (i, 0)),
                  pl.BlockSpec((tm, D), lambda i: (i, 0)),
                  _const_spec(gfin.shape),
                  pl.BlockSpec(memory_space=pl.ANY)],
        out_specs=pl.BlockSpec((tm, D), lambda i: (i, 0)),
        out_shape=jax.ShapeDtypeStruct((T, D), F32),
        scratch_shapes=[pltpu.VMEM((TOP_K, tm, D), F32), pltpu.SemaphoreType.DMA],
        compiler_params=_cparams("arbitrary"),
        name="moe_combine",
    )(dest_t, gate_tok, x2, gfin, yb)


def _moe_and_final_norm(x2, hf, idx_t, rank_t, gate_t, counts, w, gfin, *, bm, tm):
    T = x2.shape[0]
    n_exp = counts.shape[0]
    n_blocks = -(-(T * TOP_K + n_exp * (bm - 1)) // bm)
    padded = (counts + bm - 1) // bm * bm
    pad_end = jnp.cumsum(padded)
    pad_start = (pad_end - padded).astype(I32)
    n_used = (pad_end[n_exp - 1:] // bm).astype(I32)
    first_row = jnp.arange(n_blocks, dtype=I32) * bm
    block_e = jnp.minimum(jnp.sum(first_row[:, None] >= pad_end[None, :], axis=1), n_exp - 1).astype(I32)
    dest_t = _dest_rows(pad_start, idx_t, rank_t)
    xs = _dispatch(counts.astype(I32), pad_start, n_used, dest_t, hf, n_blocks * bm, tm=tm, bm=bm)
    yb = _expert_ffn(block_e, n_used, xs, w, bm=bm)
    return _combine(dest_t, gate_t.T, x2, gfin, yb, tm=tm)


def _rot_cols(wc):
    half = wc.shape[1] // 2
    return jnp.concatenate([-wc[:, half:], wc[:, :half]], axis=1)


def _prep_layer(p, dims):
    fw, nh, nope, rope, lat, vdim, x_heads, x_hd = dims
    D = p['w_in'].shape[0]
    w_in = p['w_in']
    o = 0
    wfox = w_in[:, o:o + 3 * fw]; o += 3 * fw
    wfl = w_in[:, o:o + nh]; o += nh
    qlora = p['g_q'].shape[0]
    wqc = w_in[:, o:o + qlora]; o += qlora
    wkvc = w_in[:, o:o + lat]; o += lat
    wkr = w_in[:, o:o + rope]; o += rope
    wga = w_in[:, o:o + D]; o += D
    wgb = w_in[:, o:o + D]

    def misc_cols(parts):
        out = jnp.zeros((D, LANES), F32)
        for off, cols in parts:
            out = out.at[:, off:off + cols.shape[1]].set(cols)
        return out

    wma = misc_cols([(MISC_LOGF, wfl), (MISC_CUM, wfl), (MISC_ROPE, wkr)])
    wmb = misc_cols([(MISC_ROPE, _rot_cols(wkr))])
    bf = jnp.zeros((1, LANES), F32).at[0, MISC_LOGF:MISC_LOGF + nh].set(p['b_f']).at[0, MISC_CUM:MISC_CUM + nh].set(p['b_f'])

    wuq = p['w_uq'].reshape(qlora, nh, nope + rope)
    wqa = jnp.zeros((qlora, nh, LANES), F32).at[:, :, :nope + rope].set(wuq)
    rot = jnp.concatenate([-wuq[:, :, nope + rope // 2:], wuq[:, :, nope:nope + rope // 2]], axis=2)
    wqb = jnp.zeros((qlora, nh, LANES), F32).at[:, :, nope:nope + rope].set(rot)
    wk = jnp.zeros((lat + LANES, nh, LANES), F32).at[:lat, :, :nope].set(p['w_uk'])
    place = jnp.zeros((LANES, nh, LANES), F32)
    j = jnp.arange(rope)
    place = place.at[MISC_ROPE + j, :, nope + j].set(1.0)
    wk = wk.at[lat:, :, :].set(place)
    wukp = jnp.zeros((nh, LANES, lat), F32).at[:, :nope, :].set(jnp.transpose(p['w_uk'], (1, 2, 0)))

    b = lambda a: a.astype(BF16)
    row = lambda a: a.reshape(1, -1).astype(F32)
    n_exp = p['w_router'].shape[1]
    wrt = p['w_router'].T.astype(F32)
    wrt_hi = b(wrt)
    wrt = jnp.stack([wrt_hi, b(wrt - wrt_hi.astype(F32))])
    return {
        'fox_width': fw, 'n_heads': nh, 'fox_scale': float(fw // nh) ** -0.5 * LOG2E,
        'x_heads': x_heads, 'x_scale': float(x_hd) ** -0.5 * LOG2E,
        'gmix': row(p['g_mix']), 'wfox': b(wfox), 'wma': b(wma), 'wmb': b(wmb), 'wqc': b(wqc), 'wkvc': b(wkvc),
        'wga': b(wga), 'wgb': b(wgb), 'bf': bf, 'gq': row(p['g_q']),
        'wqa': b(wqa.reshape(qlora, nh * LANES)), 'wqb': b(wqb.reshape(qlora, nh * LANES)),
        'gkv': row(p['g_kv']), 'wk': b(wk.reshape(lat + LANES, nh * LANES)),
        'wuv': b(p['w_uv'].reshape(lat, nh * vdim)), 'wukp': b(wukp),
        'wof': b(p['w_o_fox']), 'wom': b(p['w_o_mla']), 'wout': b(p['w_out']), 'gx': row(p['g_x']),
        'wxq': b(p['w_xq']), 'wxo': b(p['w_xo']), 'gffn': row(p['g_ffn']),
        'wrt': wrt, 'br': p['b_router'].reshape(n_exp, 1).astype(F32),
        'gmem': row(p['g_mem']), 'wxk': b(p['w_xk']), 'wxv': b(p['w_xv']),
        'wgu': b(p['w_gu']), 'bgu': p['b_gu'].reshape(n_exp, 1, -1).astype(F32),
        'wd': b(p['w_down']), 'bd': p['b_down'].reshape(n_exp, 1, -1).astype(F32),
    }


def _rope_tables(pos, nope, rope, q_scale):
    half = rope // 2
    inv_freq = ROPE_BASE ** (-jnp.arange(half, dtype=F32) / half)
    ang = pos.astype(F32)[:, None] * inv_freq[None, :]
    cos = jnp.concatenate([jnp.cos(ang)] * 2, axis=1)
    sin = jnp.concatenate([jnp.sin(ang)] * 2, axis=1)
    n = pos.shape[0]
    ck = jnp.zeros((n, LANES), F32).at[:, :MISC_ROPE].set(1.0).at[:, MISC_ROPE:MISC_ROPE + rope].set(cos)
    sk = jnp.zeros((n, LANES), F32).at[:, MISC_ROPE:MISC_ROPE + rope].set(sin)
    cq = jnp.zeros((n, LANES), F32).at[:, :nope].set(1.0).at[:, nope:nope + rope].set(cos) * q_scale
    sq = jnp.zeros((n, LANES), F32).at[:, nope:nope + rope].set(sin) * q_scale
    return ck, sk, cq, sq


def _pick_tile(n, target):
    t = min(n, target)
    while n % t:
        t //= 2
    return t


def _prompt_layer(x, mem, w, dims):
    fw, nh, nope, rope, lat, vdim, x_heads, x_hd = dims
    B, S, D = x.shape
    T = B * S
    tm = _pick_tile(S, TOKEN_TILE)
    tables = _rope_tables(jnp.arange(S, dtype=I32), nope, rope, float(nope + rope) ** -0.5 * LOG2E)
    fq, fk, fv, fkb, fvb, misc, ckv, qm, ga, gb, km, vm = _in_proj(
        x.reshape(T, D), w, tables, tm=tm, tiles_per_seq=S // tm, prompt=True)

    blk = _pick_tile(S, ATTN_BLOCK)
    cum = misc[:, MISC_CUM:MISC_CUM + nh].reshape(B, S // blk, blk, nh)
    cum_row = jnp.swapaxes(cum, 2, 3)
    r3 = lambda a: a.reshape(B, S, a.shape[1])
    fox_o = _prompt_attn(r3(fq), r3(fkb), r3(fvb), r3(misc), cum_row, blk=blk, n_heads=nh, fox=True)
    mla_o = _prompt_attn(r3(qm), r3(km), r3(vm), None, None, blk=blk, n_heads=nh, fox=False)

    n_mem = mem.shape[1]
    mk, mv = _mem_kv(mem.reshape(B * n_mem, D), w, tm=_pick_tile(B * n_mem, 512))
    xw = mk.shape[1]
    routed = _post_attn_prompt(
        x.reshape(T, D), fox_o.reshape(T, fw), mla_o.reshape(T, nh * vdim), ga, gb,
        mk.reshape(B, n_mem, xw), mv.reshape(B, n_mem, xw), w, tm=tm, seq=S)
    caches = (fk.reshape(B, S, nh, fw // nh), fv.reshape(B, S, nh, fw // nh),
              misc[:, MISC_LOGF:MISC_LOGF + nh].reshape(B, S, nh), ckv.reshape(B, S, lat),
              misc[:, MISC_ROPE:MISC_ROPE + rope].reshape(B, S, rope),
              mk.reshape(B, n_mem, x_heads, x_hd), mv.reshape(B, n_mem, x_heads, x_hd))
    return routed, caches


def _sample_layer(x, c_fk, c_fv, c_logf, c_ckv, c_kr, c_mk, c_mv, w, dims):
    fw, nh, nope, rope, lat, vdim, x_heads, x_hd = dims
    B, t_new, D = x.shape
    P = c_fk.shape[1]
    T = B * t_new
    pos = P + jnp.tile(jnp.arange(t_new, dtype=I32), B)
    tables = _rope_tables(pos, nope, rope, float(nope + rope) ** -0.5 * LOG2E)
    fq, fk, fv, fkb, fvb, misc, ckv, qm, ga, gb = _in_proj(
        x.reshape(T, D), w, tables, tm=T, tiles_per_seq=1, prompt=False)

    logf_new = misc[:, MISC_LOGF:MISC_LOGF + nh].reshape(B, t_new, nh)
    lf = jnp.concatenate([c_logf.astype(F32), logf_new, jnp.zeros((B, LANES - t_new, nh), F32)], axis=1)
    cum_all = _lane_cumsum(jnp.swapaxes(lf, 1, 2).reshape(B * nh, P + LANES)).reshape(B, nh, P + LANES)
    cum_q = cum_all[:, :, P:P + t_new].reshape(B, nh * t_new, 1)

    chunk = _pick_tile(P, CACHE_CHUNK)
    r3 = lambda a: a.reshape(B, t_new, a.shape[1])
    fox_o = _fox_sample_attn(r3(fq), c_fk.reshape(B, P, fw).astype(BF16), c_fv.reshape(B, P, fw).astype(BF16),
                             r3(fkb), r3(fvb), cum_q, cum_all, n_heads=nh, chunk=chunk)
    mla_o = _mla_sample_attn(r3(qm), w['wukp'], c_ckv, c_kr, r3(ckv), r3(misc), w['wuv'],
                             n_heads=nh, chunk=chunk, nope=nope, rope=rope)
    n_mem = c_mk.shape[1]
    routed = _post_attn_sample(
        x.reshape(T, D), fox_o.reshape(T, fw), mla_o.reshape(T, nh * vdim), ga, gb,
        c_mk.reshape(B, n_mem, x_heads * x_hd), c_mv.reshape(B, n_mem, x_heads * x_hd), w, t_new=t_new)
    caches = (fk.reshape(B, t_new, nh, fw // nh), fv.reshape(B, t_new, nh, fw // nh), logf_new,
              ckv.reshape(B, t_new, lat), misc[:, MISC_ROPE:MISC_ROPE + rope].reshape(B, t_new, rope))
    return routed, caches


def kernel(x_prompt, x_sample, mem_prompt, cache_fox_k, cache_fox_v, cache_fox_logf, cache_mla_ckv,
           cache_mla_krope, cache_mem_k, cache_mem_v, g_mix, w_in, b_f, g_q, w_uq, g_kv, w_uk, w_uv,
           w_o_fox, w_o_mla, w_out, g_x, g_mem, w_xq, w_xk, w_xv, w_xo, g_ffn, w_router, b_router,
           w_gu, b_gu, w_down, b_down, g_final):
    depth = w_in.shape[0]
    assert depth == 1, "the fused final norm assumes a single layer"
    nh, fhd = cache_fox_k.shape[3:]
    lat = cache_mla_ckv.shape[3]
    rope = cache_mla_krope.shape[3]
    nope, vdim = w_uk.shape[3], w_uv.shape[3]
    x_heads, x_hd = cache_mem_k.shape[3:]
    dims = (nh * fhd, nh, nope, rope, lat, vdim, x_heads, x_hd)
    B, S, D = x_prompt.shape
    Bs, t_new, _ = x_sample.shape
    gfin = g_final.reshape(1, D).astype(F32)

    l = 0
    p = {'g_mix': g_mix[l], 'w_in': w_in[l], 'b_f': b_f[l], 'g_q': g_q[l], 'w_uq': w_uq[l],
         'g_kv': g_kv[l], 'w_uk': w_uk[l], 'w_uv': w_uv[l], 'w_o_fox': w_o_fox[l],
         'w_o_mla': w_o_mla[l], 'w_out': w_out[l], 'g_x': g_x[l], 'g_mem': g_mem[l],
         'w_xq': w_xq[l], 'w_xk': w_xk[l], 'w_xv': w_xv[l], 'w_xo': w_xo[l], 'g_ffn': g_ffn[l],
         'w_router': w_router[l], 'b_router': b_router[l], 'w_gu': w_gu[l], 'b_gu': b_gu[l],
         'w_down': w_down[l], 'b_down': b_down[l]}
    w = _prep_layer(p, dims)

    (*routed_p, cnt_p), pc = _prompt_layer(x_prompt, mem_prompt, w, dims)
    (*routed_s, cnt_s), sc = _sample_layer(
        x_sample, cache_fox_k[l], cache_fox_v[l], cache_fox_logf[l], cache_mla_ckv[l],
        cache_mla_krope[l], cache_mem_k[l], cache_mem_v[l], w, dims)
    y_prompt = _moe_and_final_norm(*routed_p, cnt_p[:, 0], w, gfin,
                                   bm=EXPERT_ROWS_PROMPT, tm=TOKEN_TILE).reshape(B, S, D)
    y_sample = _moe_and_final_norm(*routed_s, cnt_s[:, 0], w, gfin, bm=EXPERT_ROWS_SAMPLE,
                                   tm=_pick_tile(Bs * t_new, TOKEN_TILE)).reshape(Bs, t_new, D)
    return (y_prompt, y_sample) + tuple(a[None] for a in pc) + tuple(a[None] for a in sc)
```

```python
import functools
import math

import jax
import jax.numpy as jnp
from jax import lax
from jax.experimental import pallas as pl
from jax.experimental.pallas import tpu as pltpu

F32 = jnp.float32
BF16 = jnp.bfloat16
I32 = jnp.int32

CHUNK = 64
EPS = 1e-6
NEG_INF = -1e30
ROPE_BASE = 10000.0
TOP_K = 4
SWIGLU_LIMIT = 7.0
SWIGLU_ALPHA = 1.702
LOG2E = math.log2(math.e)

LANES = 128
SUBLANES = 8
VMEM_LIMIT_BYTES = 56 * 1024 * 1024

MISC_LOGF = 0
MISC_CUM = 8
MISC_ROPE = 16

PROJ_TILE = 512
TOKEN_TILE = 256
ATTN_BLOCK = 256
CACHE_CHUNK = 1024
EXPERT_ROWS_PROMPT = 512
EXPERT_ROWS_SAMPLE = 128
ROW_DMA_UNROLL = 4


def _cparams(*sem):
    return pltpu.CompilerParams(dimension_semantics=sem, vmem_limit_bytes=VMEM_LIMIT_BYTES)


def _const_spec(shape):
    nd = len(shape)
    return pl.BlockSpec(shape, lambda *_: (0,) * nd, pipeline_mode=pl.Buffered(1))


def _rms(x, g):
    return x * lax.rsqrt(jnp.mean(x * x, axis=-1, keepdims=True) + EPS) * g


def _sigmoid(x):
    return 1.0 / (1.0 + jnp.exp(-x))


def _log_sigmoid(x):
    return jnp.minimum(x, 0.0) - jnp.log1p(jnp.exp(-jnp.abs(x)))


def _prefix_sum(c, axis):
    n = c.shape[axis]
    pos = lax.broadcasted_iota(I32, c.shape, axis)
    s = 1
    while s < n:
        c = c + jnp.where(pos >= s, pltpu.roll(c, s, axis), 0.0)
        s *= 2
    return c


def _dot(a, b):
    return jnp.dot(a, b, preferred_element_type=F32)


def _dot_nt(a, b):
    return lax.dot_general(a, b, (((1,), (1,)), ((), ())), preferred_element_type=F32)


def _tile_lanes(t, n):
    return jnp.concatenate([t] * n, axis=1)


def _in_proj_kernel(x_ref, gmix_ref, wfox_ref, wma_ref, wmb_ref, wqc_ref, wkvc_ref, wga_ref, wgb_ref,
                    bf_ref, gq_ref, wqa_ref, wqb_ref, gkv_ref, wk_ref, wuv_ref,
                    ck_ref, sk_ref, cq_ref, sq_ref,
                    fq_ref, fk_ref, fv_ref, fkb_ref, fvb_ref, misc_ref, ckv_ref, qm_ref, ga_ref, gb_ref, *rest,
                    fox_width, n_heads, fox_scale, tiles_per_seq, prompt):
    if prompt:
        km_ref, vm_ref, carry_ref = rest
    tm = x_ref.shape[0]
    h = _rms(x_ref[...], gmix_ref[...]).astype(BF16)

    fq_ref[...] = (_dot(h, wfox_ref[:, 0:fox_width]) * fox_scale).astype(BF16)
    fk = _dot(h, wfox_ref[:, fox_width:2 * fox_width])
    fv = _dot(h, wfox_ref[:, 2 * fox_width:3 * fox_width])
    fk_ref[...] = fk
    fv_ref[...] = fv
    fkb_ref[...] = fk.astype(BF16)
    fvb_ref[...] = fv.astype(BF16)

    lane = lax.broadcasted_iota(I32, (tm, LANES), 1)
    pre = _dot(h, wma_ref[...]) * ck_ref[...] + _dot(h, wmb_ref[...]) * sk_ref[...]
    misc = jnp.where(lane < MISC_ROPE, _log_sigmoid(pre + bf_ref[...]), pre)
    if prompt:
        i = pl.program_id(0)

        @pl.when(i % tiles_per_seq == 0)
        def _():
            carry_ref[...] = jnp.zeros_like(carry_ref)

        in_cum = (lane >= MISC_CUM) & (lane < MISC_ROPE)
        cum = _prefix_sum(jnp.where(in_cum, misc * LOG2E, 0.0), 0) + carry_ref[...]
        carry_ref[...] = cum[tm - 1:tm, :]
        misc = jnp.where(in_cum, cum, misc)
    misc_ref[...] = misc

    ckv = _rms(_dot(h, wkvc_ref[...]), gkv_ref[...])
    ckv_ref[...] = ckv

    qn = _rms(_dot(h, wqc_ref[...]), gq_ref[...]).astype(BF16)
    cq = _tile_lanes(cq_ref[...], n_heads)
    sq = _tile_lanes(sq_ref[...], n_heads)
    qm_ref[...] = (_dot(qn, wqa_ref[...]) * cq + _dot(qn, wqb_ref[...]) * sq).astype(BF16)

    if prompt:
        ckv_b = ckv.astype(BF16)
        km_ref[...] = _dot(jnp.concatenate([ckv_b, misc.astype(BF16)], axis=1), wk_ref[...]).astype(BF16)
        vm_ref[...] = _dot(ckv_b, wuv_ref[...]).astype(BF16)

    ga_ref[...] = _sigmoid(_dot(h, wga_ref[...])).astype(BF16)
    gb_ref[...] = _sigmoid(_dot(h, wgb_ref[...])).astype(BF16)


def _in_proj(x, w, tables, *, tm, tiles_per_seq, prompt):
    T, D = x.shape
    fw = w['fox_width']
    nh = w['n_heads']
    lat = w['wkvc'].shape[1]
    hw = nh * LANES
    ck, sk, cq, sq = tables
    tok = lambda n: pl.BlockSpec((tm, n), lambda i: (i, 0))
    if prompt:
        tab = pl.BlockSpec((tm, LANES), lambda i: (i % tiles_per_seq, 0))
    else:
        tab = tok(LANES)
    weights = [w['gmix'], w['wfox'], w['wma'], w['wmb'], w['wqc'], w['wkvc'], w['wga'], w['wgb'],
               w['bf'], w['gq'], w['wqa'], w['wqb'], w['gkv'], w['wk'], w['wuv']]
    sds = jax.ShapeDtypeStruct
    out_shape = [sds((T, fw), BF16), sds((T, fw), F32), sds((T, fw), F32),
                 sds((T, fw), BF16), sds((T, fw), BF16), sds((T, LANES), F32),
                 sds((T, lat), F32), sds((T, hw), BF16), sds((T, D), BF16), sds((T, D), BF16)]
    out_specs = [tok(fw), tok(fw), tok(fw), tok(fw), tok(fw), tok(LANES), tok(lat), tok(hw), tok(D), tok(D)]
    scratch = []
    if prompt:
        out_shape += [sds((T, hw), BF16), sds((T, w['wuv'].shape[1]), BF16)]
        out_specs += [tok(hw), tok(w['wuv'].shape[1])]
        scratch = [pltpu.VMEM((1, LANES), F32)]
    kern = functools.partial(_in_proj_kernel, fox_width=fw, n_heads=nh, fox_scale=w['fox_scale'],
                             tiles_per_seq=tiles_per_seq, prompt=prompt)
    return pl.pallas_call(
        kern,
        grid=(T // tm,),
        in_specs=[tok(D)] + [_const_spec(a.shape) for a in weights] + [tab] * 4,
        out_specs=out_specs,
        out_shape=out_shape,
        scratch_shapes=scratch,
        compiler_params=_cparams("arbitrary"),
        name="in_proj_prompt" if prompt else "in_proj_sample",
    )(x, *weights, ck, sk, cq, sq)


def _fold_lanes(x, op):
    out = x[:, 0:LANES]
    for j in range(1, x.shape[1] // LANES):
        out = op(out, x[:, j * LANES:(j + 1) * LANES])
    return out


def _prompt_attn_kernel(*refs, blk, n_heads, fox):
    if fox:
        q_ref, k_ref, v_ref, ccol_ref, crow_ref, o_ref, qs_ref, cq_ref, s_ref, m_ref, l_ref, acc_ref = refs
    else:
        q_ref, k_ref, v_ref, o_ref, s_ref, m_ref, l_ref, acc_ref = refs
    qi = pl.program_id(1)
    n_pairs = n_heads // 2
    half = LANES // 2
    lane = lax.broadcasted_iota(I32, (blk, LANES), 1)

    m_ref[...] = jnp.full_like(m_ref, NEG_INF)
    if fox:
        zero = jnp.zeros((), BF16)
        for p in range(n_pairs):
            q2 = q_ref[0, :, p * LANES:(p + 1) * LANES]
            qs_ref[p, 0:blk, :] = jnp.where(lane < half, q2, zero)
            qs_ref[p, blk:2 * blk, :] = jnp.where(lane >= half, q2, zero)
            for hh in range(2):
                col = MISC_CUM + 2 * p + hh
                cq_ref[p, hh * blk:(hh + 1) * blk, :] = jnp.broadcast_to(ccol_ref[0, :, col:col + 1], (blk, LANES))

    def scores(kb, diagonal):
        ks = pl.multiple_of(kb * blk, blk)
        for p in range(n_pairs):
            if fox:
                s = _dot_nt(qs_ref[p], k_ref[0, pl.ds(ks, blk), p * LANES:(p + 1) * LANES])
                ck = jnp.concatenate([jnp.broadcast_to(crow_ref[0, kb, 2 * p + hh:2 * p + hh + 1, :], (blk, blk))
                                      for hh in range(2)], axis=0)
                s = s + (_tile_lanes(cq_ref[p], blk // LANES) - ck)
            else:
                s = jnp.concatenate(
                    [_dot_nt(q_ref[0, :, hd * LANES:(hd + 1) * LANES], k_ref[0, pl.ds(ks, blk), hd * LANES:(hd + 1) * LANES])
                     for hd in (2 * p, 2 * p + 1)], axis=0)
            if diagonal:
                r = lax.broadcasted_iota(I32, (2 * blk, blk), 0)
                r = jnp.where(r >= blk, r - blk, r)
                c = lax.broadcasted_iota(I32, (2 * blk, blk), 1)
                mask = (c <= r) if fox else ((c // CHUNK) <= (r // CHUNK))
                s = jnp.where(mask, s, NEG_INF)
            s_ref[p, kb] = s
            m_ref[p] = jnp.maximum(m_ref[p], _fold_lanes(s, jnp.maximum))

    def score_body(kb, carry):
        scores(kb, False)
        return carry

    lax.fori_loop(0, qi, score_body, 0)
    scores(qi, True)

    for p in range(n_pairs):
        m_ref[p] = jnp.broadcast_to(jnp.max(m_ref[p], axis=1, keepdims=True), (2 * blk, LANES))
    l_ref[...] = jnp.zeros_like(l_ref)
    acc_ref[...] = jnp.zeros_like(acc_ref)

    def weigh_body(kb, carry):
        ks = pl.multiple_of(kb * blk, blk)
        for p in range(n_pairs):
            pe = jnp.exp2(s_ref[p, kb] - _tile_lanes(m_ref[p], blk // LANES))
            l_ref[p] = l_ref[p] + _fold_lanes(pe, jnp.add)
            acc_ref[p] = acc_ref[p] + _dot(pe.astype(BF16), v_ref[0, pl.ds(ks, blk), p * LANES:(p + 1) * LANES])
        return carry

    lax.fori_loop(0, qi + 1, weigh_body, 0)
    for p in range(n_pairs):
        o = acc_ref[p] / jnp.sum(l_ref[p], axis=1, keepdims=True)
        o_ref[0, :, p * LANES:(p + 1) * LANES] = jnp.where(lane < half, o[0:blk], o[blk:2 * blk]).astype(o_ref.dtype)


def _prompt_attn(q, k, v, cum_col, cum_row, *, blk, n_heads, fox):
    B, S, qw = q.shape
    vw = v.shape[2]
    n_pairs = n_heads // 2
    in_specs = [pl.BlockSpec((1, blk, qw), lambda b, i: (b, i, 0)),
                pl.BlockSpec((1, S, k.shape[2]), lambda b, i: (b, 0, 0)),
                pl.BlockSpec((1, S, vw), lambda b, i: (b, 0, 0))]
    args = [q, k, v]
    stat = pltpu.VMEM((n_pairs, 2 * blk, LANES), F32)
    scratch = []
    if fox:
        in_specs += [pl.BlockSpec((1, blk, LANES), lambda b, i: (b, i, 0)),
                     pl.BlockSpec((1,) + cum_row.shape[1:], lambda b, i: (b, 0, 0, 0))]
        args += [cum_col, cum_row]
        scratch = [pltpu.VMEM((n_pairs, 2 * blk, LANES), BF16), stat]
    scratch += [pltpu.VMEM((n_pairs, S // blk, 2 * blk, blk), F32), stat, stat, stat]
    return pl.pallas_call(
        functools.partial(_prompt_attn_kernel, blk=blk, n_heads=n_heads, fox=fox),
        grid=(B, S // blk),
        in_specs=in_specs,
        out_specs=pl.BlockSpec((1, blk, vw), lambda b, i: (b, i, 0)),
        out_shape=jax.ShapeDtypeStruct((B, S, vw), BF16),
        scratch_shapes=scratch,
        compiler_params=_cparams("parallel", "parallel"),
        name="fox_attn_prompt" if fox else "mla_attn_prompt",
    )(*args)


def _lane_cumsum_kernel(x_ref, o_ref):
    o_ref[...] = _prefix_sum(x_ref[...], 1) * LOG2E


def _lane_cumsum(x):
    return pl.pallas_call(
        _lane_cumsum_kernel,
        out_shape=jax.ShapeDtypeStruct(x.shape, F32),
        compiler_params=pltpu.CompilerParams(vmem_limit_bytes=VMEM_LIMIT_BYTES),
        name="logf_cumsum_sample",
    )(x)


def _head_rows(x, n_heads, head_lanes):
    t, w = x.shape
    xt = jnp.concatenate([x] * n_heads, axis=0)
    row = lax.broadcasted_iota(I32, xt.shape, 0)
    lane = lax.broadcasted_iota(I32, xt.shape, 1)
    return jnp.where(lane // head_lanes == row // t, xt, jnp.zeros((), x.dtype))


def _fold_head_rows(o, n_heads, t, head_lanes):
    row = lax.broadcasted_iota(I32, o.shape, 0)
    lane = lax.broadcasted_iota(I32, o.shape, 1)
    o = jnp.where(lane // head_lanes == row // t, o, 0.0)
    out = o[0:t]
    for hd in range(1, n_heads):
        out = out + o[hd * t:(hd + 1) * t]
    return out


def _repeat_rows(x, t):
    return jnp.concatenate([jnp.broadcast_to(x[r:r + 1], (t, x.shape[1])) for r in range(x.shape[0])], axis=0)


def _softmax_step(s, v, m_ref, l_ref, acc_ref):
    m = m_ref[...]
    m_new = jnp.maximum(m, jnp.max(s, axis=1, keepdims=True))
    alpha = jnp.exp2(m - m_new)
    pe = jnp.exp2(s - m_new)
    l_ref[...] = alpha * l_ref[...] + jnp.sum(pe, axis=1, keepdims=True)
    acc_ref[...] = alpha * acc_ref[...] + _dot(pe.astype(BF16), v)
    m_ref[...] = m_new


def _fox_sample_kernel(q_ref, k_ref, v_ref, kn_ref, vn_ref, cq_ref, ck_ref, ckn_ref, o_ref,
                       qbd_ref, m_ref, l_ref, acc_ref, *, n_heads, t_new):
    c = pl.program_id(1)
    head_lanes = q_ref.shape[2] // n_heads

    @pl.when(c == 0)
    def _():
        qbd_ref[...] = _head_rows(q_ref[0], n_heads, head_lanes)
        m_ref[...] = jnp.full_like(m_ref, NEG_INF)
        l_ref[...] = jnp.zeros_like(l_ref)
        acc_ref[...] = jnp.zeros_like(acc_ref)

    qbd = qbd_ref[...]
    cq = cq_ref[0]
    s = _dot_nt(qbd, k_ref[0]) + cq - _repeat_rows(ck_ref[0], t_new)
    _softmax_step(s, v_ref[0], m_ref, l_ref, acc_ref)

    @pl.when(c == pl.num_programs(1) - 1)
    def _():
        sn = _dot_nt(qbd, kn_ref[0]) + cq - _repeat_rows(ckn_ref[0][:, 0:t_new], t_new)
        row = lax.broadcasted_iota(I32, sn.shape, 0)
        col = lax.broadcasted_iota(I32, sn.shape, 1)
        sn = jnp.where(col <= row % t_new, sn, NEG_INF)
        _softmax_step(sn, vn_ref[0], m_ref, l_ref, acc_ref)
        o = acc_ref[...] / l_ref[...]
        o_ref[0] = _fold_head_rows(o, n_heads, t_new, head_lanes).astype(o_ref.dtype)


def _fox_sample_attn(q, k_cache, v_cache, k_new, v_new, cum_q, cum_all, *, n_heads, chunk):
    B, t_new, w = q.shape
    P = k_cache.shape[1]
    rows = n_heads * t_new
    cache = pl.BlockSpec((1, chunk, w), lambda b, c: (b, c, 0))
    new = pl.BlockSpec((1, t_new, w), lambda b, c: (b, 0, 0))
    return pl.pallas_call(
        functools.partial(_fox_sample_kernel, n_heads=n_heads, t_new=t_new),
        grid=(B, P // chunk),
        in_specs=[new, cache, cache, new, new,
                  pl.BlockSpec((1, rows, 1), lambda b, c: (b, 0, 0)),
                  pl.BlockSpec((1, n_heads, chunk), lambda b, c: (b, 0, c)),
                  pl.BlockSpec((1, n_heads, LANES), lambda b, c: (b, 0, P // LANES))],
        out_specs=new,
        out_shape=jax.ShapeDtypeStruct((B, t_new, w), BF16),
        scratch_shapes=[pltpu.VMEM((rows, w), BF16), pltpu.VMEM((rows, 1), F32),
                        pltpu.VMEM((rows, 1), F32), pltpu.VMEM((rows, w), F32)],
        compiler_params=_cparams("parallel", "arbitrary"),
        name="fox_attn_sample",
    )(q, k_cache, v_cache, k_new, v_new, cum_q, cum_all, cum_all)


def _mla_sample_kernel(q_ref, wuk_ref, ckv_ref, kr_ref, ckvn_ref, miscn_ref, wuv_ref, o_ref,
                       ql_ref, qr_ref, m_ref, l_ref, acc_ref, *, n_heads, t_new, nope, rope, past):
    c = pl.program_id(1)

    @pl.when(c == 0)
    def _():
        for hd in range(n_heads):
            qh = q_ref[0, :, hd * LANES:(hd + 1) * LANES]
            rows = slice(hd * t_new, (hd + 1) * t_new)
            ql_ref[rows, :] = _dot(qh, wuk_ref[hd]).astype(BF16)
            qr_ref[rows, :] = qh[:, nope:nope + rope]
        m_ref[...] = jnp.full_like(m_ref, NEG_INF)
        l_ref[...] = jnp.zeros_like(l_ref)
        acc_ref[...] = jnp.zeros_like(acc_ref)

    ql = ql_ref[...]
    qr = qr_ref[...]
    ckv = ckv_ref[0].astype(BF16)
    s = _dot_nt(ql, ckv) + _dot_nt(qr, kr_ref[0].astype(BF16))
    _softmax_step(s, ckv, m_ref, l_ref, acc_ref)

    @pl.when(c == pl.num_programs(1) - 1)
    def _():
        ckvn = ckvn_ref[0].astype(BF16)
        krn = miscn_ref[0][:, MISC_ROPE:MISC_ROPE + rope].astype(BF16)
        sn = _dot_nt(ql, ckvn) + _dot_nt(qr, krn)
        row = lax.broadcasted_iota(I32, sn.shape, 0)
        col = lax.broadcasted_iota(I32, sn.shape, 1)
        sn = jnp.where((past + col) // CHUNK <= (past + row % t_new) // CHUNK, sn, NEG_INF)
        _softmax_step(sn, ckvn, m_ref, l_ref, acc_ref)
        lat = (acc_ref[...] / l_ref[...]).astype(BF16)
        o = _dot(lat, wuv_ref[...])
        o_ref[0] = _fold_head_rows(o, n_heads, t_new, o.shape[1] // n_heads).astype(o_ref.dtype)


def _mla_sample_attn(qm, wukp, ckv_cache, kr_cache, ckv_new, misc_new, wuv, *, n_heads, chunk, nope, rope):
    B, t_new, qw = qm.shape
    P, lat = ckv_cache.shape[1:]
    rows = n_heads * t_new
    vw = wuv.shape[1]
    return pl.pallas_call(
        functools.partial(_mla_sample_kernel, n_heads=n_heads, t_new=t_new, nope=nope, rope=rope, past=P),
        grid=(B, P // chunk),
        in_specs=[pl.BlockSpec((1, t_new, qw), lambda b, c: (b, 0, 0)),
                  _const_spec(wukp.shape),
                  pl.BlockSpec((1, chunk, lat), lambda b, c: (b, c, 0)),
                  pl.BlockSpec((1, chunk, rope), lambda b, c: (b, c, 0)),
                  pl.BlockSpec((1, t_new, lat), lambda b, c: (b, 0, 0)),
                  pl.BlockSpec((1, t_new, LANES), lambda b, c: (b, 0, 0)),
                  _const_spec(wuv.shape)],
        out_specs=pl.BlockSpec((1, t_new, vw), lambda b, c: (b, 0, 0)),
        out_shape=jax.ShapeDtypeStruct((B, t_new, vw), BF16),
        scratch_shapes=[pltpu.VMEM((rows, lat), BF16), pltpu.VMEM((rows, rope), BF16),
                        pltpu.VMEM((rows, 1), F32), pltpu.VMEM((rows, 1), F32), pltpu.VMEM((rows, lat), F32)],
        compiler_params=_cparams("parallel", "arbitrary"),
        name="mla_attn_sample",
    )(qm, wukp, ckv_cache, kr_cache, ckv_new, misc_new, wuv)


def _mix_part(x, fo, mo, ga, gb, wof_ref, wom_ref, wout_ref, gx_ref, wxq_ref, x_scale):
    a = _dot(fo, wof_ref[...])
    b = _dot(mo, wom_ref[...])
    merged = (ga.astype(F32) * a + gb.astype(F32) * b).astype(BF16)
    x1 = x + _dot(merged, wout_ref[...])
    xq = (_dot(_rms(x1, gx_ref[...]).astype(BF16), wxq_ref[...]) * x_scale).astype(BF16)
    return x1, xq


def _cross_part(xq, mk, mv, x_heads):
    hd_w = xq.shape[1] // x_heads
    outs = []
    for hd in range(x_heads):
        sl = slice(hd * hd_w, (hd + 1) * hd_w)
        s = _dot_nt(xq[:, sl], mk[:, sl])
        pe = jnp.exp2(s - jnp.max(s, axis=1, keepdims=True))
        o = _dot(pe.astype(BF16), mv[:, sl]) / jnp.sum(pe, axis=1, keepdims=True)
        outs.append(o.astype(BF16))
    return jnp.concatenate(outs, axis=1)


def _route_part(x1, ca, wxo_ref, gffn_ref, wrt_ref, br_ref, carry_ref,
                x2_ref, hf_ref, idx_ref, rank_ref, gate_ref, cnt_ref):
    tm = x1.shape[0]
    n_exp = wrt_ref.shape[1]
    x2 = x1 + _dot(ca, wxo_ref[...])
    x2_ref[...] = x2
    hf = _rms(x2, gffn_ref[...])
    hf_ref[...] = hf
    hf_hi = hf.astype(BF16)
    hf_lo = (hf - hf_hi.astype(F32)).astype(BF16)
    logits = (_dot_nt(wrt_ref[0], hf_hi) + _dot_nt(wrt_ref[0], hf_lo) + _dot_nt(wrt_ref[1], hf_hi)
              + br_ref[...])
    erow = lax.broadcasted_iota(I32, (n_exp, tm), 0).astype(F32)
    picked = jnp.zeros((n_exp, tm), F32)
    vals, onehots, idxs = [], [], []
    for k in range(TOP_K):
        mx = jnp.max(logits, axis=0, keepdims=True)
        idx = jnp.min(jnp.where(logits == mx, erow, float(n_exp)), axis=0, keepdims=True)
        sel = erow == idx
        vals.append(mx)
        onehots.append(sel)
        idxs.append(idx)
        picked = picked + sel.astype(F32)
        logits = jnp.where(sel, -jnp.inf, logits)
    ex = [jnp.exp(v - vals[0]) for v in vals]
    den = ex[0] + ex[1] + ex[2] + ex[3]
    for k in range(TOP_K):
        gate_ref[k:k + 1, :] = ex[k] / den
    r = lax.broadcasted_iota(I32, (tm, tm), 0)
    cidx = lax.broadcasted_iota(I32, (tm, tm), 1)
    upper = jnp.where(r < cidx, 1.0, 0.0).astype(BF16)
    before = _dot(picked.astype(BF16), upper) + carry_ref[...]
    for k in range(TOP_K):
        rank = jnp.sum(jnp.where(onehots[k], before, 0.0), axis=0, keepdims=True)
        idx_ref[k:k + 1, :] = idxs[k].astype(I32)
        rank_ref[k:k + 1, :] = rank.astype(I32)
    carry_ref[...] = carry_ref[...] + jnp.sum(picked, axis=1, keepdims=True)
    cnt_ref[...] = jnp.broadcast_to(carry_ref[...], cnt_ref.shape).astype(I32)


def _post_attn_prompt_kernel(x_ref, fo_ref, mo_ref, ga_ref, gb_ref, mk_ref, mv_ref,
                             wof_ref, wom_ref, wout_ref, gx_ref, wxq_ref, wxo_ref, gffn_ref, wrt_ref, br_ref,
                             x2_ref, hf_ref, idx_ref, rank_ref, gate_ref, cnt_ref, carry_ref,
                             *, x_heads, x_scale):
    @pl.when(pl.program_id(0) == 0)
    def _():
        carry_ref[...] = jnp.zeros_like(carry_ref)

    x1, xq = _mix_part(x_ref[...], fo_ref[...], mo_ref[...], ga_ref[...], gb_ref[...],
                       wof_ref, wom_ref, wout_ref, gx_ref, wxq_ref, x_scale)
    ca = _cross_part(xq, mk_ref[0].astype(BF16), mv_ref[0].astype(BF16), x_heads)
    _route_part(x1, ca, wxo_ref, gffn_ref, wrt_ref, br_ref, carry_ref,
                x2_ref, hf_ref, idx_ref, rank_ref, gate_ref, cnt_ref)


def _route_out(T, D, n_exp, tm):
    tok = lambda n: pl.BlockSpec((tm, n), lambda i: (i, 0))
    col = pl.BlockSpec((TOP_K, tm), lambda i: (0, i))
    sds = jax.ShapeDtypeStruct
    shapes = [sds((T, D), F32), sds((T, D), F32), sds((TOP_K, T), I32), sds((TOP_K, T), I32),
              sds((TOP_K, T), F32), sds((n_exp, LANES), I32)]
    specs = [tok(D), tok(D), col, col, col, pl.BlockSpec((n_exp, LANES), lambda i: (0, 0))]
    return shapes, specs


def _post_attn_prompt(x, fo, mo, ga, gb, mk, mv, w, *, tm, seq):
    T, D = x.shape
    n_exp = w['wrt'].shape[1]
    tiles_per_seq = seq // tm
    tok = lambda n: pl.BlockSpec((tm, n), lambda i: (i, 0))
    mem = pl.BlockSpec((1,) + mk.shape[1:], lambda i: (i // tiles_per_seq, 0, 0))
    weights = [w['wof'], w['wom'], w['wout'], w['gx'], w['wxq'], w['wxo'], w['gffn'], w['wrt'], w['br']]
    shapes, specs = _route_out(T, D, n_exp, tm)
    return pl.pallas_call(
        functools.partial(_post_attn_prompt_kernel, x_heads=w['x_heads'], x_scale=w['x_scale']),
        grid=(T // tm,),
        in_specs=[tok(D), tok(fo.shape[1]), tok(mo.shape[1]), tok(D), tok(D), mem, mem]
        + [_const_spec(a.shape) for a in weights],
        out_specs=specs,
        out_shape=shapes,
        scratch_shapes=[pltpu.VMEM((n_exp, 1), F32)],
        compiler_params=_cparams("arbitrary"),
        name="post_attn_prompt",
    )(x, fo, mo, ga, gb, mk, mv, *weights)


def _mix_sample_kernel(x_ref, fo_ref, mo_ref, ga_ref, gb_ref, wof_ref, wom_ref, wout_ref, gx_ref, wxq_ref,
                       x1_ref, xq_ref, *, x_scale):
    x1, xq = _mix_part(x_ref[...], fo_ref[...], mo_ref[...], ga_ref[...], gb_ref[...],
                       wof_ref, wom_ref, wout_ref, gx_ref, wxq_ref, x_scale)
    x1_ref[...] = x1
    xq_ref[...] = xq


def _cross_sample_kernel(xq_ref, mk_ref, mv_ref, o_ref, *, x_heads):
    o_ref[0] = _cross_part(xq_ref[0], mk_ref[0].astype(BF16), mv_ref[0].astype(BF16), x_heads)


def _route_sample_kernel(x1_ref, ca_ref, wxo_ref, gffn_ref, wrt_ref, br_ref,
                         x2_ref, hf_ref, idx_ref, rank_ref, gate_ref, cnt_ref, carry_ref):
    carry_ref[...] = jnp.zeros_like(carry_ref)
    _route_part(x1_ref[...], ca_ref[...], wxo_ref, gffn_ref, wrt_ref, br_ref, carry_ref,
                x2_ref, hf_ref, idx_ref, rank_ref, gate_ref, cnt_ref)


def _post_attn_sample(x, fo, mo, ga, gb, mk, mv, w, *, t_new):
    T, D = x.shape
    B = T // t_new
    n_exp = w['wrt'].shape[1]
    xw = w['wxq'].shape[1]
    params = pltpu.CompilerParams(vmem_limit_bytes=VMEM_LIMIT_BYTES)
    x1, xq = pl.pallas_call(
        functools.partial(_mix_sample_kernel, x_scale=w['x_scale']),
        out_shape=[jax.ShapeDtypeStruct((T, D), F32), jax.ShapeDtypeStruct((T, xw), BF16)],
        compiler_params=params,
        name="mix_sample",
    )(x, fo, mo, ga, gb, w['wof'], w['wom'], w['wout'], w['gx'], w['wxq'])
    ca = pl.pallas_call(
        functools.partial(_cross_sample_kernel, x_heads=w['x_heads']),
        grid=(B,),
        in_specs=[pl.BlockSpec((1, t_new, xw), lambda b: (b, 0, 0)),
                  pl.BlockSpec((1,) + mk.shape[1:], lambda b: (b, 0, 0)),
                  pl.BlockSpec((1,) + mv.shape[1:], lambda b: (b, 0, 0))],
        out_specs=pl.BlockSpec((1, t_new, xw), lambda b: (b, 0, 0)),
        out_shape=jax.ShapeDtypeStruct((B, t_new, xw), BF16),
        compiler_params=_cparams("parallel"),
        name="cross_sample",
    )(xq.reshape(B, t_new, xw), mk, mv)
    shapes, _ = _route_out(T, D, n_exp, T)
    return pl.pallas_call(
        _route_sample_kernel,
        out_shape=shapes,
        scratch_shapes=[pltpu.VMEM((n_exp, 1), F32)],
        compiler_params=params,
        name="route_sample",
    )(x1, ca.reshape(T, xw), w['wxo'], w['gffn'], w['wrt'], w['br'])


def _mem_kv_kernel(m_ref, g_ref, wk_ref, wv_ref, k_ref, v_ref):
    m = _rms(m_ref[...], g_ref[...]).astype(BF16)
    k_ref[...] = _dot(m, wk_ref[...])
    v_ref[...] = _dot(m, wv_ref[...])


def _mem_kv(mem, w, *, tm):
    T, D = mem.shape
    xw = w['wxk'].shape[1]
    tok = lambda n: pl.BlockSpec((tm, n), lambda i: (i, 0))
    return pl.pallas_call(
        _mem_kv_kernel,
        grid=(T // tm,),
        in_specs=[tok(D), _const_spec(w['gmem'].shape), _const_spec(w['wxk'].shape), _const_spec(w['wxv'].shape)],
        out_specs=[tok(xw), tok(xw)],
        out_shape=[jax.ShapeDtypeStruct((T, xw), F32)] * 2,
        compiler_params=_cparams("parallel"),
        name="mem_kv",
    )(mem, w['gmem'], w['wxk'], w['wxv'])


def _dest_kernel(start_ref, idx_ref, rank_ref, dest_ref):
    idx = idx_ref[...]
    dest = rank_ref[...]
    for e in range(start_ref.shape[0]):
        dest = dest + jnp.where(idx == e, start_ref[e], 0)
    dest_ref[...] = dest


def _dest_rows(pad_start, idx_t, rank_t):
    whole = pl.BlockSpec(idx_t.shape, lambda i, s: (0, 0))
    return pl.pallas_call(
        _dest_kernel,
        grid_spec=pltpu.PrefetchScalarGridSpec(num_scalar_prefetch=1, grid=(1,), in_specs=[whole, whole],
                                               out_specs=whole),
        out_shape=jax.ShapeDtypeStruct(idx_t.shape, I32),
        name="moe_dest",
    )(pad_start, idx_t, rank_t)


def _row_copy(src_ref, src_row, dst_ref, dst_row, sem):
    return pltpu.make_async_copy(src_ref.at[pl.ds(src_row, 1), :], dst_ref.at[pl.ds(dst_row, 1), :], sem)


def _zero_unassigned_rows(cnt_ref, start_ref, nu_ref, xs_ref, zero_ref, sem, bm):
    zero_ref[...] = jnp.zeros_like(zero_ref)
    half = zero_ref.shape[0]
    n_blocks = xs_ref.shape[0] // bm

    def half_block(first_row):
        return pltpu.make_async_copy(zero_ref, xs_ref.at[pl.ds(pl.multiple_of(first_row, half), half)], sem)

    def zero_block(first_row):
        half_block(first_row).start()
        half_block(first_row + half).start()

    def expert_last_block(e, n):
        cnt = cnt_ref[e]
        partial = (cnt & (bm - 1)) != 0

        @pl.when(partial)
        def _():
            zero_block(start_ref[e] + (cnt & -bm))

        return n + partial.astype(I32)

    n_zeroed = lax.fori_loop(0, cnt_ref.shape[0], expert_last_block, 0)

    def tail_block(j, _):
        zero_block(j * bm)
        return 0

    lax.fori_loop(nu_ref[0], n_blocks, tail_block, 0)

    def drain(i, _):
        half_block(0).wait()
        half_block(0).wait()
        return 0

    lax.fori_loop(0, n_zeroed + n_blocks - nu_ref[0], drain, 0)


def _dispatch_kernel(cnt_ref, start_ref, nu_ref, dest_ref, h_ref, xs_ref, zero_ref, sem, zero_sem, *, bm):
    tm = h_ref.shape[0]

    @pl.when(pl.program_id(0) == 0)
    def _():
        _zero_unassigned_rows(cnt_ref, start_ref, nu_ref, xs_ref, zero_ref, zero_sem, bm)

    def issue(t, _):
        for k in range(TOP_K):
            _row_copy(h_ref, t, xs_ref, dest_ref[k, t], sem).start()
        return 0

    lax.fori_loop(0, tm, issue, 0, unroll=ROW_DMA_UNROLL)

    def drain(t, _):
        for k in range(TOP_K):
            _row_copy(h_ref, 0, xs_ref, 0, sem).wait()
        return 0

    lax.fori_loop(0, tm, drain, 0, unroll=2 * ROW_DMA_UNROLL)


def _dispatch(counts, pad_start, n_used, dest_t, h, n_rows, *, tm, bm):
    T, D = h.shape
    return pl.pallas_call(
        functools.partial(_dispatch_kernel, bm=bm),
        grid_spec=pltpu.PrefetchScalarGridSpec(
            num_scalar_prefetch=3, grid=(T // tm,),
            in_specs=[pl.BlockSpec((TOP_K, tm), lambda i, c, s, n: (0, i), memory_space=pltpu.SMEM),
                      pl.BlockSpec((tm, D), lambda i, c, s, n: (i, 0))],
            out_specs=pl.BlockSpec(memory_space=pl.ANY),
            scratch_shapes=[pltpu.VMEM((bm // 2, D), h.dtype), pltpu.SemaphoreType.DMA,
                            pltpu.SemaphoreType.DMA]),
        out_shape=jax.ShapeDtypeStruct((n_rows, D), h.dtype),
        compiler_params=_cparams("arbitrary"),
        name="moe_dispatch",
    )(counts, pad_start, n_used, dest_t, h)


def _expert_kernel(be_ref, nu_ref, x_ref, wgu_ref, bgu_ref, wd_ref, bd_ref, y_ref, *, ff):
    used = pl.program_id(0) < nu_ref[0]

    @pl.when(jnp.logical_not(used))
    def _():
        y_ref[...] = jnp.zeros_like(y_ref)

    @pl.when(used)
    def _():
        gu = _dot(x_ref[...].astype(BF16), wgu_ref[0]) + bgu_ref[0]
        gate = jnp.minimum(gu[:, :ff], SWIGLU_LIMIT)
        up = jnp.clip(gu[:, ff:], -SWIGLU_LIMIT, SWIGLU_LIMIT)
        act = (up + 1.0) * (gate * _sigmoid(gate * SWIGLU_ALPHA))
        y_ref[...] = _dot(act.astype(BF16), wd_ref[0]) + bd_ref[0]


def _expert_ffn(block_e, n_used, xs, w, *, bm):
    D = xs.shape[1]
    ff = w['wd'].shape[1]
    rows = pl.BlockSpec((bm, D), lambda i, be, nu: (i, 0))
    exp = lambda shape: pl.BlockSpec((1,) + shape, lambda i, be, nu: (be[i], 0, 0))
    return pl.pallas_call(
        functools.partial(_expert_kernel, ff=ff),
        grid_spec=pltpu.PrefetchScalarGridSpec(
            num_scalar_prefetch=2, grid=(xs.shape[0] // bm,),
            in_specs=[rows, exp((D, 2 * ff)), exp((1, 2 * ff)), exp((ff, D)), exp((1, D))],
            out_specs=rows),
        out_shape=jax.ShapeDtypeStruct(xs.shape, F32),
        compiler_params=_cparams("arbitrary"),
        name="moe_expert_ffn",
    )(block_e, n_used, xs, w['wgu'], w['bgu'], w['wd'], w['bd'])


def _combine_kernel(dest_ref, dest_next_ref, gate_ref, x2_ref, gfin_ref, yb_ref, y_ref, buf_ref, sem):
    i = pl.program_id(0)
    tm = x2_ref.shape[0]
    slot = i % 2

    def gather(d_ref, s):
        def issue(t, _):
            for k in range(TOP_K):
                _row_copy(yb_ref, d_ref[k, t], buf_ref.at[s, k], t, sem.at[s]).start()
            return 0

        lax.fori_loop(0, tm, issue, 0, unroll=ROW_DMA_UNROLL)

    @pl.when(i == 0)
    def _():
        gather(dest_ref, 0)

    @pl.when(i + 1 < pl.num_programs(0))
    def _():
        gather(dest_next_ref, 1 - slot)

    def drain(t, _):
        for k in range(TOP_K):
            _row_copy(yb_ref, 0, buf_ref.at[slot, k], 0, sem.at[slot]).wait()
        return 0

    lax.fori_loop(0, tm, drain, 0, unroll=2 * ROW_DMA_UNROLL)
    gate = gate_ref[...]
    y = x2_ref[...]
    for k in range(TOP_K):
        y = y + gate[:, k:k + 1] * buf_ref[slot, k]
    y_ref[...] = _rms(y, gfin_ref[...])


def _combine(dest_t, gate_tok, x2, gfin, yb, *, tm):
    T, D = x2.shape
    last = T // tm - 1
    return pl.pallas_call(
        _combine_kernel,
        grid=(T // tm,),
        in_specs=[pl.BlockSpec((TOP_K, tm), lambda i: (0, i), memory_space=pltpu.SMEM),
                  pl.BlockSpec((TOP_K, tm), lambda i: (0, jnp.minimum(i + 1, last)), memory_space=pltpu.SMEM),
                  pl.BlockSpec((tm, TOP_K), lambda i: (i, 0)),
                  pl.BlockSpec((tm, D), lambda i: (i, 0)),
                  _const_spec(gfin.shape),
                  pl.BlockSpec(memory_space=pl.ANY)],
        out_specs=pl.BlockSpec((tm, D), lambda i: (i, 0)),
        out_shape=jax.ShapeDtypeStruct((T, D), F32),
        scratch_shapes=[pltpu.VMEM((2, TOP_K, tm, D), F32), pltpu.SemaphoreType.DMA((2,))],
        compiler_params=_cparams("arbitrary"),
        name="moe_combine",
    )(dest_t, dest_t, gate_tok, x2, gfin, yb)


def _moe_and_final_norm(x2, hf, idx_t, rank_t, gate_t, counts, w, gfin, *, bm, tm):
    T = x2.shape[0]
    n_exp = counts.shape[0]
    n_blocks = -(-(T * TOP_K + n_exp * (bm - 1)) // bm)
    padded = (counts + bm - 1) // bm * bm
    pad_end = jnp.cumsum(padded)
    pad_start = (pad_end - padded).astype(I32)
    n_used = (pad_end[n_exp - 1:] // bm).astype(I32)
    first_row = jnp.arange(n_blocks, dtype=I32) * bm
    block_e = jnp.minimum(jnp.sum(first_row[:, None] >= pad_end[None, :], axis=1), n_exp - 1).astype(I32)
    dest_t = _dest_rows(pad_start, idx_t, rank_t)
    xs = _dispatch(counts.astype(I32), pad_start, n_used, dest_t, hf, n_blocks * bm, tm=tm, bm=bm)
    yb = _expert_ffn(block_e, n_used, xs, w, bm=bm)
    return _combine(dest_t, gate_t.T, x2, gfin, yb, tm=tm)


def _rot_cols(wc):
    half = wc.shape[1] // 2
    return jnp.concatenate([-wc[:, half:], wc[:, :half]], axis=1)


def _prep_layer(p, dims):
    fw, nh, nope, rope, lat, vdim, x_heads, x_hd = dims
    D = p['w_in'].shape[0]
    w_in = p['w_in']
    o = 0
    wfox = w_in[:, o:o + 3 * fw]; o += 3 * fw
    wfl = w_in[:, o:o + nh]; o += nh
    qlora = p['g_q'].shape[0]
    wqc = w_in[:, o:o + qlora]; o += qlora
    wkvc = w_in[:, o:o + lat]; o += lat
    wkr = w_in[:, o:o + rope]; o += rope
    wga = w_in[:, o:o + D]; o += D
    wgb = w_in[:, o:o + D]

    def misc_cols(parts):
        out = jnp.zeros((D, LANES), F32)
        for off, cols in parts:
            out = out.at[:, off:off + cols.shape[1]].set(cols)
        return out

    wma = misc_cols([(MISC_LOGF, wfl), (MISC_CUM, wfl), (MISC_ROPE, wkr)])
    wmb = misc_cols([(MISC_ROPE, _rot_cols(wkr))])
    bf = jnp.zeros((1, LANES), F32).at[0, MISC_LOGF:MISC_LOGF + nh].set(p['b_f']).at[0, MISC_CUM:MISC_CUM + nh].set(p['b_f'])

    wuq = p['w_uq'].reshape(qlora, nh, nope + rope)
    wqa = jnp.zeros((qlora, nh, LANES), F32).at[:, :, :nope + rope].set(wuq)
    rot = jnp.concatenate([-wuq[:, :, nope + rope // 2:], wuq[:, :, nope:nope + rope // 2]], axis=2)
    wqb = jnp.zeros((qlora, nh, LANES), F32).at[:, :, nope:nope + rope].set(rot)
    wk = jnp.zeros((lat + LANES, nh, LANES), F32).at[:lat, :, :nope].set(p['w_uk'])
    place = jnp.zeros((LANES, nh, LANES), F32)
    j = jnp.arange(rope)
    place = place.at[MISC_ROPE + j, :, nope + j].set(1.0)
    wk = wk.at[lat:, :, :].set(place)
    wukp = jnp.zeros((nh, LANES, lat), F32).at[:, :nope, :].set(jnp.transpose(p['w_uk'], (1, 2, 0)))

    b = lambda a: a.astype(BF16)
    row = lambda a: a.reshape(1, -1).astype(F32)
    n_exp = p['w_router'].shape[1]
    wrt = p['w_router'].T.astype(F32)
    wrt_hi = b(wrt)
    wrt = jnp.stack([wrt_hi, b(wrt - wrt_hi.astype(F32))])
    return {
        'fox_width': fw, 'n_heads': nh, 'fox_scale': float(fw // nh) ** -0.5 * LOG2E,
        'x_heads': x_heads, 'x_scale': float(x_hd) ** -0.5 * LOG2E,
        'gmix': row(p['g_mix']), 'wfox': b(wfox), 'wma': b(wma), 'wmb': b(wmb), 'wqc': b(wqc), 'wkvc': b(wkvc),
        'wga': b(wga), 'wgb': b(wgb), 'bf': bf, 'gq': row(p['g_q']),
        'wqa': b(wqa.reshape(qlora, nh * LANES)), 'wqb': b(wqb.reshape(qlora, nh * LANES)),
        'gkv': row(p['g_kv']), 'wk': b(wk.reshape(lat + LANES, nh * LANES)),
        'wuv': b(p['w_uv'].reshape(lat, nh * vdim)), 'wukp': b(wukp),
        'wof': b(p['w_o_fox']), 'wom': b(p['w_o_mla']), 'wout': b(p['w_out']), 'gx': row(p['g_x']),
        'wxq': b(p['w_xq']), 'wxo': b(p['w_xo']), 'gffn': row(p['g_ffn']),
        'wrt': wrt, 'br': p['b_router'].reshape(n_exp, 1).astype(F32),
        'gmem': row(p['g_mem']), 'wxk': b(p['w_xk']), 'wxv': b(p['w_xv']),
        'wgu': b(p['w_gu']), 'bgu': p['b_gu'].reshape(n_exp, 1, -1).astype(F32),
        'wd': b(p['w_down']), 'bd': p['b_down'].reshape(n_exp, 1, -1).astype(F32),
    }


def _rope_tables(pos, nope, rope, q_scale):
    half = rope // 2
    inv_freq = ROPE_BASE ** (-jnp.arange(half, dtype=F32) / half)
    ang = pos.astype(F32)[:, None] * inv_freq[None, :]
    cos = jnp.concatenate([jnp.cos(ang)] * 2, axis=1)
    sin = jnp.concatenate([jnp.sin(ang)] * 2, axis=1)
    n = pos.shape[0]
    ck = jnp.zeros((n, LANES), F32).at[:, :MISC_ROPE].set(1.0).at[:, MISC_ROPE:MISC_ROPE + rope].set(cos)
    sk = jnp.zeros((n, LANES), F32).at[:, MISC_ROPE:MISC_ROPE + rope].set(sin)
    cq = jnp.zeros((n, LANES), F32).at[:, :nope].set(1.0).at[:, nope:nope + rope].set(cos) * q_scale
    sq = jnp.zeros((n, LANES), F32).at[:, nope:nope + rope].set(sin) * q_scale
    return ck, sk, cq, sq


def _pick_tile(n, target):
    t = min(n, target)
    while n % t:
        t //= 2
    return t


def _prompt_layer(x, mem, w, dims):
    fw, nh, nope, rope, lat, vdim, x_heads, x_hd = dims
    B, S, D = x.shape
    T = B * S
    tm = _pick_tile(S, PROJ_TILE)
    tables = _rope_tables(jnp.arange(S, dtype=I32), nope, rope, float(nope + rope) ** -0.5 * LOG2E)
    fq, fk, fv, fkb, fvb, misc, ckv, qm, ga, gb, km, vm = _in_proj(
        x.reshape(T, D), w, tables, tm=tm, tiles_per_seq=S // tm, prompt=True)

    blk = _pick_tile(S, ATTN_BLOCK)
    cum = misc[:, MISC_CUM:MISC_CUM + nh].reshape(B, S // blk, blk, nh)
    cum_row = jnp.swapaxes(cum, 2, 3)
    r3 = lambda a: a.reshape(B, S, a.shape[1])
    fox_o = _prompt_attn(r3(fq), r3(fkb), r3(fvb), r3(misc), cum_row, blk=blk, n_heads=nh, fox=True)
    mla_o = _prompt_attn(r3(qm), r3(km), r3(vm), None, None, blk=blk, n_heads=nh, fox=False)

    n_mem = mem.shape[1]
    mk, mv = _mem_kv(mem.reshape(B * n_mem, D), w, tm=_pick_tile(B * n_mem, 512))
    xw = mk.shape[1]
    routed = _post_attn_prompt(
        x.reshape(T, D), fox_o.reshape(T, fw), mla_o.reshape(T, nh * vdim), ga, gb,
        mk.reshape(B, n_mem, xw), mv.reshape(B, n_mem, xw), w, tm=tm, seq=S)
    caches = (fk.reshape(B, S, nh, fw // nh), fv.reshape(B, S, nh, fw // nh),
              misc[:, MISC_LOGF:MISC_LOGF + nh].reshape(B, S, nh), ckv.reshape(B, S, lat),
              misc[:, MISC_ROPE:MISC_ROPE + rope].reshape(B, S, rope),
              mk.reshape(B, n_mem, x_heads, x_hd), mv.reshape(B, n_mem, x_heads, x_hd))
    return routed, caches


def _sample_layer(x, c_fk, c_fv, c_logf, c_ckv, c_kr, c_mk, c_mv, w, dims):
    fw, nh, nope, rope, lat, vdim, x_heads, x_hd = dims
    B, t_new, D = x.shape
    P = c_fk.shape[1]
    T = B * t_new
    pos = P + jnp.tile(jnp.arange(t_new, dtype=I32), B)
    tables = _rope_tables(pos, nope, rope, float(nope + rope) ** -0.5 * LOG2E)
    fq, fk, fv, fkb, fvb, misc, ckv, qm, ga, gb = _in_proj(
        x.reshape(T, D), w, tables, tm=T, tiles_per_seq=1, prompt=False)

    logf_new = misc[:, MISC_LOGF:MISC_LOGF + nh].reshape(B, t_new, nh)
    lf = jnp.concatenate([c_logf.astype(F32), logf_new, jnp.zeros((B, LANES - t_new, nh), F32)], axis=1)
    cum_all = _lane_cumsum(jnp.swapaxes(lf, 1, 2).reshape(B * nh, P + LANES)).reshape(B, nh, P + LANES)
    cum_q = cum_all[:, :, P:P + t_new].reshape(B, nh * t_new, 1)

    chunk = _pick_tile(P, CACHE_CHUNK)
    r3 = lambda a: a.reshape(B, t_new, a.shape[1])
    fox_o = _fox_sample_attn(r3(fq), c_fk.reshape(B, P, fw).astype(BF16), c_fv.reshape(B, P, fw).astype(BF16),
                             r3(fkb), r3(fvb), cum_q, cum_all, n_heads=nh, chunk=chunk)
    mla_o = _mla_sample_attn(r3(qm), w['wukp'], c_ckv, c_kr, r3(ckv), r3(misc), w['wuv'],
                             n_heads=nh, chunk=chunk, nope=nope, rope=rope)
    n_mem = c_mk.shape[1]
    routed = _post_attn_sample(
        x.reshape(T, D), fox_o.reshape(T, fw), mla_o.reshape(T, nh * vdim), ga, gb,
        c_mk.reshape(B, n_mem, x_heads * x_hd), c_mv.reshape(B, n_mem, x_heads * x_hd), w, t_new=t_new)
    caches = (fk.reshape(B, t_new, nh, fw // nh), fv.reshape(B, t_new, nh, fw // nh), logf_new,
              ckv.reshape(B, t_new, lat), misc[:, MISC_ROPE:MISC_ROPE + rope].reshape(B, t_new, rope))
    return routed, caches


def kernel(x_prompt, x_sample, mem_prompt, cache_fox_k, cache_fox_v, cache_fox_logf, cache_mla_ckv,
           cache_mla_krope, cache_mem_k, cache_mem_v, g_mix, w_in, b_f, g_q, w_uq, g_kv, w_uk, w_uv,
           w_o_fox, w_o_mla, w_out, g_x, g_mem, w_xq, w_xk, w_xv, w_xo, g_ffn, w_router, b_router,
           w_gu, b_gu, w_down, b_down, g_final):
    depth = w_in.shape[0]
    assert depth == 1, "the fused final norm assumes a single layer"
    nh, fhd = cache_fox_k.shape[3:]
    lat = cache_mla_ckv.shape[3]
    rope = cache_mla_krope.shape[3]
    nope, vdim = w_uk.shape[3], w_uv.shape[3]
    x_heads, x_hd = cache_mem_k.shape[3:]
    dims = (nh * fhd, nh, nope, rope, lat, vdim, x_heads, x_hd)
    B, S, D = x_prompt.shape
    Bs, t_new, _ = x_sample.shape
    gfin = g_final.reshape(1, D).astype(F32)

    l = 0
    p = {'g_mix': g_mix[l], 'w_in': w_in[l], 'b_f': b_f[l], 'g_q': g_q[l], 'w_uq': w_uq[l],
         'g_kv': g_kv[l], 'w_uk': w_uk[l], 'w_uv': w_uv[l], 'w_o_fox': w_o_fox[l],
         'w_o_mla': w_o_mla[l], 'w_out': w_out[l], 'g_x': g_x[l], 'g_mem': g_mem[l],
         'w_xq': w_xq[l], 'w_xk': w_xk[l], 'w_xv': w_xv[l], 'w_xo': w_xo[l], 'g_ffn': g_ffn[l],
         'w_router': w_router[l], 'b_router': b_router[l], 'w_gu': w_gu[l], 'b_gu': b_gu[l],
         'w_down': w_down[l], 'b_down': b_down[l]}
    w = _prep_layer(p, dims)

    (*routed_p, cnt_p), pc = _prompt_layer(x_prompt, mem_prompt, w, dims)
    (*routed_s, cnt_s), sc = _sample_layer(
        x_sample, cache_fox_k[l], cache_fox_v[l], cache_fox_logf[l], cache_mla_ckv[l],
        cache_mla_krope[l], cache_mem_k[l], cache_mem_v[l], w, dims)
    y_prompt = _moe_and_final_norm(*routed_p, cnt_p[:, 0], w, gfin,
                                   bm=EXPERT_ROWS_PROMPT, tm=TOKEN_TILE).reshape(B, S, D)
    y_sample = _moe_and_final_norm(*routed_s, cnt_s[:, 0], w, gfin, bm=EXPERT_ROWS_SAMPLE,
                                   tm=_pick_tile(Bs * t_new, TOKEN_TILE)).reshape(Bs, t_new, D)
    return (y_prompt, y_sample) + tuple(a[None] for a in pc) + tuple(a[None] for a in sc)
```

```python
import functools
import math

import jax
import jax.numpy as jnp
from jax import lax
from jax.experimental import pallas as pl
from jax.experimental.pallas import tpu as pltpu

F32 = jnp.float32
BF16 = jnp.bfloat16
I32 = jnp.int32

CHUNK = 64
EPS = 1e-6
NEG_INF = -1e30
ROPE_BASE = 10000.0
TOP_K = 4
SWIGLU_LIMIT = 7.0
SWIGLU_ALPHA = 1.702
LOG2E = math.log2(math.e)

LANES = 128
SUBLANES = 8
VMEM_LIMIT_BYTES = 56 * 1024 * 1024

MISC_LOGF = 0
MISC_CUM = 8
MISC_ROPE = 16

PROJ_TILE = 512
TOKEN_TILE = 256
ATTN_BLOCK = 256
CACHE_CHUNK = 1024
EXPERT_ROWS_PROMPT = 512
EXPERT_ROWS_SAMPLE = 128
ROW_DMA_UNROLL = 4


def _cparams(*sem):
    return pltpu.CompilerParams(dimension_semantics=sem, vmem_limit_bytes=VMEM_LIMIT_BYTES)


def _const_spec(shape):
    nd = len(shape)
    return pl.BlockSpec(shape, lambda *_: (0,) * nd, pipeline_mode=pl.Buffered(1))


def _rms(x, g):
    return x * lax.rsqrt(jnp.mean(x * x, axis=-1, keepdims=True) + EPS) * g


def _sigmoid(x):
    return 1.0 / (1.0 + jnp.exp(-x))


def _log_sigmoid(x):
    return jnp.minimum(x, 0.0) - jnp.log1p(jnp.exp(-jnp.abs(x)))


def _prefix_sum(c, axis):
    n = c.shape[axis]
    pos = lax.broadcasted_iota(I32, c.shape, axis)
    s = 1
    while s < n:
        c = c + jnp.where(pos >= s, pltpu.roll(c, s, axis), 0.0)
        s *= 2
    return c


def _dot(a, b):
    return jnp.dot(a, b, preferred_element_type=F32)


def _dot_nt(a, b):
    return lax.dot_general(a, b, (((1,), (1,)), ((), ())), preferred_element_type=F32)


def _tile_lanes(t, n):
    return jnp.concatenate([t] * n, axis=1)


def _in_proj_kernel(x_ref, gmix_ref, wfox_ref, wma_ref, wmb_ref, wqc_ref, wkvc_ref, wga_ref, wgb_ref,
                    bf_ref, gq_ref, wqa_ref, wqb_ref, gkv_ref, wk_ref, wuv_ref,
                    ck_ref, sk_ref, cq_ref, sq_ref,
                    fq_ref, fk_ref, fv_ref, fkb_ref, fvb_ref, misc_ref, ckv_ref, qm_ref, ga_ref, gb_ref, *rest,
                    fox_width, n_heads, fox_scale, tiles_per_seq, prompt):
    if prompt:
        km_ref, vm_ref, carry_ref = rest
    tm = x_ref.shape[0]
    h = _rms(x_ref[...], gmix_ref[...]).astype(BF16)

    fq_ref[...] = (_dot(h, wfox_ref[:, 0:fox_width]) * fox_scale).astype(BF16)
    fk = _dot(h, wfox_ref[:, fox_width:2 * fox_width])
    fv = _dot(h, wfox_ref[:, 2 * fox_width:3 * fox_width])
    fk_ref[...] = fk
    fv_ref[...] = fv
    fkb_ref[...] = fk.astype(BF16)
    fvb_ref[...] = fv.astype(BF16)

    lane = lax.broadcasted_iota(I32, (tm, LANES), 1)
    pre = _dot(h, wma_ref[...]) * ck_ref[...] + _dot(h, wmb_ref[...]) * sk_ref[...]
    misc = jnp.where(lane < MISC_ROPE, _log_sigmoid(pre + bf_ref[...]), pre)
    if prompt:
        i = pl.program_id(0)

        @pl.when(i % tiles_per_seq == 0)
        def _():
            carry_ref[...] = jnp.zeros_like(carry_ref)

        in_cum = (lane >= MISC_CUM) & (lane < MISC_ROPE)
        cum = _prefix_sum(jnp.where(in_cum, misc * LOG2E, 0.0), 0) + carry_ref[...]
        carry_ref[...] = cum[tm - 1:tm, :]
        misc = jnp.where(in_cum, cum, misc)
    misc_ref[...] = misc

    ckv = _rms(_dot(h, wkvc_ref[...]), gkv_ref[...])
    ckv_ref[...] = ckv

    qn = _rms(_dot(h, wqc_ref[...]), gq_ref[...]).astype(BF16)
    cq = _tile_lanes(cq_ref[...], n_heads)
    sq = _tile_lanes(sq_ref[...], n_heads)
    qm_ref[...] = (_dot(qn, wqa_ref[...]) * cq + _dot(qn, wqb_ref[...]) * sq).astype(BF16)

    if prompt:
        ckv_b = ckv.astype(BF16)
        km_ref[...] = _dot(jnp.concatenate([ckv_b, misc.astype(BF16)], axis=1), wk_ref[...]).astype(BF16)
        vm_ref[...] = _dot(ckv_b, wuv_ref[...]).astype(BF16)

    ga_ref[...] = _sigmoid(_dot(h, wga_ref[...])).astype(BF16)
    gb_ref[...] = _sigmoid(_dot(h, wgb_ref[...])).astype(BF16)


def _in_proj(x, w, tables, *, tm, tiles_per_seq, prompt):
    T, D = x.shape
    fw = w['fox_width']
    nh = w['n_heads']
    lat = w['wkvc'].shape[1]
    hw = nh * LANES
    ck, sk, cq, sq = tables
    tok = lambda n: pl.BlockSpec((tm, n), lambda i: (i, 0))
    if prompt:
        tab = pl.BlockSpec((tm, LANES), lambda i: (i % tiles_per_seq, 0))
    else:
        tab = tok(LANES)
    weights = [w['gmix'], w['wfox'], w['wma'], w['wmb'], w['wqc'], w['wkvc'], w['wga'], w['wgb'],
               w['bf'], w['gq'], w['wqa'], w['wqb'], w['gkv'], w['wk'], w['wuv']]
    sds = jax.ShapeDtypeStruct
    out_shape = [sds((T, fw), BF16), sds((T, fw), F32), sds((T, fw), F32),
                 sds((T, fw), BF16), sds((T, fw), BF16), sds((T, LANES), F32),
                 sds((T, lat), F32), sds((T, hw), BF16), sds((T, D), BF16), sds((T, D), BF16)]
    out_specs = [tok(fw), tok(fw), tok(fw), tok(fw), tok(fw), tok(LANES), tok(lat), tok(hw), tok(D), tok(D)]
    scratch = []
    if prompt:
        out_shape += [sds((T, hw), BF16), sds((T, w['wuv'].shape[1]), BF16)]
        out_specs += [tok(hw), tok(w['wuv'].shape[1])]
        scratch = [pltpu.VMEM((1, LANES), F32)]
    kern = functools.partial(_in_proj_kernel, fox_width=fw, n_heads=nh, fox_scale=w['fox_scale'],
                             tiles_per_seq=tiles_per_seq, prompt=prompt)
    return pl.pallas_call(
        kern,
        grid=(T // tm,),
        in_specs=[tok(D)] + [_const_spec(a.shape) for a in weights] + [tab] * 4,
        out_specs=out_specs,
        out_shape=out_shape,
        scratch_shapes=scratch,
        compiler_params=_cparams("arbitrary"),
        name="in_proj_prompt" if prompt else "in_proj_sample",
    )(x, *weights, ck, sk, cq, sq)


def _fold_lanes(x, op):
    out = x[:, 0:LANES]
    for j in range(1, x.shape[1] // LANES):
        out = op(out, x[:, j * LANES:(j + 1) * LANES])
    return out


def _prompt_attn_kernel(*refs, blk, n_heads, fox):
    if fox:
        q_ref, k_ref, v_ref, ccol_ref, crow_ref, o_ref, qs_ref, cq_ref, s_ref, m_ref, l_ref, acc_ref = refs
    else:
        q_ref, k_ref, v_ref, o_ref, s_ref, m_ref, l_ref, acc_ref = refs
    qi = pl.program_id(1)
    n_pairs = n_heads // 2
    half = LANES // 2
    lane = lax.broadcasted_iota(I32, (blk, LANES), 1)

    m_ref[...] = jnp.full_like(m_ref, NEG_INF)
    if fox:
        zero = jnp.zeros((), BF16)
        for p in range(n_pairs):
            q2 = q_ref[0, :, p * LANES:(p + 1) * LANES]
            qs_ref[p, 0:blk, :] = jnp.where(lane < half, q2, zero)
            qs_ref[p, blk:2 * blk, :] = jnp.where(lane >= half, q2, zero)
            for hh in range(2):
                col = MISC_CUM + 2 * p + hh
                cq_ref[p, hh * blk:(hh + 1) * blk, :] = jnp.broadcast_to(ccol_ref[0, :, col:col + 1], (blk, LANES))

    def scores(kb, diagonal):
        ks = pl.multiple_of(kb * blk, blk)
        for p in range(n_pairs):
            if fox:
                s = _dot_nt(qs_ref[p], k_ref[0, pl.ds(ks, blk), p * LANES:(p + 1) * LANES])
                ck = jnp.concatenate([jnp.broadcast_to(crow_ref[0, kb, 2 * p + hh:2 * p + hh + 1, :], (blk, blk))
                                      for hh in range(2)], axis=0)
                s = s + (_tile_lanes(cq_ref[p], blk // LANES) - ck)
            else:
                s = jnp.concatenate(
                    [_dot_nt(q_ref[0, :, hd * LANES:(hd + 1) * LANES], k_ref[0, pl.ds(ks, blk), hd * LANES:(hd + 1) * LANES])
                     for hd in (2 * p, 2 * p + 1)], axis=0)
            if diagonal:
                r = lax.broadcasted_iota(I32, (2 * blk, blk), 0)
                r = jnp.where(r >= blk, r - blk, r)
                c = lax.broadcasted_iota(I32, (2 * blk, blk), 1)
                mask = (c <= r) if fox else ((c // CHUNK) <= (r // CHUNK))
                s = jnp.where(mask, s, NEG_INF)
            s_ref[p, kb] = s
            m_ref[p] = jnp.maximum(m_ref[p], _fold_lanes(s, jnp.maximum))

    def score_body(kb, carry):
        scores(kb, False)
        return carry

    lax.fori_loop(0, qi, score_body, 0)
    scores(qi, True)

    for p in range(n_pairs):
        m_ref[p] = jnp.broadcast_to(jnp.max(m_ref[p], axis=1, keepdims=True), (2 * blk, LANES))
    l_ref[...] = jnp.zeros_like(l_ref)
    acc_ref[...] = jnp.zeros_like(acc_ref)

    def weigh_body(kb, carry):
        ks = pl.multiple_of(kb * blk, blk)
        for p in range(n_pairs):
            pe = jnp.exp2(s_ref[p, kb] - _tile_lanes(m_ref[p], blk // LANES))
            l_ref[p] = l_ref[p] + _fold_lanes(pe, jnp.add)
            acc_ref[p] = acc_ref[p] + _dot(pe.astype(BF16), v_ref[0, pl.ds(ks, blk), p * LANES:(p + 1) * LANES])
        return carry

    lax.fori_loop(0, qi + 1, weigh_body, 0)
    for p in range(n_pairs):
        o = acc_ref[p] / jnp.sum(l_ref[p], axis=1, keepdims=True)
        o_ref[0, :, p * LANES:(p + 1) * LANES] = jnp.where(lane < half, o[0:blk], o[blk:2 * blk]).astype(o_ref.dtype)


def _prompt_attn(q, k, v, cum_col, cum_row, *, blk, n_heads, fox):
    B, S, qw = q.shape
    vw = v.shape[2]
    n_pairs = n_heads // 2
    in_specs = [pl.BlockSpec((1, blk, qw), lambda b, i: (b, i, 0)),
                pl.BlockSpec((1, S, k.shape[2]), lambda b, i: (b, 0, 0)),
                pl.BlockSpec((1, S, vw), lambda b, i: (b, 0, 0))]
    args = [q, k, v]
    stat = pltpu.VMEM((n_pairs, 2 * blk, LANES), F32)
    scratch = []
    if fox:
        in_specs += [pl.BlockSpec((1, blk, LANES), lambda b, i: (b, i, 0)),
                     pl.BlockSpec((1,) + cum_row.shape[1:], lambda b, i: (b, 0, 0, 0))]
        args += [cum_col, cum_row]
        scratch = [pltpu.VMEM((n_pairs, 2 * blk, LANES), BF16), stat]
    scratch += [pltpu.VMEM((n_pairs, S // blk, 2 * blk, blk), F32), stat, stat, stat]
    return pl.pallas_call(
        functools.partial(_prompt_attn_kernel, blk=blk, n_heads=n_heads, fox=fox),
        grid=(B, S // blk),
        in_specs=in_specs,
        out_specs=pl.BlockSpec((1, blk, vw), lambda b, i: (b, i, 0)),
        out_shape=jax.ShapeDtypeStruct((B, S, vw), BF16),
        scratch_shapes=scratch,
        compiler_params=_cparams("parallel", "parallel"),
        name="fox_attn_prompt" if fox else "mla_attn_prompt",
    )(*args)


def _lane_cumsum_kernel(x_ref, o_ref):
    o_ref[...] = _prefix_sum(x_ref[...], 1) * LOG2E


def _lane_cumsum(x):
    return pl.pallas_call(
        _lane_cumsum_kernel,
        out_shape=jax.ShapeDtypeStruct(x.shape, F32),
        compiler_params=pltpu.CompilerParams(vmem_limit_bytes=VMEM_LIMIT_BYTES),
        name="logf_cumsum_sample",
    )(x)


def _head_rows(x, n_heads, head_lanes):
    t, w = x.shape
    xt = jnp.concatenate([x] * n_heads, axis=0)
    row = lax.broadcasted_iota(I32, xt.shape, 0)
    lane = lax.broadcasted_iota(I32, xt.shape, 1)
    return jnp.where(lane // head_lanes == row // t, xt, jnp.zeros((), x.dtype))


def _fold_head_rows(o, n_heads, t, head_lanes):
    row = lax.broadcasted_iota(I32, o.shape, 0)
    lane = lax.broadcasted_iota(I32, o.shape, 1)
    o = jnp.where(lane // head_lanes == row // t, o, 0.0)
    out = o[0:t]
    for hd in range(1, n_heads):
        out = out + o[hd * t:(hd + 1) * t]
    return out


def _repeat_rows(x, t):
    return jnp.concatenate([jnp.broadcast_to(x[r:r + 1], (t, x.shape[1])) for r in range(x.shape[0])], axis=0)


def _softmax_step(s, v, m_ref, l_ref, acc_ref):
    m = m_ref[...]
    m_new = jnp.maximum(m, jnp.max(s, axis=1, keepdims=True))
    alpha = jnp.exp2(m - m_new)
    pe = jnp.exp2(s - m_new)
    l_ref[...] = alpha * l_ref[...] + jnp.sum(pe, axis=1, keepdims=True)
    acc_ref[...] = alpha * acc_ref[...] + _dot(pe.astype(BF16), v)
    m_ref[...] = m_new


def _fox_sample_kernel(q_ref, k_ref, v_ref, kn_ref, vn_ref, cq_ref, ck_ref, ckn_ref, o_ref,
                       qall_ref, qbd_ref, m_ref, l_ref, acc_ref, *, n_heads, t_new):
    c = pl.program_id(1)
    hd_w = q_ref.shape[2] // n_heads
    heads = [slice(hd * hd_w, (hd + 1) * hd_w) for hd in range(n_heads)]
    groups = [slice(hd * t_new, (hd + 1) * t_new) for hd in range(n_heads)]

    @pl.when(c == 0)
    def _():
        q = q_ref[0]
        for hd in range(n_heads):
            qall_ref[groups[hd], :] = q[:, heads[hd]]
        qbd_ref[...] = _head_rows(q, n_heads, hd_w)
        m_ref[...] = jnp.full_like(m_ref, NEG_INF)
        l_ref[...] = jnp.zeros_like(l_ref)
        acc_ref[...] = jnp.zeros_like(acc_ref)

    def step(s, pv_fn):
        m = m_ref[...]
        m_new = jnp.maximum(m, jnp.max(s, axis=1, keepdims=True))
        alpha = jnp.exp2(m - m_new)
        pe = jnp.exp2(s - m_new)
        l_ref[...] = alpha * l_ref[...] + jnp.sum(pe, axis=1, keepdims=True)
        acc_ref[...] = alpha * acc_ref[...] + pv_fn(pe.astype(BF16))
        m_ref[...] = m_new

    cq = cq_ref[0]
    s = _dot_nt(qall_ref[...], k_ref[0].astype(BF16)) + cq - ck_ref[0]
    row = lax.broadcasted_iota(I32, s.shape, 0)
    col = lax.broadcasted_iota(I32, s.shape, 1)
    s = jnp.where(col % n_heads == row // t_new, s, NEG_INF)
    step(s, lambda pe: _dot(pe, v_ref[0].astype(BF16)))

    @pl.when(c == pl.num_programs(1) - 1)
    def _():
        sn = _dot_nt(qbd_ref[...], kn_ref[0]) + cq - _repeat_rows(ckn_ref[0][:, 0:t_new], t_new)
        rown = lax.broadcasted_iota(I32, sn.shape, 0)
        coln = lax.broadcasted_iota(I32, sn.shape, 1)
        sn = jnp.where(coln <= rown % t_new, sn, NEG_INF)

        def new_values(pe):
            full = _dot(pe, vn_ref[0])
            return jnp.concatenate([full[groups[hd], heads[hd]] for hd in range(n_heads)], axis=0)

        step(sn, new_values)
        o = acc_ref[...] / l_ref[...]
        o_ref[0] = jnp.concatenate([o[groups[hd]] for hd in range(n_heads)], axis=1).astype(o_ref.dtype)


def _fox_sample_attn(q, k_flat, v_flat, k_new, v_new, cum_q, cum_flat, cum_all, *, n_heads, chunk):
    B, t_new, w = q.shape
    hd_w = w // n_heads
    P = k_flat.shape[1] // n_heads
    rows = n_heads * t_new
    cache = pl.BlockSpec((1, chunk * n_heads, hd_w), lambda b, c: (b, c, 0))
    new = pl.BlockSpec((1, t_new, w), lambda b, c: (b, 0, 0))
    return pl.pallas_call(
        functools.partial(_fox_sample_kernel, n_heads=n_heads, t_new=t_new),
        grid=(B, P // chunk),
        in_specs=[new, cache, cache, new, new,
                  pl.BlockSpec((1, rows, 1), lambda b, c: (b, 0, 0)),
                  pl.BlockSpec((1, 1, chunk * n_heads), lambda b, c: (b, 0, c)),
                  pl.BlockSpec((1, n_heads, LANES), lambda b, c: (b, 0, P // LANES))],
        out_specs=new,
        out_shape=jax.ShapeDtypeStruct((B, t_new, w), BF16),
        scratch_shapes=[pltpu.VMEM((rows, hd_w), BF16), pltpu.VMEM((rows, w), BF16), pltpu.VMEM((rows, 1), F32),
                        pltpu.VMEM((rows, 1), F32), pltpu.VMEM((rows, hd_w), F32)],
        compiler_params=_cparams("parallel", "arbitrary"),
        name="fox_attn_sample",
    )(q, k_flat, v_flat, k_new, v_new, cum_q, cum_flat, cum_all)


def _mla_sample_kernel(q_ref, wuk_ref, ckv_ref, kr_ref, ckvn_ref, miscn_ref, wuv_ref, o_ref,
                       ql_ref, qr_ref, m_ref, l_ref, acc_ref, *, n_heads, t_new, nope, rope, past):
    c = pl.program_id(1)

    @pl.when(c == 0)
    def _():
        for hd in range(n_heads):
            qh = q_ref[0, :, hd * LANES:(hd + 1) * LANES]
            rows = slice(hd * t_new, (hd + 1) * t_new)
            ql_ref[rows, :] = _dot(qh, wuk_ref[hd]).astype(BF16)
            qr_ref[rows, :] = qh[:, nope:nope + rope]
        m_ref[...] = jnp.full_like(m_ref, NEG_INF)
        l_ref[...] = jnp.zeros_like(l_ref)
        acc_ref[...] = jnp.zeros_like(acc_ref)

    ql = ql_ref[...]
    qr = qr_ref[...]
    ckv = ckv_ref[0].astype(BF16)
    s = _dot_nt(ql, ckv) + _dot_nt(qr, kr_ref[0].astype(BF16))
    _softmax_step(s, ckv, m_ref, l_ref, acc_ref)

    @pl.when(c == pl.num_programs(1) - 1)
    def _():
        ckvn = ckvn_ref[0].astype(BF16)
        krn = miscn_ref[0][:, MISC_ROPE:MISC_ROPE + rope].astype(BF16)
        sn = _dot_nt(ql, ckvn) + _dot_nt(qr, krn)
        row = lax.broadcasted_iota(I32, sn.shape, 0)
        col = lax.broadcasted_iota(I32, sn.shape, 1)
        sn = jnp.where((past + col) // CHUNK <= (past + row % t_new) // CHUNK, sn, NEG_INF)
        _softmax_step(sn, ckvn, m_ref, l_ref, acc_ref)
        lat = (acc_ref[...] / l_ref[...]).astype(BF16)
        o = _dot(lat, wuv_ref[...])
        o_ref[0] = _fold_head_rows(o, n_heads, t_new, o.shape[1] // n_heads).astype(o_ref.dtype)


def _mla_sample_attn(qm, wukp, ckv_cache, kr_cache, ckv_new, misc_new, wuv, *, n_heads, chunk, nope, rope):
    B, t_new, qw = qm.shape
    P, lat = ckv_cache.shape[1:]
    rows = n_heads * t_new
    vw = wuv.shape[1]
    return pl.pallas_call(
        functools.partial(_mla_sample_kernel, n_heads=n_heads, t_new=t_new, nope=nope, rope=rope, past=P),
        grid=(B, P // chunk),
        in_specs=[pl.BlockSpec((1, t_new, qw), lambda b, c: (b, 0, 0)),
                  _const_spec(wukp.shape),
                  pl.BlockSpec((1, chunk, lat), lambda b, c: (b, c, 0)),
                  pl.BlockSpec((1, chunk, rope), lambda b, c: (b, c, 0)),
                  pl.BlockSpec((1, t_new, lat), lambda b, c: (b, 0, 0)),
                  pl.BlockSpec((1, t_new, LANES), lambda b, c: (b, 0, 0)),
                  _const_spec(wuv.shape)],
        out_specs=pl.BlockSpec((1, t_new, vw), lambda b, c: (b, 0, 0)),
        out_shape=jax.ShapeDtypeStruct((B, t_new, vw), BF16),
        scratch_shapes=[pltpu.VMEM((rows, lat), BF16), pltpu.VMEM((rows, rope), BF16),
                        pltpu.VMEM((rows, 1), F32), pltpu.VMEM((rows, 1), F32), pltpu.VMEM((rows, lat), F32)],
        compiler_params=_cparams("parallel", "arbitrary"),
        name="mla_attn_sample",
    )(qm, wukp, ckv_cache, kr_cache, ckv_new, misc_new, wuv)


def _mix_part(x, fo, mo, ga, gb, wof_ref, wom_ref, wout_ref, gx_ref, wxq_ref, x_scale):
    a = _dot(fo, wof_ref[...])
    b = _dot(mo, wom_ref[...])
    merged = (ga.astype(F32) * a + gb.astype(F32) * b).astype(BF16)
    x1 = x + _dot(merged, wout_ref[...])
    xq = (_dot(_rms(x1, gx_ref[...]).astype(BF16), wxq_ref[...]) * x_scale).astype(BF16)
    return x1, xq


def _cross_part(xq, mk, mv, x_heads):
    hd_w = xq.shape[1] // x_heads
    outs = []
    for hd in range(x_heads):
        sl = slice(hd * hd_w, (hd + 1) * hd_w)
        s = _dot_nt(xq[:, sl], mk[:, sl])
        pe = jnp.exp2(s - jnp.max(s, axis=1, keepdims=True))
        o = _dot(pe.astype(BF16), mv[:, sl]) / jnp.sum(pe, axis=1, keepdims=True)
        outs.append(o.astype(BF16))
    return jnp.concatenate(outs, axis=1)


def _route_part(x1, ca, wxo_ref, gffn_ref, wrt_ref, br_ref, carry_ref,
                x2_ref, hf_ref, idx_ref, rank_ref, gate_ref, cnt_ref):
    tm = x1.shape[0]
    n_exp = wrt_ref.shape[1]
    x2 = x1 + _dot(ca, wxo_ref[...])
    x2_ref[...] = x2
    hf = _rms(x2, gffn_ref[...])
    hf_ref[...] = hf
    hf_hi = hf.astype(BF16)
    hf_lo = (hf - hf_hi.astype(F32)).astype(BF16)
    logits = (_dot_nt(wrt_ref[0], hf_hi) + _dot_nt(wrt_ref[0], hf_lo) + _dot_nt(wrt_ref[1], hf_hi)
              + br_ref[...])
    erow = lax.broadcasted_iota(I32, (n_exp, tm), 0).astype(F32)
    picked = jnp.zeros((n_exp, tm), F32)
    vals, onehots, idxs = [], [], []
    for k in range(TOP_K):
        mx = jnp.max(logits, axis=0, keepdims=True)
        idx = jnp.min(jnp.where(logits == mx, erow, float(n_exp)), axis=0, keepdims=True)
        sel = erow == idx
        vals.append(mx)
        onehots.append(sel)
        idxs.append(idx)
        picked = picked + sel.astype(F32)
        logits = jnp.where(sel, -jnp.inf, logits)
    ex = [jnp.exp(v - vals[0]) for v in vals]
    den = ex[0] + ex[1] + ex[2] + ex[3]
    for k in range(TOP_K):
        gate_ref[k:k + 1, :] = ex[k] / den
    r = lax.broadcasted_iota(I32, (tm, tm), 0)
    cidx = lax.broadcasted_iota(I32, (tm, tm), 1)
    upper = jnp.where(r < cidx, 1.0, 0.0).astype(BF16)
    before = _dot(picked.astype(BF16), upper) + carry_ref[...]
    for k in range(TOP_K):
        rank = jnp.sum(jnp.where(onehots[k], before, 0.0), axis=0, keepdims=True)
        idx_ref[k:k + 1, :] = idxs[k].astype(I32)
        rank_ref[k:k + 1, :] = rank.astype(I32)
    carry_ref[...] = carry_ref[...] + jnp.sum(picked, axis=1, keepdims=True)
    cnt_ref[...] = jnp.broadcast_to(carry_ref[...], cnt_ref.shape).astype(I32)


def _post_attn_prompt_kernel(x_ref, fo_ref, mo_ref, ga_ref, gb_ref, mk_ref, mv_ref,
                             wof_ref, wom_ref, wout_ref, gx_ref, wxq_ref, wxo_ref, gffn_ref, wrt_ref, br_ref,
                             x2_ref, hf_ref, idx_ref, rank_ref, gate_ref, cnt_ref, carry_ref,
                             *, x_heads, x_scale):
    @pl.when(pl.program_id(0) == 0)
    def _():
        carry_ref[...] = jnp.zeros_like(carry_ref)

    x1, xq = _mix_part(x_ref[...], fo_ref[...], mo_ref[...], ga_ref[...], gb_ref[...],
                       wof_ref, wom_ref, wout_ref, gx_ref, wxq_ref, x_scale)
    ca = _cross_part(xq, mk_ref[0].astype(BF16), mv_ref[0].astype(BF16), x_heads)
    _route_part(x1, ca, wxo_ref, gffn_ref, wrt_ref, br_ref, carry_ref,
                x2_ref, hf_ref, idx_ref, rank_ref, gate_ref, cnt_ref)


def _route_out(T, D, n_exp, tm):
    tok = lambda n: pl.BlockSpec((tm, n), lambda i: (i, 0))
    col = pl.BlockSpec((TOP_K, tm), lambda i: (0, i))
    sds = jax.ShapeDtypeStruct
    shapes = [sds((T, D), F32), sds((T, D), F32), sds((TOP_K, T), I32), sds((TOP_K, T), I32),
              sds((TOP_K, T), F32), sds((n_exp, LANES), I32)]
    specs = [tok(D), tok(D), col, col, col, pl.BlockSpec((n_exp, LANES), lambda i: (0, 0))]
    return shapes, specs


def _post_attn_prompt(x, fo, mo, ga, gb, mk, mv, w, *, tm, seq):
    T, D = x.shape
    n_exp = w['wrt'].shape[1]
    tiles_per_seq = seq // tm
    tok = lambda n: pl.BlockSpec((tm, n), lambda i: (i, 0))
    mem = pl.BlockSpec((1,) + mk.shape[1:], lambda i: (i // tiles_per_seq, 0, 0))
    weights = [w['wof'], w['wom'], w['wout'], w['gx'], w['wxq'], w['wxo'], w['gffn'], w['wrt'], w['br']]
    shapes, specs = _route_out(T, D, n_exp, tm)
    return pl.pallas_call(
        functools.partial(_post_attn_prompt_kernel, x_heads=w['x_heads'], x_scale=w['x_scale']),
        grid=(T // tm,),
        in_specs=[tok(D), tok(fo.shape[1]), tok(mo.shape[1]), tok(D), tok(D), mem, mem]
        + [_const_spec(a.shape) for a in weights],
        out_specs=specs,
        out_shape=shapes,
        scratch_shapes=[pltpu.VMEM((n_exp, 1), F32)],
        compiler_params=_cparams("arbitrary"),
        name="post_attn_prompt",
    )(x, fo, mo, ga, gb, mk, mv, *weights)


def _mix_sample_kernel(x_ref, fo_ref, mo_ref, ga_ref, gb_ref, wof_ref, wom_ref, wout_ref, gx_ref, wxq_ref,
                       x1_ref, xq_ref, *, x_scale):
    x1, xq = _mix_part(x_ref[...], fo_ref[...], mo_ref[...], ga_ref[...], gb_ref[...],
                       wof_ref, wom_ref, wout_ref, gx_ref, wxq_ref, x_scale)
    x1_ref[...] = x1
    xq_ref[...] = xq


def _cross_sample_kernel(xq_ref, mk_ref, mv_ref, o_ref, *, x_heads):
    o_ref[0] = _cross_part(xq_ref[0], mk_ref[0].astype(BF16), mv_ref[0].astype(BF16), x_heads)


def _route_sample_kernel(x1_ref, ca_ref, wxo_ref, gffn_ref, wrt_ref, br_ref,
                         x2_ref, hf_ref, idx_ref, rank_ref, gate_ref, cnt_ref, carry_ref):
    carry_ref[...] = jnp.zeros_like(carry_ref)
    _route_part(x1_ref[...], ca_ref[...], wxo_ref, gffn_ref, wrt_ref, br_ref, carry_ref,
                x2_ref, hf_ref, idx_ref, rank_ref, gate_ref, cnt_ref)


def _post_attn_sample(x, fo, mo, ga, gb, mk, mv, w, *, t_new):
    T, D = x.shape
    B = T // t_new
    n_exp = w['wrt'].shape[1]
    xw = w['wxq'].shape[1]
    params = pltpu.CompilerParams(vmem_limit_bytes=VMEM_LIMIT_BYTES)
    x1, xq = pl.pallas_call(
        functools.partial(_mix_sample_kernel, x_scale=w['x_scale']),
        out_shape=[jax.ShapeDtypeStruct((T, D), F32), jax.ShapeDtypeStruct((T, xw), BF16)],
        compiler_params=params,
        name="mix_sample",
    )(x, fo, mo, ga, gb, w['wof'], w['wom'], w['wout'], w['gx'], w['wxq'])
    ca = pl.pallas_call(
        functools.partial(_cross_sample_kernel, x_heads=w['x_heads']),
        grid=(B,),
        in_specs=[pl.BlockSpec((1, t_new, xw), lambda b: (b, 0, 0)),
                  pl.BlockSpec((1,) + mk.shape[1:], lambda b: (b, 0, 0)),
                  pl.BlockSpec((1,) + mv.shape[1:], lambda b: (b, 0, 0))],
        out_specs=pl.BlockSpec((1, t_new, xw), lambda b: (b, 0, 0)),
        out_shape=jax.ShapeDtypeStruct((B, t_new, xw), BF16),
        compiler_params=_cparams("parallel"),
        name="cross_sample",
    )(xq.reshape(B, t_new, xw), mk, mv)
    shapes, _ = _route_out(T, D, n_exp, T)
    return pl.pallas_call(
        _route_sample_kernel,
        out_shape=shapes,
        scratch_shapes=[pltpu.VMEM((n_exp, 1), F32)],
        compiler_params=params,
        name="route_sample",
    )(x1, ca.reshape(T, xw), w['wxo'], w['gffn'], w['wrt'], w['br'])


def _mem_kv_kernel(m_ref, g_ref, wk_ref, wv_ref, k_ref, v_ref):
    m = _rms(m_ref[...], g_ref[...]).astype(BF16)
    k_ref[...] = _dot(m, wk_ref[...])
    v_ref[...] = _dot(m, wv_ref[...])


def _mem_kv(mem, w, *, tm):
    T, D = mem.shape
    xw = w['wxk'].shape[1]
    tok = lambda n: pl.BlockSpec((tm, n), lambda i: (i, 0))
    return pl.pallas_call(
        _mem_kv_kernel,
        grid=(T // tm,),
        in_specs=[tok(D), _const_spec(w['gmem'].shape), _const_spec(w['wxk'].shape), _const_spec(w['wxv'].shape)],
        out_specs=[tok(xw), tok(xw)],
        out_shape=[jax.ShapeDtypeStruct((T, xw), F32)] * 2,
        compiler_params=_cparams("parallel"),
        name="mem_kv",
    )(mem, w['gmem'], w['wxk'], w['wxv'])


def _dest_kernel(start_ref, idx_ref, rank_ref, dest_ref):
    idx = idx_ref[...]
    dest = rank_ref[...]
    for e in range(start_ref.shape[0]):
        dest = dest + jnp.where(idx == e, start_ref[e], 0)
    dest_ref[...] = dest


def _dest_rows(pad_start, idx_t, rank_t):
    whole = pl.BlockSpec(idx_t.shape, lambda i, s: (0, 0))
    return pl.pallas_call(
        _dest_kernel,
        grid_spec=pltpu.PrefetchScalarGridSpec(num_scalar_prefetch=1, grid=(1,), in_specs=[whole, whole],
                                               out_specs=whole),
        out_shape=jax.ShapeDtypeStruct(idx_t.shape, I32),
        name="moe_dest",
    )(pad_start, idx_t, rank_t)


def _row_copy(src_ref, src_row, dst_ref, dst_row, sem):
    return pltpu.make_async_copy(src_ref.at[pl.ds(src_row, 1), :], dst_ref.at[pl.ds(dst_row, 1), :], sem)


def _zero_unassigned_rows(cnt_ref, start_ref, nu_ref, xs_ref, zero_ref, sem, bm):
    zero_ref[...] = jnp.zeros_like(zero_ref)
    half = zero_ref.shape[0]
    n_blocks = xs_ref.shape[0] // bm

    def half_block(first_row):
        return pltpu.make_async_copy(zero_ref, xs_ref.at[pl.ds(pl.multiple_of(first_row, half), half)], sem)

    def zero_block(first_row):
        half_block(first_row).start()
        half_block(first_row + half).start()

    def expert_last_block(e, n):
        cnt = cnt_ref[e]
        partial = (cnt & (bm - 1)) != 0

        @pl.when(partial)
        def _():
            zero_block(start_ref[e] + (cnt & -bm))

        return n + partial.astype(I32)

    n_zeroed = lax.fori_loop(0, cnt_ref.shape[0], expert_last_block, 0)

    def tail_block(j, _):
        zero_block(j * bm)
        return 0

    lax.fori_loop(nu_ref[0], n_blocks, tail_block, 0)

    def drain(i, _):
        half_block(0).wait()
        half_block(0).wait()
        return 0

    lax.fori_loop(0, n_zeroed + n_blocks - nu_ref[0], drain, 0)


def _dispatch_kernel(cnt_ref, start_ref, nu_ref, dest_ref, h_ref, xs_ref, zero_ref, sem, zero_sem, *, bm):
    tm = h_ref.shape[0]

    @pl.when(pl.program_id(0) == 0)
    def _():
        _zero_unassigned_rows(cnt_ref, start_ref, nu_ref, xs_ref, zero_ref, zero_sem, bm)

    def issue(t, _):
        for k in range(TOP_K):
            _row_copy(h_ref, t, xs_ref, dest_ref[k, t], sem).start()
        return 0

    lax.fori_loop(0, tm, issue, 0, unroll=ROW_DMA_UNROLL)

    def drain(t, _):
        for k in range(TOP_K):
            _row_copy(h_ref, 0, xs_ref, 0, sem).wait()
        return 0

    lax.fori_loop(0, tm, drain, 0, unroll=2 * ROW_DMA_UNROLL)


def _dispatch(counts, pad_start, n_used, dest_t, h, n_rows, *, tm, bm):
    T, D = h.shape
    return pl.pallas_call(
        functools.partial(_dispatch_kernel, bm=bm),
        grid_spec=pltpu.PrefetchScalarGridSpec(
            num_scalar_prefetch=3, grid=(T // tm,),
            in_specs=[pl.BlockSpec((TOP_K, tm), lambda i, c, s, n: (0, i), memory_space=pltpu.SMEM),
                      pl.BlockSpec((tm, D), lambda i, c, s, n: (i, 0))],
            out_specs=pl.BlockSpec(memory_space=pl.ANY),
            scratch_shapes=[pltpu.VMEM((bm // 2, D), h.dtype), pltpu.SemaphoreType.DMA,
                            pltpu.SemaphoreType.DMA]),
        out_shape=jax.ShapeDtypeStruct((n_rows, D), h.dtype),
        compiler_params=_cparams("arbitrary"),
        name="moe_dispatch",
    )(counts, pad_start, n_used, dest_t, h)


def _expert_kernel(be_ref, nu_ref, x_ref, wgu_ref, bgu_ref, wd_ref, bd_ref, y_ref, *, ff):
    used = pl.program_id(0) < nu_ref[0]

    @pl.when(jnp.logical_not(used))
    def _():
        y_ref[...] = jnp.zeros_like(y_ref)

    @pl.when(used)
    def _():
        gu = _dot(x_ref[...].astype(BF16), wgu_ref[0]) + bgu_ref[0]
        gate = jnp.minimum(gu[:, :ff], SWIGLU_LIMIT)
        up = jnp.clip(gu[:, ff:], -SWIGLU_LIMIT, SWIGLU_LIMIT)
        act = (up + 1.0) * (gate * _sigmoid(gate * SWIGLU_ALPHA))
        y_ref[...] = _dot(act.astype(BF16), wd_ref[0]) + bd_ref[0]


def _expert_ffn(block_e, n_used, xs, w, *, bm):
    D = xs.shape[1]
    ff = w['wd'].shape[1]
    rows = pl.BlockSpec((bm, D), lambda i, be, nu: (i, 0))
    exp = lambda shape: pl.BlockSpec((1,) + shape, lambda i, be, nu: (be[i], 0, 0))
    return pl.pallas_call(
        functools.partial(_expert_kernel, ff=ff),
        grid_spec=pltpu.PrefetchScalarGridSpec(
            num_scalar_prefetch=2, grid=(xs.shape[0] // bm,),
            in_specs=[rows, exp((D, 2 * ff)), exp((1, 2 * ff)), exp((ff, D)), exp((1, D))],
            out_specs=rows),
        out_shape=jax.ShapeDtypeStruct(xs.shape, F32),
        compiler_params=_cparams("arbitrary"),
        name="moe_expert_ffn",
    )(block_e, n_used, xs, w['wgu'], w['bgu'], w['wd'], w['bd'])


def _combine_kernel(dest_ref, dest_next_ref, gate_ref, x2_ref, gfin_ref, yb_ref, y_ref, buf_ref, sem):
    i = pl.program_id(0)
    tm = x2_ref.shape[0]
    slot = i % 2

    def gather(d_ref, s):
        def issue(t, _):
            for k in range(TOP_K):
                _row_copy(yb_ref, d_ref[k, t], buf_ref.at[s, k], t, sem.at[s]).start()
            return 0

        lax.fori_loop(0, tm, issue, 0, unroll=ROW_DMA_UNROLL)

    @pl.when(i == 0)
    def _():
        gather(dest_ref, 0)

    @pl.when(i + 1 < pl.num_programs(0))
    def _():
        gather(dest_next_ref, 1 - slot)

    def drain(t, _):
        for k in range(TOP_K):
            _row_copy(yb_ref, 0, buf_ref.at[slot, k], 0, sem.at[slot]).wait()
        return 0

    lax.fori_loop(0, tm, drain, 0, unroll=2 * ROW_DMA_UNROLL)
    gate = gate_ref[...]
    y = x2_ref[...]
    for k in range(TOP_K):
        y = y + gate[:, k:k + 1] * buf_ref[slot, k]
    y_ref[...] = _rms(y, gfin_ref[...])


def _combine(dest_t, gate_tok, x2, gfin, yb, *, tm):
    T, D = x2.shape
    last = T // tm - 1
    return pl.pallas_call(
        _combine_kernel,
        grid=(T // tm,),
        in_specs=[pl.BlockSpec((TOP_K, tm), lambda i: (0, i), memory_space=pltpu.SMEM),
                  pl.BlockSpec((TOP_K, tm), lambda i: (0, jnp.minimum(i + 1, last)), memory_space=pltpu.SMEM),
                  pl.BlockSpec((tm, TOP_K), lambda i: (i, 0)),
                  pl.BlockSpec((tm, D), lambda i: (i, 0)),
                  _const_spec(gfin.shape),
                  pl.BlockSpec(memory_space=pl.ANY)],
        out_specs=pl.BlockSpec((tm, D), lambda i: (i, 0)),
        out_shape=jax.ShapeDtypeStruct((T, D), F32),
        scratch_shapes=[pltpu.VMEM((2, TOP_K, tm, D), F32), pltpu.SemaphoreType.DMA((2,))],
        compiler_params=_cparams("arbitrary"),
        name="moe_combine",
    )(dest_t, dest_t, gate_tok, x2, gfin, yb)


def _moe_and_final_norm(x2, hf, idx_t, rank_t, gate_t, counts, w, gfin, *, bm, tm):
    T = x2.shape[0]
    n_exp = counts.shape[0]
    n_blocks = -(-(T * TOP_K + n_exp * (bm - 1)) // bm)
    padded = (counts + bm - 1) // bm * bm
    pad_end = jnp.cumsum(padded)
    pad_start = (pad_end - padded).astype(I32)
    n_used = (pad_end[n_exp - 1:] // bm).astype(I32)
    first_row = jnp.arange(n_blocks, dtype=I32) * bm
    block_e = jnp.minimum(jnp.sum(first_row[:, None] >= pad_end[None, :], axis=1), n_exp - 1).astype(I32)
    dest_t = _dest_rows(pad_start, idx_t, rank_t)
    xs = _dispatch(counts.astype(I32), pad_start, n_used, dest_t, hf, n_blocks * bm, tm=tm, bm=bm)
    yb = _expert_ffn(block_e, n_used, xs, w, bm=bm)
    return _combine(dest_t, gate_t.T, x2, gfin, yb, tm=tm)


def _rot_cols(wc):
    half = wc.shape[1] // 2
    return jnp.concatenate([-wc[:, half:], wc[:, :half]], axis=1)


def _prep_layer(p, dims):
    fw, nh, nope, rope, lat, vdim, x_heads, x_hd = dims
    D = p['w_in'].shape[0]
    w_in = p['w_in']
    o = 0
    wfox = w_in[:, o:o + 3 * fw]; o += 3 * fw
    wfl = w_in[:, o:o + nh]; o += nh
    qlora = p['g_q'].shape[0]
    wqc = w_in[:, o:o + qlora]; o += qlora
    wkvc = w_in[:, o:o + lat]; o += lat
    wkr = w_in[:, o:o + rope]; o += rope
    wga = w_in[:, o:o + D]; o += D
    wgb = w_in[:, o:o + D]

    def misc_cols(parts):
        out = jnp.zeros((D, LANES), F32)
        for off, cols in parts:
            out = out.at[:, off:off + cols.shape[1]].set(cols)
        return out

    wma = misc_cols([(MISC_LOGF, wfl), (MISC_CUM, wfl), (MISC_ROPE, wkr)])
    wmb = misc_cols([(MISC_ROPE, _rot_cols(wkr))])
    bf = jnp.zeros((1, LANES), F32).at[0, MISC_LOGF:MISC_LOGF + nh].set(p['b_f']).at[0, MISC_CUM:MISC_CUM + nh].set(p['b_f'])

    wuq = p['w_uq'].reshape(qlora, nh, nope + rope)
    wqa = jnp.zeros((qlora, nh, LANES), F32).at[:, :, :nope + rope].set(wuq)
    rot = jnp.concatenate([-wuq[:, :, nope + rope // 2:], wuq[:, :, nope:nope + rope // 2]], axis=2)
    wqb = jnp.zeros((qlora, nh, LANES), F32).at[:, :, nope:nope + rope].set(rot)
    wk = jnp.zeros((lat + LANES, nh, LANES), F32).at[:lat, :, :nope].set(p['w_uk'])
    place = jnp.zeros((LANES, nh, LANES), F32)
    j = jnp.arange(rope)
    place = place.at[MISC_ROPE + j, :, nope + j].set(1.0)
    wk = wk.at[lat:, :, :].set(place)
    wukp = jnp.zeros((nh, LANES, lat), F32).at[:, :nope, :].set(jnp.transpose(p['w_uk'], (1, 2, 0)))

    b = lambda a: a.astype(BF16)
    row = lambda a: a.reshape(1, -1).astype(F32)
    n_exp = p['w_router'].shape[1]
    wrt = p['w_router'].T.astype(F32)
    wrt_hi = b(wrt)
    wrt = jnp.stack([wrt_hi, b(wrt - wrt_hi.astype(F32))])
    return {
        'fox_width': fw, 'n_heads': nh, 'fox_scale': float(fw // nh) ** -0.5 * LOG2E,
        'x_heads': x_heads, 'x_scale': float(x_hd) ** -0.5 * LOG2E,
        'gmix': row(p['g_mix']), 'wfox': b(wfox), 'wma': b(wma), 'wmb': b(wmb), 'wqc': b(wqc), 'wkvc': b(wkvc),
        'wga': b(wga), 'wgb': b(wgb), 'bf': bf, 'gq': row(p['g_q']),
        'wqa': b(wqa.reshape(qlora, nh * LANES)), 'wqb': b(wqb.reshape(qlora, nh * LANES)),
        'gkv': row(p['g_kv']), 'wk': b(wk.reshape(lat + LANES, nh * LANES)),
        'wuv': b(p['w_uv'].reshape(lat, nh * vdim)), 'wukp': b(wukp),
        'wof': b(p['w_o_fox']), 'wom': b(p['w_o_mla']), 'wout': b(p['w_out']), 'gx': row(p['g_x']),
        'wxq': b(p['w_xq']), 'wxo': b(p['w_xo']), 'gffn': row(p['g_ffn']),
        'wrt': wrt, 'br': p['b_router'].reshape(n_exp, 1).astype(F32),
        'gmem': row(p['g_mem']), 'wxk': b(p['w_xk']), 'wxv': b(p['w_xv']),
        'wgu': b(p['w_gu']), 'bgu': p['b_gu'].reshape(n_exp, 1, -1).astype(F32),
        'wd': b(p['w_down']), 'bd': p['b_down'].reshape(n_exp, 1, -1).astype(F32),
    }


def _rope_tables(pos, nope, rope, q_scale):
    half = rope // 2
    inv_freq = ROPE_BASE ** (-jnp.arange(half, dtype=F32) / half)
    ang = pos.astype(F32)[:, None] * inv_freq[None, :]
    cos = jnp.concatenate([jnp.cos(ang)] * 2, axis=1)
    sin = jnp.concatenate([jnp.sin(ang)] * 2, axis=1)
    n = pos.shape[0]
    ck = jnp.zeros((n, LANES), F32).at[:, :MISC_ROPE].set(1.0).at[:, MISC_ROPE:MISC_ROPE + rope].set(cos)
    sk = jnp.zeros((n, LANES), F32).at[:, MISC_ROPE:MISC_ROPE + rope].set(sin)
    cq = jnp.zeros((n, LANES), F32).at[:, :nope].set(1.0).at[:, nope:nope + rope].set(cos) * q_scale
    sq = jnp.zeros((n, LANES), F32).at[:, nope:nope + rope].set(sin) * q_scale
    return ck, sk, cq, sq


def _pick_tile(n, target):
    t = min(n, target)
    while n % t:
        t //= 2
    return t


def _prompt_layer(x, mem, w, dims):
    fw, nh, nope, rope, lat, vdim, x_heads, x_hd = dims
    B, S, D = x.shape
    T = B * S
    tm = _pick_tile(S, PROJ_TILE)
    tables = _rope_tables(jnp.arange(S, dtype=I32), nope, rope, float(nope + rope) ** -0.5 * LOG2E)
    fq, fk, fv, fkb, fvb, misc, ckv, qm, ga, gb, km, vm = _in_proj(
        x.reshape(T, D), w, tables, tm=tm, tiles_per_seq=S // tm, prompt=True)

    blk = _pick_tile(S, ATTN_BLOCK)
    cum = misc[:, MISC_CUM:MISC_CUM + nh].reshape(B, S // blk, blk, nh)
    cum_row = jnp.swapaxes(cum, 2, 3)
    r3 = lambda a: a.reshape(B, S, a.shape[1])
    fox_o = _prompt_attn(r3(fq), r3(fkb), r3(fvb), r3(misc), cum_row, blk=blk, n_heads=nh, fox=True)
    mla_o = _prompt_attn(r3(qm), r3(km), r3(vm), None, None, blk=blk, n_heads=nh, fox=False)

    n_mem = mem.shape[1]
    mk, mv = _mem_kv(mem.reshape(B * n_mem, D), w, tm=_pick_tile(B * n_mem, 512))
    xw = mk.shape[1]
    routed = _post_attn_prompt(
        x.reshape(T, D), fox_o.reshape(T, fw), mla_o.reshape(T, nh * vdim), ga, gb,
        mk.reshape(B, n_mem, xw), mv.reshape(B, n_mem, xw), w, tm=tm, seq=S)
    caches = (fk.reshape(B, S, nh, fw // nh), fv.reshape(B, S, nh, fw // nh),
              misc[:, MISC_LOGF:MISC_LOGF + nh].reshape(B, S, nh), ckv.reshape(B, S, lat),
              misc[:, MISC_ROPE:MISC_ROPE + rope].reshape(B, S, rope),
              mk.reshape(B, n_mem, x_heads, x_hd), mv.reshape(B, n_mem, x_heads, x_hd))
    return routed, caches


def _sample_layer(x, c_fk, c_fv, c_logf, c_ckv, c_kr, c_mk, c_mv, w, dims):
    fw, nh, nope, rope, lat, vdim, x_heads, x_hd = dims
    B, t_new, D = x.shape
    P = c_fk.shape[1]
    T = B * t_new
    pos = P + jnp.tile(jnp.arange(t_new, dtype=I32), B)
    tables = _rope_tables(pos, nope, rope, float(nope + rope) ** -0.5 * LOG2E)
    fq, fk, fv, fkb, fvb, misc, ckv, qm, ga, gb = _in_proj(
        x.reshape(T, D), w, tables, tm=T, tiles_per_seq=1, prompt=False)

    logf_new = misc[:, MISC_LOGF:MISC_LOGF + nh].reshape(B, t_new, nh)
    lf = jnp.concatenate([c_logf.astype(F32), logf_new, jnp.zeros((B, LANES - t_new, nh), F32)], axis=1)
    cum_all = _lane_cumsum(jnp.swapaxes(lf, 1, 2).reshape(B * nh, P + LANES)).reshape(B, nh, P + LANES)
    cum_q = cum_all[:, :, P:P + t_new].reshape(B, nh * t_new, 1)

    chunk = _pick_tile(P, CACHE_CHUNK)
    r3 = lambda a: a.reshape(B, t_new, a.shape[1])
    cum_flat = jnp.swapaxes(cum_all, 1, 2).reshape(B, 1, (P + LANES) * nh)
    fox_o = _fox_sample_attn(r3(fq), c_fk.reshape(B, P * nh, fw // nh), c_fv.reshape(B, P * nh, fw // nh),
                             r3(fkb), r3(fvb), cum_q, cum_flat, cum_all, n_heads=nh, chunk=chunk)
    mla_o = _mla_sample_attn(r3(qm), w['wukp'], c_ckv, c_kr, r3(ckv), r3(misc), w['wuv'],
                             n_heads=nh, chunk=chunk, nope=nope, rope=rope)
    n_mem = c_mk.shape[1]
    routed = _post_attn_sample(
        x.reshape(T, D), fox_o.reshape(T, fw), mla_o.reshape(T, nh * vdim), ga, gb,
        c_mk.reshape(B, n_mem, x_heads * x_hd), c_mv.reshape(B, n_mem, x_heads * x_hd), w, t_new=t_new)
    caches = (fk.reshape(B, t_new, nh, fw // nh), fv.reshape(B, t_new, nh, fw // nh), logf_new,
              ckv.reshape(B, t_new, lat), misc[:, MISC_ROPE:MISC_ROPE + rope].reshape(B, t_new, rope))
    return routed, caches


def kernel(x_prompt, x_sample, mem_prompt, cache_fox_k, cache_fox_v, cache_fox_logf, cache_mla_ckv,
           cache_mla_krope, cache_mem_k, cache_mem_v, g_mix, w_in, b_f, g_q, w_uq, g_kv, w_uk, w_uv,
           w_o_fox, w_o_mla, w_out, g_x, g_mem, w_xq, w_xk, w_xv, w_xo, g_ffn, w_router, b_router,
           w_gu, b_gu, w_down, b_down, g_final):
    depth = w_in.shape[0]
    assert depth == 1, "the fused final norm assumes a single layer"
    nh, fhd = cache_fox_k.shape[3:]
    lat = cache_mla_ckv.shape[3]
    rope = cache_mla_krope.shape[3]
    nope, vdim = w_uk.shape[3], w_uv.shape[3]
    x_heads, x_hd = cache_mem_k.shape[3:]
    dims = (nh * fhd, nh, nope, rope, lat, vdim, x_heads, x_hd)
    B, S, D = x_prompt.shape
    Bs, t_new, _ = x_sample.shape
    gfin = g_final.reshape(1, D).astype(F32)

    l = 0
    p = {'g_mix': g_mix[l], 'w_in': w_in[l], 'b_f': b_f[l], 'g_q': g_q[l], 'w_uq': w_uq[l],
         'g_kv': g_kv[l], 'w_uk': w_uk[l], 'w_uv': w_uv[l], 'w_o_fox': w_o_fox[l],
         'w_o_mla': w_o_mla[l], 'w_out': w_out[l], 'g_x': g_x[l], 'g_mem': g_mem[l],
         'w_xq': w_xq[l], 'w_xk': w_xk[l], 'w_xv': w_xv[l], 'w_xo': w_xo[l], 'g_ffn': g_ffn[l],
         'w_router': w_router[l], 'b_router': b_router[l], 'w_gu': w_gu[l], 'b_gu': b_gu[l],
         'w_down': w_down[l], 'b_down': b_down[l]}
    w = _prep_layer(p, dims)

    (*routed_p, cnt_p), pc = _prompt_layer(x_prompt, mem_prompt, w, dims)
    (*routed_s, cnt_s), sc = _sample_layer(
        x_sample, cache_fox_k[l], cache_fox_v[l], cache_fox_logf[l], cache_mla_ckv[l],
        cache_mla_krope[l], cache_mem_k[l], cache_mem_v[l], w, dims)
    y_prompt = _moe_and_final_norm(*routed_p, cnt_p[:, 0], w, gfin,
                                   bm=EXPERT_ROWS_PROMPT, tm=TOKEN_TILE).reshape(B, S, D)
    y_sample = _moe_and_final_norm(*routed_s, cnt_s[:, 0], w, gfin, bm=EXPERT_ROWS_SAMPLE,
                                   tm=_pick_tile(Bs * t_new, TOKEN_TILE)).reshape(Bs, t_new, D)
    return (y_prompt, y_sample) + tuple(a[None] for a in pc) + tuple(a[None] for a in sc)
```

```python
import functools
import math

import jax
import jax.numpy as jnp
from jax import lax
from jax.experimental import pallas as pl
from jax.experimental.pallas import tpu as pltpu

F32 = jnp.float32
BF16 = jnp.bfloat16
I32 = jnp.int32

CHUNK = 64
EPS = 1e-6
NEG_INF = -1e30
ROPE_BASE = 10000.0
TOP_K = 4
SWIGLU_LIMIT = 7.0
SWIGLU_ALPHA = 1.702
LOG2E = math.log2(math.e)

LANES = 128
SUBLANES = 8
VMEM_LIMIT_BYTES = 56 * 1024 * 1024

MISC_LOGF = 0
MISC_CUM = 8
MISC_ROPE = 16

PROJ_TILE = 512
TOKEN_TILE = 256
ATTN_BLOCK = 256
CACHE_CHUNK = 1024
EXPERT_ROWS_PROMPT = 512
EXPERT_ROWS_SAMPLE = 128
ROW_DMA_UNROLL = 4


def _cparams(*sem):
    return pltpu.CompilerParams(dimension_semantics=sem, vmem_limit_bytes=VMEM_LIMIT_BYTES)


def _const_spec(shape):
    nd = len(shape)
    return pl.BlockSpec(shape, lambda *_: (0,) * nd, pipeline_mode=pl.Buffered(1))


def _rms(x, g):
    return x * lax.rsqrt(jnp.mean(x * x, axis=-1, keepdims=True) + EPS) * g


def _sigmoid(x):
    return 1.0 / (1.0 + jnp.exp(-x))


def _log_sigmoid(x):
    return jnp.minimum(x, 0.0) - jnp.log1p(jnp.exp(-jnp.abs(x)))


def _prefix_sum(c, axis):
    n = c.shape[axis]
    pos = lax.broadcasted_iota(I32, c.shape, axis)
    s = 1
    while s < n:
        c = c + jnp.where(pos >= s, pltpu.roll(c, s, axis), 0.0)
        s *= 2
    return c


def _dot(a, b):
    return jnp.dot(a, b, preferred_element_type=F32)


def _dot_nt(a, b):
    return lax.dot_general(a, b, (((1,), (1,)), ((), ())), preferred_element_type=F32)


def _tile_lanes(t, n):
    return jnp.concatenate([t] * n, axis=1)


def _in_proj_kernel(x_ref, gmix_ref, wfox_ref, wma_ref, wmb_ref, wqc_ref, wkvc_ref, wga_ref, wgb_ref,
                    bf_ref, gq_ref, wqa_ref, wqb_ref, gkv_ref, wk_ref, wuv_ref,
                    ck_ref, sk_ref, cq_ref, sq_ref,
                    fq_ref, fk_ref, fv_ref, fkb_ref, fvb_ref, misc_ref, ckv_ref, qm_ref, ga_ref, gb_ref, *rest,
                    fox_width, n_heads, fox_scale, tiles_per_seq, prompt):
    if prompt:
        km_ref, vm_ref, carry_ref = rest
    tm = x_ref.shape[0]
    h = _rms(x_ref[...], gmix_ref[...]).astype(BF16)

    fq_ref[...] = (_dot(h, wfox_ref[:, 0:fox_width]) * fox_scale).astype(BF16)
    fk = _dot(h, wfox_ref[:, fox_width:2 * fox_width])
    fv = _dot(h, wfox_ref[:, 2 * fox_width:3 * fox_width])
    fk_ref[...] = fk
    fv_ref[...] = fv
    fkb_ref[...] = fk.astype(BF16)
    fvb_ref[...] = fv.astype(BF16)

    lane = lax.broadcasted_iota(I32, (tm, LANES), 1)
    pre = _dot(h, wma_ref[...]) * ck_ref[...] + _dot(h, wmb_ref[...]) * sk_ref[...]
    misc = jnp.where(lane < MISC_ROPE, _log_sigmoid(pre + bf_ref[...]), pre)
    if prompt:
        i = pl.program_id(0)

        @pl.when(i % tiles_per_seq == 0)
        def _():
            carry_ref[...] = jnp.zeros_like(carry_ref)

        in_cum = (lane >= MISC_CUM) & (lane < MISC_ROPE)
        cum = _prefix_sum(jnp.where(in_cum, misc * LOG2E, 0.0), 0) + carry_ref[...]
        carry_ref[...] = cum[tm - 1:tm, :]
        misc = jnp.where(in_cum, cum, misc)
    misc_ref[...] = misc

    ckv = _rms(_dot(h, wkvc_ref[...]), gkv_ref[...])
    ckv_ref[...] = ckv

    qn = _rms(_dot(h, wqc_ref[...]), gq_ref[...]).astype(BF16)
    cq = _tile_lanes(cq_ref[...], n_heads)
    sq = _tile_lanes(sq_ref[...], n_heads)
    qm_ref[...] = (_dot(qn, wqa_ref[...]) * cq + _dot(qn, wqb_ref[...]) * sq).astype(BF16)

    if prompt:
        ckv_b = ckv.astype(BF16)
        km_ref[...] = _dot(jnp.concatenate([ckv_b, misc.astype(BF16)], axis=1), wk_ref[...]).astype(BF16)
        vm_ref[...] = _dot(ckv_b, wuv_ref[...]).astype(BF16)

    ga_ref[...] = _sigmoid(_dot(h, wga_ref[...])).astype(BF16)
    gb_ref[...] = _sigmoid(_dot(h, wgb_ref[...])).astype(BF16)


def _in_proj(x, w, tables, *, tm, tiles_per_seq, prompt):
    T, D = x.shape
    fw = w['fox_width']
    nh = w['n_heads']
    lat = w['wkvc'].shape[1]
    hw = nh * LANES
    ck, sk, cq, sq = tables
    tok = lambda n: pl.BlockSpec((tm, n), lambda i: (i, 0))
    if prompt:
        tab = pl.BlockSpec((tm, LANES), lambda i: (i % tiles_per_seq, 0))
    else:
        tab = tok(LANES)
    weights = [w['gmix'], w['wfox'], w['wma'], w['wmb'], w['wqc'], w['wkvc'], w['wga'], w['wgb'],
               w['bf'], w['gq'], w['wqa'], w['wqb'], w['gkv'], w['wk'], w['wuv']]
    sds = jax.ShapeDtypeStruct
    out_shape = [sds((T, fw), BF16), sds((T, fw), F32), sds((T, fw), F32),
                 sds((T, fw), BF16), sds((T, fw), BF16), sds((T, LANES), F32),
                 sds((T, lat), F32), sds((T, hw), BF16), sds((T, D), BF16), sds((T, D), BF16)]
    out_specs = [tok(fw), tok(fw), tok(fw), tok(fw), tok(fw), tok(LANES), tok(lat), tok(hw), tok(D), tok(D)]
    scratch = []
    if prompt:
        out_shape += [sds((T, hw), BF16), sds((T, w['wuv'].shape[1]), BF16)]
        out_specs += [tok(hw), tok(w['wuv'].shape[1])]
        scratch = [pltpu.VMEM((1, LANES), F32)]
    kern = functools.partial(_in_proj_kernel, fox_width=fw, n_heads=nh, fox_scale=w['fox_scale'],
                             tiles_per_seq=tiles_per_seq, prompt=prompt)
    return pl.pallas_call(
        kern,
        grid=(T // tm,),
        in_specs=[tok(D)] + [_const_spec(a.shape) for a in weights] + [tab] * 4,
        out_specs=out_specs,
        out_shape=out_shape,
        scratch_shapes=scratch,
        compiler_params=_cparams("arbitrary"),
        name="in_proj_prompt" if prompt else "in_proj_sample",
    )(x, *weights, ck, sk, cq, sq)


def _fold_lanes(x, op):
    out = x[:, 0:LANES]
    for j in range(1, x.shape[1] // LANES):
        out = op(out, x[:, j * LANES:(j + 1) * LANES])
    return out


def _prompt_attn_kernel(*refs, blk, n_heads, fox):
    if fox:
        q_ref, k_ref, v_ref, ccol_ref, crow_ref, o_ref, qs_ref, cq_ref, s_ref, m_ref, l_ref, acc_ref = refs
    else:
        q_ref, k_ref, v_ref, o_ref, s_ref, m_ref, l_ref, acc_ref = refs
    qi = pl.program_id(1)
    n_pairs = n_heads // 2
    half = LANES // 2
    lane = lax.broadcasted_iota(I32, (blk, LANES), 1)

    m_ref[...] = jnp.full_like(m_ref, NEG_INF)
    if fox:
        zero = jnp.zeros((), BF16)
        for p in range(n_pairs):
            q2 = q_ref[0, :, p * LANES:(p + 1) * LANES]
            qs_ref[p, 0:blk, :] = jnp.where(lane < half, q2, zero)
            qs_ref[p, blk:2 * blk, :] = jnp.where(lane >= half, q2, zero)
            for hh in range(2):
                col = MISC_CUM + 2 * p + hh
                cq_ref[p, hh * blk:(hh + 1) * blk, :] = jnp.broadcast_to(ccol_ref[0, :, col:col + 1], (blk, LANES))

    def scores(kb, diagonal):
        ks = pl.multiple_of(kb * blk, blk)
        for p in range(n_pairs):
            if fox:
                s = _dot_nt(qs_ref[p], k_ref[0, pl.ds(ks, blk), p * LANES:(p + 1) * LANES])
                ck = jnp.concatenate([jnp.broadcast_to(crow_ref[0, kb, 2 * p + hh:2 * p + hh + 1, :], (blk, blk))
                                      for hh in range(2)], axis=0)
                s = s + (_tile_lanes(cq_ref[p], blk // LANES) - ck)
            else:
                s = jnp.concatenate(
                    [_dot_nt(q_ref[0, :, hd * LANES:(hd + 1) * LANES], k_ref[0, pl.ds(ks, blk), hd * LANES:(hd + 1) * LANES])
                     for hd in (2 * p, 2 * p + 1)], axis=0)
            if diagonal:
                r = lax.broadcasted_iota(I32, (2 * blk, blk), 0)
                r = jnp.where(r >= blk, r - blk, r)
                c = lax.broadcasted_iota(I32, (2 * blk, blk), 1)
                mask = (c <= r) if fox else ((c // CHUNK) <= (r // CHUNK))
                s = jnp.where(mask, s, NEG_INF)
            s_ref[p, kb] = s
            m_ref[p] = jnp.maximum(m_ref[p], _fold_lanes(s, jnp.maximum))

    def score_body(kb, carry):
        scores(kb, False)
        return carry

    lax.fori_loop(0, qi, score_body, 0)
    scores(qi, True)

    for p in range(n_pairs):
        m_ref[p] = jnp.broadcast_to(jnp.max(m_ref[p], axis=1, keepdims=True), (2 * blk, LANES))
    l_ref[...] = jnp.zeros_like(l_ref)
    acc_ref[...] = jnp.zeros_like(acc_ref)

    def weigh_body(kb, carry):
        ks = pl.multiple_of(kb * blk, blk)
        for p in range(n_pairs):
            pe = jnp.exp2(s_ref[p, kb] - _tile_lanes(m_ref[p], blk // LANES))
            l_ref[p] = l_ref[p] + _fold_lanes(pe, jnp.add)
            acc_ref[p] = acc_ref[p] + _dot(pe.astype(BF16), v_ref[0, pl.ds(ks, blk), p * LANES:(p + 1) * LANES])
        return carry

    lax.fori_loop(0, qi + 1, weigh_body, 0)
    for p in range(n_pairs):
        o = acc_ref[p] / jnp.sum(l_ref[p], axis=1, keepdims=True)
        o_ref[0, :, p * LANES:(p + 1) * LANES] = jnp.where(lane < half, o[0:blk], o[blk:2 * blk]).astype(o_ref.dtype)


def _prompt_attn(q, k, v, cum_col, cum_row, *, blk, n_heads, fox):
    B, S, qw = q.shape
    vw = v.shape[2]
    n_pairs = n_heads // 2
    in_specs = [pl.BlockSpec((1, blk, qw), lambda b, i: (b, i, 0)),
                pl.BlockSpec((1, S, k.shape[2]), lambda b, i: (b, 0, 0)),
                pl.BlockSpec((1, S, vw), lambda b, i: (b, 0, 0))]
    args = [q, k, v]
    stat = pltpu.VMEM((n_pairs, 2 * blk, LANES), F32)
    scratch = []
    if fox:
        in_specs += [pl.BlockSpec((1, blk, LANES), lambda b, i: (b, i, 0)),
                     pl.BlockSpec((1,) + cum_row.shape[1:], lambda b, i: (b, 0, 0, 0))]
        args += [cum_col, cum_row]
        scratch = [pltpu.VMEM((n_pairs, 2 * blk, LANES), BF16), stat]
    scratch += [pltpu.VMEM((n_pairs, S // blk, 2 * blk, blk), F32), stat, stat, stat]
    return pl.pallas_call(
        functools.partial(_prompt_attn_kernel, blk=blk, n_heads=n_heads, fox=fox),
        grid=(B, S // blk),
        in_specs=in_specs,
        out_specs=pl.BlockSpec((1, blk, vw), lambda b, i: (b, i, 0)),
        out_shape=jax.ShapeDtypeStruct((B, S, vw), BF16),
        scratch_shapes=scratch,
        compiler_params=_cparams("parallel", "parallel"),
        name="fox_attn_prompt" if fox else "mla_attn_prompt",
    )(*args)


def _lane_cumsum_kernel(x_ref, o_ref):
    o_ref[...] = _prefix_sum(x_ref[...], 1) * LOG2E


def _lane_cumsum(x):
    return pl.pallas_call(
        _lane_cumsum_kernel,
        out_shape=jax.ShapeDtypeStruct(x.shape, F32),
        compiler_params=pltpu.CompilerParams(vmem_limit_bytes=VMEM_LIMIT_BYTES),
        name="logf_cumsum_sample",
    )(x)


def _head_rows(x, n_heads, head_lanes):
    t, w = x.shape
    xt = jnp.concatenate([x] * n_heads, axis=0)
    row = lax.broadcasted_iota(I32, xt.shape, 0)
    lane = lax.broadcasted_iota(I32, xt.shape, 1)
    return jnp.where(lane // head_lanes == row // t, xt, jnp.zeros((), x.dtype))


def _fold_head_rows(o, n_heads, t, head_lanes):
    row = lax.broadcasted_iota(I32, o.shape, 0)
    lane = lax.broadcasted_iota(I32, o.shape, 1)
    o = jnp.where(lane // head_lanes == row // t, o, 0.0)
    out = o[0:t]
    for hd in range(1, n_heads):
        out = out + o[hd * t:(hd + 1) * t]
    return out


def _repeat_rows(x, t):
    return jnp.concatenate([jnp.broadcast_to(x[r:r + 1], (t, x.shape[1])) for r in range(x.shape[0])], axis=0)


def _softmax_step(s, v, m_ref, l_ref, acc_ref):
    m = m_ref[...]
    m_new = jnp.maximum(m, jnp.max(s, axis=1, keepdims=True))
    alpha = jnp.exp2(m - m_new)
    pe = jnp.exp2(s - m_new)
    l_ref[...] = alpha * l_ref[...] + jnp.sum(pe, axis=1, keepdims=True)
    acc_ref[...] = alpha * acc_ref[...] + _dot(pe.astype(BF16), v)
    m_ref[...] = m_new


def _fox_sample_kernel(q_ref, k_ref, v_ref, kn_ref, vn_ref, cq_ref, ck_ref, ckn_ref, o_ref,
                       qall_ref, qbd_ref, m_ref, l_ref, acc_ref, *, n_heads, t_new):
    c = pl.program_id(1)
    hd_w = q_ref.shape[2] // n_heads
    heads = [slice(hd * hd_w, (hd + 1) * hd_w) for hd in range(n_heads)]
    groups = [slice(hd * t_new, (hd + 1) * t_new) for hd in range(n_heads)]

    @pl.when(c == 0)
    def _():
        q = q_ref[0]
        for hd in range(n_heads):
            qall_ref[groups[hd], :] = q[:, heads[hd]]
        qbd_ref[...] = _head_rows(q, n_heads, hd_w)
        m_ref[...] = jnp.full_like(m_ref, NEG_INF)
        l_ref[...] = jnp.zeros_like(l_ref)
        acc_ref[...] = jnp.zeros_like(acc_ref)

    def step(s, pv_fn):
        m = m_ref[...]
        m_new = jnp.maximum(m, jnp.max(s, axis=1, keepdims=True))
        alpha = jnp.exp2(m - m_new)
        pe = jnp.exp2(s - m_new)
        l_ref[...] = alpha * l_ref[...] + jnp.sum(pe, axis=1, keepdims=True)
        acc_ref[...] = alpha * acc_ref[...] + pv_fn(pe.astype(BF16))
        m_ref[...] = m_new

    def flat_rows(ref):
        pos, nh, w = ref.shape[1:]
        return ref[0].reshape(pos * nh, w).astype(BF16)

    cq = cq_ref[0]
    s = _dot_nt(qall_ref[...], flat_rows(k_ref)) + cq - ck_ref[0]
    row = lax.broadcasted_iota(I32, s.shape, 0)
    col = lax.broadcasted_iota(I32, s.shape, 1)
    s = jnp.where(col % n_heads == row // t_new, s, NEG_INF)
    step(s, lambda pe: _dot(pe, flat_rows(v_ref)))

    @pl.when(c == pl.num_programs(1) - 1)
    def _():
        sn = _dot_nt(qbd_ref[...], kn_ref[0]) + cq - _repeat_rows(ckn_ref[0][:, 0:t_new], t_new)
        rown = lax.broadcasted_iota(I32, sn.shape, 0)
        coln = lax.broadcasted_iota(I32, sn.shape, 1)
        sn = jnp.where(coln <= rown % t_new, sn, NEG_INF)

        def new_values(pe):
            full = _dot(pe, vn_ref[0])
            return jnp.concatenate([full[groups[hd], heads[hd]] for hd in range(n_heads)], axis=0)

        step(sn, new_values)
        o = acc_ref[...] / l_ref[...]
        o_ref[0] = jnp.concatenate([o[groups[hd]] for hd in range(n_heads)], axis=1).astype(o_ref.dtype)


def _fox_sample_attn(q, k_cache, v_cache, k_new, v_new, cum_q, cum_flat, cum_all, *, n_heads, chunk):
    B, t_new, w = q.shape
    hd_w = w // n_heads
    P = k_cache.shape[1]
    rows = n_heads * t_new
    cache = pl.BlockSpec((1, chunk, n_heads, hd_w), lambda b, c: (b, c, 0, 0))
    new = pl.BlockSpec((1, t_new, w), lambda b, c: (b, 0, 0))
    return pl.pallas_call(
        functools.partial(_fox_sample_kernel, n_heads=n_heads, t_new=t_new),
        grid=(B, P // chunk),
        in_specs=[new, cache, cache, new, new,
                  pl.BlockSpec((1, rows, 1), lambda b, c: (b, 0, 0)),
                  pl.BlockSpec((1, 1, chunk * n_heads), lambda b, c: (b, 0, c)),
                  pl.BlockSpec((1, n_heads, LANES), lambda b, c: (b, 0, P // LANES))],
        out_specs=new,
        out_shape=jax.ShapeDtypeStruct((B, t_new, w), BF16),
        scratch_shapes=[pltpu.VMEM((rows, hd_w), BF16), pltpu.VMEM((rows, w), BF16), pltpu.VMEM((rows, 1), F32),
                        pltpu.VMEM((rows, 1), F32), pltpu.VMEM((rows, hd_w), F32)],
        compiler_params=_cparams("parallel", "arbitrary"),
        name="fox_attn_sample",
    )(q, k_cache, v_cache, k_new, v_new, cum_q, cum_flat, cum_all)


def _mla_sample_kernel(q_ref, wuk_ref, ckv_ref, kr_ref, ckvn_ref, miscn_ref, wuv_ref, o_ref,
                       ql_ref, qr_ref, m_ref, l_ref, acc_ref, *, n_heads, t_new, nope, rope, past):
    c = pl.program_id(1)

    @pl.when(c == 0)
    def _():
        for hd in range(n_heads):
            qh = q_ref[0, :, hd * LANES:(hd + 1) * LANES]
            rows = slice(hd * t_new, (hd + 1) * t_new)
            ql_ref[rows, :] = _dot(qh, wuk_ref[hd]).astype(BF16)
            qr_ref[rows, :] = qh[:, nope:nope + rope]
        m_ref[...] = jnp.full_like(m_ref, NEG_INF)
        l_ref[...] = jnp.zeros_like(l_ref)
        acc_ref[...] = jnp.zeros_like(acc_ref)

    ql = ql_ref[...]
    qr = qr_ref[...]
    ckv = ckv_ref[0].astype(BF16)
    s = _dot_nt(ql, ckv) + _dot_nt(qr, kr_ref[0].astype(BF16))
    _softmax_step(s, ckv, m_ref, l_ref, acc_ref)

    @pl.when(c == pl.num_programs(1) - 1)
    def _():
        ckvn = ckvn_ref[0].astype(BF16)
        krn = miscn_ref[0][:, MISC_ROPE:MISC_ROPE + rope].astype(BF16)
        sn = _dot_nt(ql, ckvn) + _dot_nt(qr, krn)
        row = lax.broadcasted_iota(I32, sn.shape, 0)
        col = lax.broadcasted_iota(I32, sn.shape, 1)
        sn = jnp.where((past + col) // CHUNK <= (past + row % t_new) // CHUNK, sn, NEG_INF)
        _softmax_step(sn, ckvn, m_ref, l_ref, acc_ref)
        lat = (acc_ref[...] / l_ref[...]).astype(BF16)
        o = _dot(lat, wuv_ref[...])
        o_ref[0] = _fold_head_rows(o, n_heads, t_new, o.shape[1] // n_heads).astype(o_ref.dtype)


def _mla_sample_attn(qm, wukp, ckv_cache, kr_cache, ckv_new, misc_new, wuv, *, n_heads, chunk, nope, rope):
    B, t_new, qw = qm.shape
    P, lat = ckv_cache.shape[1:]
    rows = n_heads * t_new
    vw = wuv.shape[1]
    return pl.pallas_call(
        functools.partial(_mla_sample_kernel, n_heads=n_heads, t_new=t_new, nope=nope, rope=rope, past=P),
        grid=(B, P // chunk),
        in_specs=[pl.BlockSpec((1, t_new, qw), lambda b, c: (b, 0, 0)),
                  _const_spec(wukp.shape),
                  pl.BlockSpec((1, chunk, lat), lambda b, c: (b, c, 0)),
                  pl.BlockSpec((1, chunk, rope), lambda b, c: (b, c, 0)),
                  pl.BlockSpec((1, t_new, lat), lambda b, c: (b, 0, 0)),
                  pl.BlockSpec((1, t_new, LANES), lambda b, c: (b, 0, 0)),
                  _const_spec(wuv.shape)],
        out_specs=pl.BlockSpec((1, t_new, vw), lambda b, c: (b, 0, 0)),
        out_shape=jax.ShapeDtypeStruct((B, t_new, vw), BF16),
        scratch_shapes=[pltpu.VMEM((rows, lat), BF16), pltpu.VMEM((rows, rope), BF16),
                        pltpu.VMEM((rows, 1), F32), pltpu.VMEM((rows, 1), F32), pltpu.VMEM((rows, lat), F32)],
        compiler_params=_cparams("parallel", "arbitrary"),
        name="mla_attn_sample",
    )(qm, wukp, ckv_cache, kr_cache, ckv_new, misc_new, wuv)


def _mix_part(x, fo, mo, ga, gb, wof_ref, wom_ref, wout_ref, gx_ref, wxq_ref, x_scale):
    a = _dot(fo, wof_ref[...])
    b = _dot(mo, wom_ref[...])
    merged = (ga.astype(F32) * a + gb.astype(F32) * b).astype(BF16)
    x1 = x + _dot(merged, wout_ref[...])
    xq = (_dot(_rms(x1, gx_ref[...]).astype(BF16), wxq_ref[...]) * x_scale).astype(BF16)
    return x1, xq


def _cross_part(xq, mk, mv, x_heads):
    hd_w = xq.shape[1] // x_heads
    outs = []
    for hd in range(x_heads):
        sl = slice(hd * hd_w, (hd + 1) * hd_w)
        s = _dot_nt(xq[:, sl], mk[:, sl])
        pe = jnp.exp2(s - jnp.max(s, axis=1, keepdims=True))
        o = _dot(pe.astype(BF16), mv[:, sl]) / jnp.sum(pe, axis=1, keepdims=True)
        outs.append(o.astype(BF16))
    return jnp.concatenate(outs, axis=1)


def _route_part(x1, ca, wxo_ref, gffn_ref, wrt_ref, br_ref, carry_ref,
                x2_ref, hf_ref, idx_ref, rank_ref, gate_ref, cnt_ref):
    tm = x1.shape[0]
    n_exp = wrt_ref.shape[1]
    x2 = x1 + _dot(ca, wxo_ref[...])
    x2_ref[...] = x2
    hf = _rms(x2, gffn_ref[...])
    hf_ref[...] = hf
    hf_hi = hf.astype(BF16)
    hf_lo = (hf - hf_hi.astype(F32)).astype(BF16)
    logits = (_dot_nt(wrt_ref[0], hf_hi) + _dot_nt(wrt_ref[0], hf_lo) + _dot_nt(wrt_ref[1], hf_hi)
              + br_ref[...])
    erow = lax.broadcasted_iota(I32, (n_exp, tm), 0).astype(F32)
    picked = jnp.zeros((n_exp, tm), F32)
    vals, onehots, idxs = [], [], []
    for k in range(TOP_K):
        mx = jnp.max(logits, axis=0, keepdims=True)
        idx = jnp.min(jnp.where(logits == mx, erow, float(n_exp)), axis=0, keepdims=True)
        sel = erow == idx
        vals.append(mx)
        onehots.append(sel)
        idxs.append(idx)
        picked = picked + sel.astype(F32)
        logits = jnp.where(sel, -jnp.inf, logits)
    ex = [jnp.exp(v - vals[0]) for v in vals]
    den = ex[0] + ex[1] + ex[2] + ex[3]
    for k in range(TOP_K):
        gate_ref[k:k + 1, :] = ex[k] / den
    r = lax.broadcasted_iota(I32, (tm, tm), 0)
    cidx = lax.broadcasted_iota(I32, (tm, tm), 1)
    upper = jnp.where(r < cidx, 1.0, 0.0).astype(BF16)
    before = _dot(picked.astype(BF16), upper) + carry_ref[...]
    for k in range(TOP_K):
        rank = jnp.sum(jnp.where(onehots[k], before, 0.0), axis=0, keepdims=True)
        idx_ref[k:k + 1, :] = idxs[k].astype(I32)
        rank_ref[k:k + 1, :] = rank.astype(I32)
    carry_ref[...] = carry_ref[...] + jnp.sum(picked, axis=1, keepdims=True)
    cnt_ref[...] = jnp.broadcast_to(carry_ref[...], cnt_ref.shape).astype(I32)


def _post_attn_prompt_kernel(x_ref, fo_ref, mo_ref, ga_ref, gb_ref, mk_ref, mv_ref,
                             wof_ref, wom_ref, wout_ref, gx_ref, wxq_ref, wxo_ref, gffn_ref, wrt_ref, br_ref,
                             x2_ref, hf_ref, idx_ref, rank_ref, gate_ref, cnt_ref, carry_ref,
                             *, x_heads, x_scale):
    @pl.when(pl.program_id(0) == 0)
    def _():
        carry_ref[...] = jnp.zeros_like(carry_ref)

    x1, xq = _mix_part(x_ref[...], fo_ref[...], mo_ref[...], ga_ref[...], gb_ref[...],
                       wof_ref, wom_ref, wout_ref, gx_ref, wxq_ref, x_scale)
    ca = _cross_part(xq, mk_ref[0].astype(BF16), mv_ref[0].astype(BF16), x_heads)
    _route_part(x1, ca, wxo_ref, gffn_ref, wrt_ref, br_ref, carry_ref,
                x2_ref, hf_ref, idx_ref, rank_ref, gate_ref, cnt_ref)


def _route_out(T, D, n_exp, tm):
    tok = lambda n: pl.BlockSpec((tm, n), lambda i: (i, 0))
    col = pl.BlockSpec((TOP_K, tm), lambda i: (0, i))
    sds = jax.ShapeDtypeStruct
    shapes = [sds((T, D), F32), sds((T, D), F32), sds((TOP_K, T), I32), sds((TOP_K, T), I32),
              sds((TOP_K, T), F32), sds((n_exp, LANES), I32)]
    specs = [tok(D), tok(D), col, col, col, pl.BlockSpec((n_exp, LANES), lambda i: (0, 0))]
    return shapes, specs


def _post_attn_prompt(x, fo, mo, ga, gb, mk, mv, w, *, tm, seq):
    T, D = x.shape
    n_exp = w['wrt'].shape[1]
    tiles_per_seq = seq // tm
    tok = lambda n: pl.BlockSpec((tm, n), lambda i: (i, 0))
    mem = pl.BlockSpec((1,) + mk.shape[1:], lambda i: (i // tiles_per_seq, 0, 0))
    weights = [w['wof'], w['wom'], w['wout'], w['gx'], w['wxq'], w['wxo'], w['gffn'], w['wrt'], w['br']]
    shapes, specs = _route_out(T, D, n_exp, tm)
    return pl.pallas_call(
        functools.partial(_post_attn_prompt_kernel, x_heads=w['x_heads'], x_scale=w['x_scale']),
        grid=(T // tm,),
        in_specs=[tok(D), tok(fo.shape[1]), tok(mo.shape[1]), tok(D), tok(D), mem, mem]
        + [_const_spec(a.shape) for a in weights],
        out_specs=specs,
        out_shape=shapes,
        scratch_shapes=[pltpu.VMEM((n_exp, 1), F32)],
        compiler_params=_cparams("arbitrary"),
        name="post_attn_prompt",
    )(x, fo, mo, ga, gb, mk, mv, *weights)


def _mix_sample_kernel(x_ref, fo_ref, mo_ref, ga_ref, gb_ref, wof_ref, wom_ref, wout_ref, gx_ref, wxq_ref,
                       x1_ref, xq_ref, *, x_scale):
    x1, xq = _mix_part(x_ref[...], fo_ref[...], mo_ref[...], ga_ref[...], gb_ref[...],
                       wof_ref, wom_ref, wout_ref, gx_ref, wxq_ref, x_scale)
    x1_ref[...] = x1
    xq_ref[...] = xq


def _cross_sample_kernel(xq_ref, mk_ref, mv_ref, o_ref, *, x_heads):
    o_ref[0] = _cross_part(xq_ref[0], mk_ref[0].astype(BF16), mv_ref[0].astype(BF16), x_heads)


def _route_sample_kernel(x1_ref, ca_ref, wxo_ref, gffn_ref, wrt_ref, br_ref,
                         x2_ref, hf_ref, idx_ref, rank_ref, gate_ref, cnt_ref, carry_ref):
    carry_ref[...] = jnp.zeros_like(carry_ref)
    _route_part(x1_ref[...], ca_ref[...], wxo_ref, gffn_ref, wrt_ref, br_ref, carry_ref,
                x2_ref, hf_ref, idx_ref, rank_ref, gate_ref, cnt_ref)


def _post_attn_sample(x, fo, mo, ga, gb, mk, mv, w, *, t_new):
    T, D = x.shape
    B = T // t_new
    n_exp = w['wrt'].shape[1]
    xw = w['wxq'].shape[1]
    params = pltpu.CompilerParams(vmem_limit_bytes=VMEM_LIMIT_BYTES)
    x1, xq = pl.pallas_call(
        functools.partial(_mix_sample_kernel, x_scale=w['x_scale']),
        out_shape=[jax.ShapeDtypeStruct((T, D), F32), jax.ShapeDtypeStruct((T, xw), BF16)],
        compiler_params=params,
        name="mix_sample",
    )(x, fo, mo, ga, gb, w['wof'], w['wom'], w['wout'], w['gx'], w['wxq'])
    ca = pl.pallas_call(
        functools.partial(_cross_sample_kernel, x_heads=w['x_heads']),
        grid=(B,),
        in_specs=[pl.BlockSpec((1, t_new, xw), lambda b: (b, 0, 0)),
                  pl.BlockSpec((1,) + mk.shape[1:], lambda b: (b, 0, 0)),
                  pl.BlockSpec((1,) + mv.shape[1:], lambda b: (b, 0, 0))],
        out_specs=pl.BlockSpec((1, t_new, xw), lambda b: (b, 0, 0)),
        out_shape=jax.ShapeDtypeStruct((B, t_new, xw), BF16),
        compiler_params=_cparams("parallel"),
        name="cross_sample",
    )(xq.reshape(B, t_new, xw), mk, mv)
    shapes, _ = _route_out(T, D, n_exp, T)
    return pl.pallas_call(
        _route_sample_kernel,
        out_shape=shapes,
        scratch_shapes=[pltpu.VMEM((n_exp, 1), F32)],
        compiler_params=params,
        name="route_sample",
    )(x1, ca.reshape(T, xw), w['wxo'], w['gffn'], w['wrt'], w['br'])


def _mem_kv_kernel(m_ref, g_ref, wk_ref, wv_ref, k_ref, v_ref):
    m = _rms(m_ref[...], g_ref[...]).astype(BF16)
    k_ref[...] = _dot(m, wk_ref[...])
    v_ref[...] = _dot(m, wv_ref[...])


def _mem_kv(mem, w, *, tm):
    T, D = mem.shape
    xw = w['wxk'].shape[1]
    tok = lambda n: pl.BlockSpec((tm, n), lambda i: (i, 0))
    return pl.pallas_call(
        _mem_kv_kernel,
        grid=(T // tm,),
        in_specs=[tok(D), _const_spec(w['gmem'].shape), _const_spec(w['wxk'].shape), _const_spec(w['wxv'].shape)],
        out_specs=[tok(xw), tok(xw)],
        out_shape=[jax.ShapeDtypeStruct((T, xw), F32)] * 2,
        compiler_params=_cparams("parallel"),
        name="mem_kv",
    )(mem, w['gmem'], w['wxk'], w['wxv'])


def _dest_kernel(start_ref, idx_ref, rank_ref, dest_ref):
    idx = idx_ref[...]
    dest = rank_ref[...]
    for e in range(start_ref.shape[0]):
        dest = dest + jnp.where(idx == e, start_ref[e], 0)
    dest_ref[...] = dest


def _dest_rows(pad_start, idx_t, rank_t):
    whole = pl.BlockSpec(idx_t.shape, lambda i, s: (0, 0))
    return pl.pallas_call(
        _dest_kernel,
        grid_spec=pltpu.PrefetchScalarGridSpec(num_scalar_prefetch=1, grid=(1,), in_specs=[whole, whole],
                                               out_specs=whole),
        out_shape=jax.ShapeDtypeStruct(idx_t.shape, I32),
        name="moe_dest",
    )(pad_start, idx_t, rank_t)


def _row_copy(src_ref, src_row, dst_ref, dst_row, sem):
    return pltpu.make_async_copy(src_ref.at[pl.ds(src_row, 1), :], dst_ref.at[pl.ds(dst_row, 1), :], sem)


def _zero_unassigned_rows(cnt_ref, start_ref, nu_ref, xs_ref, zero_ref, sem, bm):
    zero_ref[...] = jnp.zeros_like(zero_ref)
    half = zero_ref.shape[0]
    n_blocks = xs_ref.shape[0] // bm

    def half_block(first_row):
        return pltpu.make_async_copy(zero_ref, xs_ref.at[pl.ds(pl.multiple_of(first_row, half), half)], sem)

    def zero_block(first_row):
        half_block(first_row).start()
        half_block(first_row + half).start()

    def expert_last_block(e, n):
        cnt = cnt_ref[e]
        partial = (cnt & (bm - 1)) != 0

        @pl.when(partial)
        def _():
            zero_block(start_ref[e] + (cnt & -bm))

        return n + partial.astype(I32)

    n_zeroed = lax.fori_loop(0, cnt_ref.shape[0], expert_last_block, 0)

    def tail_block(j, _):
        zero_block(j * bm)
        return 0

    lax.fori_loop(nu_ref[0], n_blocks, tail_block, 0)

    def drain(i, _):
        half_block(0).wait()
        half_block(0).wait()
        return 0

    lax.fori_loop(0, n_zeroed + n_blocks - nu_ref[0], drain, 0)


def _dispatch_kernel(cnt_ref, start_ref, nu_ref, dest_ref, h_ref, xs_ref, zero_ref, sem, zero_sem, *, bm):
    tm = h_ref.shape[0]

    @pl.when(pl.program_id(0) == 0)
    def _():
        _zero_unassigned_rows(cnt_ref, start_ref, nu_ref, xs_ref, zero_ref, zero_sem, bm)

    def issue(t, _):
        for k in range(TOP_K):
            _row_copy(h_ref, t, xs_ref, dest_ref[k, t], sem).start()
        return 0

    lax.fori_loop(0, tm, issue, 0, unroll=ROW_DMA_UNROLL)

    def drain(t, _):
        for k in range(TOP_K):
            _row_copy(h_ref, 0, xs_ref, 0, sem).wait()
        return 0

    lax.fori_loop(0, tm, drain, 0, unroll=2 * ROW_DMA_UNROLL)


def _dispatch(counts, pad_start, n_used, dest_t, h, n_rows, *, tm, bm):
    T, D = h.shape
    return pl.pallas_call(
        functools.partial(_dispatch_kernel, bm=bm),
        grid_spec=pltpu.PrefetchScalarGridSpec(
            num_scalar_prefetch=3, grid=(T // tm,),
            in_specs=[pl.BlockSpec((TOP_K, tm), lambda i, c, s, n: (0, i), memory_space=pltpu.SMEM),
                      pl.BlockSpec((tm, D), lambda i, c, s, n: (i, 0))],
            out_specs=pl.BlockSpec(memory_space=pl.ANY),
            scratch_shapes=[pltpu.VMEM((bm // 2, D), h.dtype), pltpu.SemaphoreType.DMA,
                            pltpu.SemaphoreType.DMA]),
        out_shape=jax.ShapeDtypeStruct((n_rows, D), h.dtype),
        compiler_params=_cparams("arbitrary"),
        name="moe_dispatch",
    )(counts, pad_start, n_used, dest_t, h)


def _expert_kernel(be_ref, nu_ref, x_ref, wgu_ref, bgu_ref, wd_ref, bd_ref, y_ref, *, ff):
    used = pl.program_id(0) < nu_ref[0]

    @pl.when(jnp.logical_not(used))
    def _():
        y_ref[...] = jnp.zeros_like(y_ref)

    @pl.when(used)
    def _():
        gu = _dot(x_ref[...].astype(BF16), wgu_ref[0]) + bgu_ref[0]
        gate = jnp.minimum(gu[:, :ff], SWIGLU_LIMIT)
        up = jnp.clip(gu[:, ff:], -SWIGLU_LIMIT, SWIGLU_LIMIT)
        act = (up + 1.0) * (gate * _sigmoid(gate * SWIGLU_ALPHA))
        y_ref[...] = _dot(act.astype(BF16), wd_ref[0]) + bd_ref[0]


def _expert_ffn(block_e, n_used, xs, w, *, bm):
    D = xs.shape[1]
    ff = w['wd'].shape[1]
    rows = pl.BlockSpec((bm, D), lambda i, be, nu: (i, 0))
    exp = lambda shape: pl.BlockSpec((1,) + shape, lambda i, be, nu: (be[i], 0, 0))
    return pl.pallas_call(
        functools.partial(_expert_kernel, ff=ff),
        grid_spec=pltpu.PrefetchScalarGridSpec(
            num_scalar_prefetch=2, grid=(xs.shape[0] // bm,),
            in_specs=[rows, exp((D, 2 * ff)), exp((1, 2 * ff)), exp((ff, D)), exp((1, D))],
            out_specs=rows),
        out_shape=jax.ShapeDtypeStruct(xs.shape, F32),
        compiler_params=_cparams("arbitrary"),
        name="moe_expert_ffn",
    )(block_e, n_used, xs, w['wgu'], w['bgu'], w['wd'], w['bd'])


def _combine_kernel(dest_ref, dest_next_ref, gate_ref, x2_ref, gfin_ref, yb_ref, y_ref, buf_ref, sem):
    i = pl.program_id(0)
    tm = x2_ref.shape[0]
    slot = i % 2

    def gather(d_ref, s):
        def issue(t, _):
            for k in range(TOP_K):
                _row_copy(yb_ref, d_ref[k, t], buf_ref.at[s, k], t, sem.at[s]).start()
            return 0

        lax.fori_loop(0, tm, issue, 0, unroll=ROW_DMA_UNROLL)

    @pl.when(i == 0)
    def _():
        gather(dest_ref, 0)

    @pl.when(i + 1 < pl.num_programs(0))
    def _():
        gather(dest_next_ref, 1 - slot)

    def drain(t, _):
        for k in range(TOP_K):
            _row_copy(yb_ref, 0, buf_ref.at[slot, k], 0, sem.at[slot]).wait()
        return 0

    lax.fori_loop(0, tm, drain, 0, unroll=2 * ROW_DMA_UNROLL)
    gate = gate_ref[...]
    y = x2_ref[...]
    for k in range(TOP_K):
        y = y + gate[:, k:k + 1] * buf_ref[slot, k]
    y_ref[...] = _rms(y, gfin_ref[...])


def _combine(dest_t, gate_tok, x2, gfin, yb, *, tm):
    T, D = x2.shape
    last = T // tm - 1
    return pl.pallas_call(
        _combine_kernel,
        grid=(T // tm,),
        in_specs=[pl.BlockSpec((TOP_K, tm), lambda i: (0, i), memory_space=pltpu.SMEM),
                  pl.BlockSpec((TOP_K, tm), lambda i: (0, jnp.minimum(i + 1, last)), memory_space=pltpu.SMEM),
                  pl.BlockSpec((tm, TOP_K), lambda i: (i, 0)),
                  pl.BlockSpec((tm, D), lambda i: (i, 0)),
                  _const_spec(gfin.shape),
                  pl.BlockSpec(memory_space=pl.ANY)],
        out_specs=pl.BlockSpec((tm, D), lambda i: (i, 0)),
        out_shape=jax.ShapeDtypeStruct((T, D), F32),
        scratch_shapes=[pltpu.VMEM((2, TOP_K, tm, D), F32), pltpu.SemaphoreType.DMA((2,))],
        compiler_params=_cparams("arbitrary"),
        name="moe_combine",
    )(dest_t, dest_t, gate_tok, x2, gfin, yb)


def _moe_and_final_norm(x2, hf, idx_t, rank_t, gate_t, counts, w, gfin, *, bm, tm):
    T = x2.shape[0]
    n_exp = counts.shape[0]
    n_blocks = -(-(T * TOP_K + n_exp * (bm - 1)) // bm)
    padded = (counts + bm - 1) // bm * bm
    pad_end = jnp.cumsum(padded)
    pad_start = (pad_end - padded).astype(I32)
    n_used = (pad_end[n_exp - 1:] // bm).astype(I32)
    first_row = jnp.arange(n_blocks, dtype=I32) * bm
    block_e = jnp.minimum(jnp.sum(first_row[:, None] >= pad_end[None, :], axis=1), n_exp - 1).astype(I32)
    dest_t = _dest_rows(pad_start, idx_t, rank_t)
    xs = _dispatch(counts.astype(I32), pad_start, n_used, dest_t, hf, n_blocks * bm, tm=tm, bm=bm)
    yb = _expert_ffn(block_e, n_used, xs, w, bm=bm)
    return _combine(dest_t, gate_t.T, x2, gfin, yb, tm=tm)


def _rot_cols(wc):
    half = wc.shape[1] // 2
    return jnp.concatenate([-wc[:, half:], wc[:, :half]], axis=1)


def _prep_layer(p, dims):
    fw, nh, nope, rope, lat, vdim, x_heads, x_hd = dims
    D = p['w_in'].shape[0]
    w_in = p['w_in']
    o = 0
    wfox = w_in[:, o:o + 3 * fw]; o += 3 * fw
    wfl = w_in[:, o:o + nh]; o += nh
    qlora = p['g_q'].shape[0]
    wqc = w_in[:, o:o + qlora]; o += qlora
    wkvc = w_in[:, o:o + lat]; o += lat
    wkr = w_in[:, o:o + rope]; o += rope
    wga = w_in[:, o:o + D]; o += D
    wgb = w_in[:, o:o + D]

    def misc_cols(parts):
        out = jnp.zeros((D, LANES), F32)
        for off, cols in parts:
            out = out.at[:, off:off + cols.shape[1]].set(cols)
        return out

    wma = misc_cols([(MISC_LOGF, wfl), (MISC_CUM, wfl), (MISC_ROPE, wkr)])
    wmb = misc_cols([(MISC_ROPE, _rot_cols(wkr))])
    bf = jnp.zeros((1, LANES), F32).at[0, MISC_LOGF:MISC_LOGF + nh].set(p['b_f']).at[0, MISC_CUM:MISC_CUM + nh].set(p['b_f'])

    wuq = p['w_uq'].reshape(qlora, nh, nope + rope)
    wqa = jnp.zeros((qlora, nh, LANES), F32).at[:, :, :nope + rope].set(wuq)
    rot = jnp.concatenate([-wuq[:, :, nope + rope // 2:], wuq[:, :, nope:nope + rope // 2]], axis=2)
    wqb = jnp.zeros((qlora, nh, LANES), F32).at[:, :, nope:nope + rope].set(rot)
    wk = jnp.zeros((lat + LANES, nh, LANES), F32).at[:lat, :, :nope].set(p['w_uk'])
    place = jnp.zeros((LANES, nh, LANES), F32)
    j = jnp.arange(rope)
    place = place.at[MISC_ROPE + j, :, nope + j].set(1.0)
    wk = wk.at[lat:, :, :].set(place)
    wukp = jnp.zeros((nh, LANES, lat), F32).at[:, :nope, :].set(jnp.transpose(p['w_uk'], (1, 2, 0)))

    b = lambda a: a.astype(BF16)
    row = lambda a: a.reshape(1, -1).astype(F32)
    n_exp = p['w_router'].shape[1]
    wrt = p['w_router'].T.astype(F32)
    wrt_hi = b(wrt)
    wrt = jnp.stack([wrt_hi, b(wrt - wrt_hi.astype(F32))])
    return {
        'fox_width': fw, 'n_heads': nh, 'fox_scale': float(fw // nh) ** -0.5 * LOG2E,
        'x_heads': x_heads, 'x_scale': float(x_hd) ** -0.5 * LOG2E,
        'gmix': row(p['g_mix']), 'wfox': b(wfox), 'wma': b(wma), 'wmb': b(wmb), 'wqc': b(wqc), 'wkvc': b(wkvc),
        'wga': b(wga), 'wgb': b(wgb), 'bf': bf, 'gq': row(p['g_q']),
        'wqa': b(wqa.reshape(qlora, nh * LANES)), 'wqb': b(wqb.reshape(qlora, nh * LANES)),
        'gkv': row(p['g_kv']), 'wk': b(wk.reshape(lat + LANES, nh * LANES)),
        'wuv': b(p['w_uv'].reshape(lat, nh * vdim)), 'wukp': b(wukp),
        'wof': b(p['w_o_fox']), 'wom': b(p['w_o_mla']), 'wout': b(p['w_out']), 'gx': row(p['g_x']),
        'wxq': b(p['w_xq']), 'wxo': b(p['w_xo']), 'gffn': row(p['g_ffn']),
        'wrt': wrt, 'br': p['b_router'].reshape(n_exp, 1).astype(F32),
        'gmem': row(p['g_mem']), 'wxk': b(p['w_xk']), 'wxv': b(p['w_xv']),
        'wgu': b(p['w_gu']), 'bgu': p['b_gu'].reshape(n_exp, 1, -1).astype(F32),
        'wd': b(p['w_down']), 'bd': p['b_down'].reshape(n_exp, 1, -1).astype(F32),
    }


def _rope_tables(pos, nope, rope, q_scale):
    half = rope // 2
    inv_freq = ROPE_BASE ** (-jnp.arange(half, dtype=F32) / half)
    ang = pos.astype(F32)[:, None] * inv_freq[None, :]
    cos = jnp.concatenate([jnp.cos(ang)] * 2, axis=1)
    sin = jnp.concatenate([jnp.sin(ang)] * 2, axis=1)
    n = pos.shape[0]
    ck = jnp.zeros((n, LANES), F32).at[:, :MISC_ROPE].set(1.0).at[:, MISC_ROPE:MISC_ROPE + rope].set(cos)
    sk = jnp.zeros((n, LANES), F32).at[:, MISC_ROPE:MISC_ROPE + rope].set(sin)
    cq = jnp.zeros((n, LANES), F32).at[:, :nope].set(1.0).at[:, nope:nope + rope].set(cos) * q_scale
    sq = jnp.zeros((n, LANES), F32).at[:, nope:nope + rope].set(sin) * q_scale
    return ck, sk, cq, sq


def _pick_tile(n, target):
    t = min(n, target)
    while n % t:
        t //= 2
    return t


def _prompt_layer(x, mem, w, dims):
    fw, nh, nope, rope, lat, vdim, x_heads, x_hd = dims
    B, S, D = x.shape
    T = B * S
    tm = _pick_tile(S, PROJ_TILE)
    tables = _rope_tables(jnp.arange(S, dtype=I32), nope, rope, float(nope + rope) ** -0.5 * LOG2E)
    fq, fk, fv, fkb, fvb, misc, ckv, qm, ga, gb, km, vm = _in_proj(
        x.reshape(T, D), w, tables, tm=tm, tiles_per_seq=S // tm, prompt=True)

    blk = _pick_tile(S, ATTN_BLOCK)
    cum = misc[:, MISC_CUM:MISC_CUM + nh].reshape(B, S // blk, blk, nh)
    cum_row = jnp.swapaxes(cum, 2, 3)
    r3 = lambda a: a.reshape(B, S, a.shape[1])
    fox_o = _prompt_attn(r3(fq), r3(fkb), r3(fvb), r3(misc), cum_row, blk=blk, n_heads=nh, fox=True)
    mla_o = _prompt_attn(r3(qm), r3(km), r3(vm), None, None, blk=blk, n_heads=nh, fox=False)

    n_mem = mem.shape[1]
    mk, mv = _mem_kv(mem.reshape(B * n_mem, D), w, tm=_pick_tile(B * n_mem, 512))
    xw = mk.shape[1]
    routed = _post_attn_prompt(
        x.reshape(T, D), fox_o.reshape(T, fw), mla_o.reshape(T, nh * vdim), ga, gb,
        mk.reshape(B, n_mem, xw), mv.reshape(B, n_mem, xw), w, tm=tm, seq=S)
    caches = (fk.reshape(B, S, nh, fw // nh), fv.reshape(B, S, nh, fw // nh),
              misc[:, MISC_LOGF:MISC_LOGF + nh].reshape(B, S, nh), ckv.reshape(B, S, lat),
              misc[:, MISC_ROPE:MISC_ROPE + rope].reshape(B, S, rope),
              mk.reshape(B, n_mem, x_heads, x_hd), mv.reshape(B, n_mem, x_heads, x_hd))
    return routed, caches


def _sample_layer(x, c_fk, c_fv, c_logf, c_ckv, c_kr, c_mk, c_mv, w, dims):
    fw, nh, nope, rope, lat, vdim, x_heads, x_hd = dims
    B, t_new, D = x.shape
    P = c_fk.shape[1]
    T = B * t_new
    pos = P + jnp.tile(jnp.arange(t_new, dtype=I32), B)
    tables = _rope_tables(pos, nope, rope, float(nope + rope) ** -0.5 * LOG2E)
    fq, fk, fv, fkb, fvb, misc, ckv, qm, ga, gb = _in_proj(
        x.reshape(T, D), w, tables, tm=T, tiles_per_seq=1, prompt=False)

    logf_new = misc[:, MISC_LOGF:MISC_LOGF + nh].reshape(B, t_new, nh)
    lf = jnp.concatenate([c_logf.astype(F32), logf_new, jnp.zeros((B, LANES - t_new, nh), F32)], axis=1)
    cum_all = _lane_cumsum(jnp.swapaxes(lf, 1, 2).reshape(B * nh, P + LANES)).reshape(B, nh, P + LANES)
    cum_q = cum_all[:, :, P:P + t_new].reshape(B, nh * t_new, 1)

    chunk = _pick_tile(P, CACHE_CHUNK)
    r3 = lambda a: a.reshape(B, t_new, a.shape[1])
    cum_flat = jnp.swapaxes(cum_all, 1, 2).reshape(B, 1, (P + LANES) * nh)
    fox_o = _fox_sample_attn(r3(fq), c_fk, c_fv, r3(fkb), r3(fvb), cum_q, cum_flat, cum_all, n_heads=nh, chunk=chunk)
    mla_o = _mla_sample_attn(r3(qm), w['wukp'], c_ckv, c_kr, r3(ckv), r3(misc), w['wuv'],
                             n_heads=nh, chunk=chunk, nope=nope, rope=rope)
    n_mem = c_mk.shape[1]
    routed = _post_attn_sample(
        x.reshape(T, D), fox_o.reshape(T, fw), mla_o.reshape(T, nh * vdim), ga, gb,
        c_mk.reshape(B, n_mem, x_heads * x_hd), c_mv.reshape(B, n_mem, x_heads * x_hd), w, t_new=t_new)
    caches = (fk.reshape(B, t_new, nh, fw // nh), fv.reshape(B, t_new, nh, fw // nh), logf_new,
              ckv.reshape(B, t_new, lat), misc[:, MISC_ROPE:MISC_ROPE + rope].reshape(B, t_new, rope))
    return routed, caches


def kernel(x_prompt, x_sample, mem_prompt, cache_fox_k, cache_fox_v, cache_fox_logf, cache_mla_ckv,
           cache_mla_krope, cache_mem_k, cache_mem_v, g_mix, w_in, b_f, g_q, w_uq, g_kv, w_uk, w_uv,
           w_o_fox, w_o_mla, w_out, g_x, g_mem, w_xq, w_xk, w_xv, w_xo, g_ffn, w_router, b_router,
           w_gu, b_gu, w_down, b_down, g_final):
    depth = w_in.shape[0]
    assert depth == 1, "the fused final norm assumes a single layer"
    nh, fhd = cache_fox_k.shape[3:]
    lat = cache_mla_ckv.shape[3]
    rope = cache_mla_krope.shape[3]
    nope, vdim = w_uk.shape[3], w_uv.shape[3]
    x_heads, x_hd = cache_mem_k.shape[3:]
    dims = (nh * fhd, nh, nope, rope, lat, vdim, x_heads, x_hd)
    B, S, D = x_prompt.shape
    Bs, t_new, _ = x_sample.shape
    gfin = g_final.reshape(1, D).astype(F32)

    l = 0
    p = {'g_mix': g_mix[l], 'w_in': w_in[l], 'b_f': b_f[l], 'g_q': g_q[l], 'w_uq': w_uq[l],
         'g_kv': g_kv[l], 'w_uk': w_uk[l], 'w_uv': w_uv[l], 'w_o_fox': w_o_fox[l],
         'w_o_mla': w_o_mla[l], 'w_out': w_out[l], 'g_x': g_x[l], 'g_mem': g_mem[l],
         'w_xq': w_xq[l], 'w_xk': w_xk[l], 'w_xv': w_xv[l], 'w_xo': w_xo[l], 'g_ffn': g_ffn[l],
         'w_router': w_router[l], 'b_router': b_router[l], 'w_gu': w_gu[l], 'b_gu': b_gu[l],
         'w_down': w_down[l], 'b_down': b_down[l]}
    w = _prep_layer(p, dims)

    (*routed_p, cnt_p), pc = _prompt_layer(x_prompt, mem_prompt, w, dims)
    (*routed_s, cnt_s), sc = _sample_layer(
        x_sample, cache_fox_k[l], cache_fox_v[l], cache_fox_logf[l], cache_mla_ckv[l],
        cache_mla_krope[l], cache_mem_k[l], cache_mem_v[l], w, dims)
    y_prompt = _moe_and_final_norm(*routed_p, cnt_p[:, 0], w, gfin,
                                   bm=EXPERT_ROWS_PROMPT, tm=TOKEN_TILE).reshape(B, S, D)
    y_sample = _moe_and_final_norm(*routed_s, cnt_s[:, 0], w, gfin, bm=EXPERT_ROWS_SAMPLE,
                                   tm=_pick_tile(Bs * t_new, TOKEN_TILE)).reshape(Bs, t_new, D)
    return (y_prompt, y_sample) + tuple(a[None] for a in pc) + tuple(a[None] for a in sc)
```

```python
import functools
import math

import jax
import jax.numpy as jnp
from jax import lax
from jax.experimental import pallas as pl
from jax.experimental.pallas import tpu as pltpu

F32 = jnp.float32
BF16 = jnp.bfloat16
I32 = jnp.int32

CHUNK = 64
EPS = 1e-6
NEG_INF = -1e30
ROPE_BASE = 10000.0
TOP_K = 4
SWIGLU_LIMIT = 7.0
SWIGLU_ALPHA = 1.702
LOG2E = math.log2(math.e)

LANES = 128
SUBLANES = 8
VMEM_LIMIT_BYTES = 56 * 1024 * 1024

MISC_LOGF = 0
MISC_CUM = 8
MISC_ROPE = 16

PROJ_TILE = 512
TOKEN_TILE = 256
ATTN_BLOCK = 256
CACHE_CHUNK = 1024
EXPERT_ROWS_PROMPT = 512
EXPERT_ROWS_SAMPLE = 128
ROW_DMA_UNROLL = 4


def _cparams(*sem):
    return pltpu.CompilerParams(dimension_semantics=sem, vmem_limit_bytes=VMEM_LIMIT_BYTES)


def _const_spec(shape):
    nd = len(shape)
    return pl.BlockSpec(shape, lambda *_: (0,) * nd, pipeline_mode=pl.Buffered(1))


def _rms(x, g):
    return x * lax.rsqrt(jnp.mean(x * x, axis=-1, keepdims=True) + EPS) * g


def _sigmoid(x):
    return 1.0 / (1.0 + jnp.exp(-x))


def _log_sigmoid(x):
    return jnp.minimum(x, 0.0) - jnp.log1p(jnp.exp(-jnp.abs(x)))


def _prefix_sum(c, axis):
    n = c.shape[axis]
    pos = lax.broadcasted_iota(I32, c.shape, axis)
    s = 1
    while s < n:
        c = c + jnp.where(pos >= s, pltpu.roll(c, s, axis), 0.0)
        s *= 2
    return c


def _dot(a, b):
    return jnp.dot(a, b, preferred_element_type=F32)


def _dot_nt(a, b):
    return lax.dot_general(a, b, (((1,), (1,)), ((), ())), preferred_element_type=F32)


def _tile_lanes(t, n):
    return jnp.concatenate([t] * n, axis=1)


def _in_proj_kernel(x_ref, gmix_ref, wfox_ref, wma_ref, wmb_ref, wqc_ref, wkvc_ref, wga_ref, wgb_ref,
                    bf_ref, gq_ref, wqa_ref, wqb_ref, gkv_ref, wk_ref, wuv_ref,
                    ck_ref, sk_ref, cq_ref, sq_ref,
                    fq_ref, fk_ref, fv_ref, fkb_ref, fvb_ref, misc_ref, ckv_ref, qm_ref, ga_ref, gb_ref, *rest,
                    fox_width, n_heads, fox_scale, tiles_per_seq, prompt):
    if prompt:
        km_ref, vm_ref, carry_ref = rest
    tm = x_ref.shape[0]
    h = _rms(x_ref[...], gmix_ref[...]).astype(BF16)

    fq_ref[...] = (_dot(h, wfox_ref[:, 0:fox_width]) * fox_scale).astype(BF16)
    fk = _dot(h, wfox_ref[:, fox_width:2 * fox_width])
    fv = _dot(h, wfox_ref[:, 2 * fox_width:3 * fox_width])
    if prompt:
        fk_ref[0] = fk.T
        fv_ref[0] = fv.T
    else:
        fk_ref[...] = fk
        fv_ref[...] = fv
    fkb_ref[...] = fk.astype(BF16)
    fvb_ref[...] = fv.astype(BF16)

    lane = lax.broadcasted_iota(I32, (tm, LANES), 1)
    pre = _dot(h, wma_ref[...]) * ck_ref[...] + _dot(h, wmb_ref[...]) * sk_ref[...]
    misc = jnp.where(lane < MISC_ROPE, _log_sigmoid(pre + bf_ref[...]), pre)
    if prompt:
        i = pl.program_id(0)

        @pl.when(i % tiles_per_seq == 0)
        def _():
            carry_ref[...] = jnp.zeros_like(carry_ref)

        in_cum = (lane >= MISC_CUM) & (lane < MISC_ROPE)
        cum = _prefix_sum(jnp.where(in_cum, misc * LOG2E, 0.0), 0) + carry_ref[...]
        carry_ref[...] = cum[tm - 1:tm, :]
        misc = jnp.where(in_cum, cum, misc)
    misc_ref[...] = misc

    ckv = _rms(_dot(h, wkvc_ref[...]), gkv_ref[...])
    ckv_ref[...] = ckv

    qn = _rms(_dot(h, wqc_ref[...]), gq_ref[...]).astype(BF16)
    cq = _tile_lanes(cq_ref[...], n_heads)
    sq = _tile_lanes(sq_ref[...], n_heads)
    qm_ref[...] = (_dot(qn, wqa_ref[...]) * cq + _dot(qn, wqb_ref[...]) * sq).astype(BF16)

    if prompt:
        ckv_b = ckv.astype(BF16)
        km_ref[...] = _dot(jnp.concatenate([ckv_b, misc.astype(BF16)], axis=1), wk_ref[...]).astype(BF16)
        vm_ref[...] = _dot(ckv_b, wuv_ref[...]).astype(BF16)

    ga_ref[...] = _sigmoid(_dot(h, wga_ref[...])).astype(BF16)
    gb_ref[...] = _sigmoid(_dot(h, wgb_ref[...])).astype(BF16)


def _in_proj(x, w, tables, *, tm, tiles_per_seq, prompt):
    T, D = x.shape
    fw = w['fox_width']
    nh = w['n_heads']
    lat = w['wkvc'].shape[1]
    hw = nh * LANES
    ck, sk, cq, sq = tables
    tok = lambda n: pl.BlockSpec((tm, n), lambda i: (i, 0))
    if prompt:
        tab = pl.BlockSpec((tm, LANES), lambda i: (i % tiles_per_seq, 0))
    else:
        tab = tok(LANES)
    weights = [w['gmix'], w['wfox'], w['wma'], w['wmb'], w['wqc'], w['wkvc'], w['wga'], w['wgb'],
               w['bf'], w['gq'], w['wqa'], w['wqb'], w['gkv'], w['wk'], w['wuv']]
    sds = jax.ShapeDtypeStruct
    out_shape = [sds((T, fw), BF16), sds((T, fw), F32), sds((T, fw), F32),
                 sds((T, fw), BF16), sds((T, fw), BF16), sds((T, LANES), F32),
                 sds((T, lat), F32), sds((T, hw), BF16), sds((T, D), BF16), sds((T, D), BF16)]
    out_specs = [tok(fw), tok(fw), tok(fw), tok(fw), tok(fw), tok(LANES), tok(lat), tok(hw), tok(D), tok(D)]
    scratch = []
    if prompt:
        seq = tm * tiles_per_seq
        out_shape[1] = out_shape[2] = sds((T // seq, fw, seq), F32)
        out_specs[1] = out_specs[2] = pl.BlockSpec((1, fw, tm), lambda i: (i // tiles_per_seq, 0, i % tiles_per_seq))
        out_shape += [sds((T, hw), BF16), sds((T, w['wuv'].shape[1]), BF16)]
        out_specs += [tok(hw), tok(w['wuv'].shape[1])]
        scratch = [pltpu.VMEM((1, LANES), F32)]
    kern = functools.partial(_in_proj_kernel, fox_width=fw, n_heads=nh, fox_scale=w['fox_scale'],
                             tiles_per_seq=tiles_per_seq, prompt=prompt)
    return pl.pallas_call(
        kern,
        grid=(T // tm,),
        in_specs=[tok(D)] + [_const_spec(a.shape) for a in weights] + [tab] * 4,
        out_specs=out_specs,
        out_shape=out_shape,
        scratch_shapes=scratch,
        compiler_params=_cparams("arbitrary"),
        name="in_proj_prompt" if prompt else "in_proj_sample",
    )(x, *weights, ck, sk, cq, sq)


def _fold_lanes(x, op):
    out = x[:, 0:LANES]
    for j in range(1, x.shape[1] // LANES):
        out = op(out, x[:, j * LANES:(j + 1) * LANES])
    return out


def _prompt_attn_kernel(*refs, blk, n_heads, fox):
    if fox:
        q_ref, k_ref, v_ref, ccol_ref, crow_ref, o_ref, qs_ref, cq_ref, s_ref, m_ref, l_ref, acc_ref = refs
    else:
        q_ref, k_ref, v_ref, o_ref, s_ref, m_ref, l_ref, acc_ref = refs
    qi = pl.program_id(1)
    n_pairs = n_heads // 2
    half = LANES // 2
    lane = lax.broadcasted_iota(I32, (blk, LANES), 1)

    m_ref[...] = jnp.full_like(m_ref, NEG_INF)
    if fox:
        zero = jnp.zeros((), BF16)
        for p in range(n_pairs):
            q2 = q_ref[0, :, p * LANES:(p + 1) * LANES]
            qs_ref[p, 0:blk, :] = jnp.where(lane < half, q2, zero)
            qs_ref[p, blk:2 * blk, :] = jnp.where(lane >= half, q2, zero)
            for hh in range(2):
                col = MISC_CUM + 2 * p + hh
                cq_ref[p, hh * blk:(hh + 1) * blk, :] = jnp.broadcast_to(ccol_ref[0, :, col:col + 1], (blk, LANES))

    def scores(kb, diagonal):
        ks = pl.multiple_of(kb * blk, blk)
        for p in range(n_pairs):
            if fox:
                s = _dot_nt(qs_ref[p], k_ref[0, pl.ds(ks, blk), p * LANES:(p + 1) * LANES])
                ck = jnp.concatenate([jnp.broadcast_to(crow_ref[0, kb, 2 * p + hh:2 * p + hh + 1, :], (blk, blk))
                                      for hh in range(2)], axis=0)
                s = s + (_tile_lanes(cq_ref[p], blk // LANES) - ck)
            else:
                s = jnp.concatenate(
                    [_dot_nt(q_ref[0, :, hd * LANES:(hd + 1) * LANES], k_ref[0, pl.ds(ks, blk), hd * LANES:(hd + 1) * LANES])
                     for hd in (2 * p, 2 * p + 1)], axis=0)
            if diagonal:
                r = lax.broadcasted_iota(I32, (2 * blk, blk), 0)
                r = jnp.where(r >= blk, r - blk, r)
                c = lax.broadcasted_iota(I32, (2 * blk, blk), 1)
                mask = (c <= r) if fox else ((c // CHUNK) <= (r // CHUNK))
                s = jnp.where(mask, s, NEG_INF)
            s_ref[p, kb] = s
            m_ref[p] = jnp.maximum(m_ref[p], _fold_lanes(s, jnp.maximum))

    def score_body(kb, carry):
        scores(kb, False)
        return carry

    lax.fori_loop(0, qi, score_body, 0)
    scores(qi, True)

    for p in range(n_pairs):
        m_ref[p] = jnp.broadcast_to(jnp.max(m_ref[p], axis=1, keepdims=True), (2 * blk, LANES))
    l_ref[...] = jnp.zeros_like(l_ref)
    acc_ref[...] = jnp.zeros_like(acc_ref)

    def weigh_body(kb, carry):
        ks = pl.multiple_of(kb * blk, blk)
        for p in range(n_pairs):
            pe = jnp.exp2(s_ref[p, kb] - _tile_lanes(m_ref[p], blk // LANES))
            l_ref[p] = l_ref[p] + _fold_lanes(pe, jnp.add)
            acc_ref[p] = acc_ref[p] + _dot(pe.astype(BF16), v_ref[0, pl.ds(ks, blk), p * LANES:(p + 1) * LANES])
        return carry

    lax.fori_loop(0, qi + 1, weigh_body, 0)
    for p in range(n_pairs):
        o = acc_ref[p] / jnp.sum(l_ref[p], axis=1, keepdims=True)
        o_ref[0, :, p * LANES:(p + 1) * LANES] = jnp.where(lane < half, o[0:blk], o[blk:2 * blk]).astype(o_ref.dtype)


def _prompt_attn(q, k, v, cum_col, cum_row, *, blk, n_heads, fox):
    B, S, qw = q.shape
    vw = v.shape[2]
    n_pairs = n_heads // 2
    in_specs = [pl.BlockSpec((1, blk, qw), lambda b, i: (b, i, 0)),
                pl.BlockSpec((1, S, k.shape[2]), lambda b, i: (b, 0, 0)),
                pl.BlockSpec((1, S, vw), lambda b, i: (b, 0, 0))]
    args = [q, k, v]
    stat = pltpu.VMEM((n_pairs, 2 * blk, LANES), F32)
    scratch = []
    if fox:
        in_specs += [pl.BlockSpec((1, blk, LANES), lambda b, i: (b, i, 0)),
                     pl.BlockSpec((1,) + cum_row.shape[1:], lambda b, i: (b, 0, 0, 0))]
        args += [cum_col, cum_row]
        scratch = [pltpu.VMEM((n_pairs, 2 * blk, LANES), BF16), stat]
    scratch += [pltpu.VMEM((n_pairs, S // blk, 2 * blk, blk), F32), stat, stat, stat]
    return pl.pallas_call(
        functools.partial(_prompt_attn_kernel, blk=blk, n_heads=n_heads, fox=fox),
        grid=(B, S // blk),
        in_specs=in_specs,
        out_specs=pl.BlockSpec((1, blk, vw), lambda b, i: (b, i, 0)),
        out_shape=jax.ShapeDtypeStruct((B, S, vw), BF16),
        scratch_shapes=scratch,
        compiler_params=_cparams("parallel", "parallel"),
        name="fox_attn_prompt" if fox else "mla_attn_prompt",
    )(*args)


def _lane_cumsum_kernel(x_ref, o_ref):
    o_ref[...] = _prefix_sum(x_ref[...], 1) * LOG2E


def _lane_cumsum(x):
    return pl.pallas_call(
        _lane_cumsum_kernel,
        out_shape=jax.ShapeDtypeStruct(x.shape, F32),
        compiler_params=pltpu.CompilerParams(vmem_limit_bytes=VMEM_LIMIT_BYTES),
        name="logf_cumsum_sample",
    )(x)


def _head_rows(x, n_heads, head_lanes):
    t, w = x.shape
    xt = jnp.concatenate([x] * n_heads, axis=0)
    row = lax.broadcasted_iota(I32, xt.shape, 0)
    lane = lax.broadcasted_iota(I32, xt.shape, 1)
    return jnp.where(lane // head_lanes == row // t, xt, jnp.zeros((), x.dtype))


def _fold_head_rows(o, n_heads, t, head_lanes):
    row = lax.broadcasted_iota(I32, o.shape, 0)
    lane = lax.broadcasted_iota(I32, o.shape, 1)
    o = jnp.where(lane // head_lanes == row // t, o, 0.0)
    out = o[0:t]
    for hd in range(1, n_heads):
        out = out + o[hd * t:(hd + 1) * t]
    return out


def _repeat_rows(x, t):
    return jnp.concatenate([jnp.broadcast_to(x[r:r + 1], (t, x.shape[1])) for r in range(x.shape[0])], axis=0)


def _softmax_step(s, v, m_ref, l_ref, acc_ref):
    m = m_ref[...]
    m_new = jnp.maximum(m, jnp.max(s, axis=1, keepdims=True))
    alpha = jnp.exp2(m - m_new)
    pe = jnp.exp2(s - m_new)
    l_ref[...] = alpha * l_ref[...] + jnp.sum(pe, axis=1, keepdims=True)
    acc_ref[...] = alpha * acc_ref[...] + _dot(pe.astype(BF16), v)
    m_ref[...] = m_new


def _fox_sample_kernel(q_ref, k_ref, v_ref, kn_ref, vn_ref, cq_ref, ck_ref, ckn_ref, o_ref,
                       qbd_ref, m_ref, l_ref, acc_ref, *, n_heads, t_new):
    c = pl.program_id(1)
    hd_w = q_ref.shape[2] // n_heads
    heads = [slice(hd * hd_w, (hd + 1) * hd_w) for hd in range(n_heads)]
    groups = [slice(hd * t_new, (hd + 1) * t_new) for hd in range(n_heads)]
    q = q_ref[0]

    @pl.when(c == 0)
    def _():
        qbd_ref[...] = _head_rows(q, n_heads, hd_w)
        m_ref[...] = jnp.full_like(m_ref, NEG_INF)
        l_ref[...] = jnp.zeros_like(l_ref)
        acc_ref[...] = jnp.zeros_like(acc_ref)

    def step(s, pv_fn):
        m = m_ref[...]
        m_new = jnp.maximum(m, jnp.max(s, axis=1, keepdims=True))
        alpha = jnp.exp2(m - m_new)
        pe = jnp.exp2(s - m_new)
        l_ref[...] = alpha * l_ref[...] + jnp.sum(pe, axis=1, keepdims=True)
        acc_ref[...] = alpha * acc_ref[...] + pv_fn(pe.astype(BF16))
        m_ref[...] = m_new

    cq = cq_ref[0]
    s = jnp.concatenate([_dot(q[:, heads[hd]], k_ref[0, hd].astype(BF16)) for hd in range(n_heads)], axis=0)
    s = s + cq - _repeat_rows(ck_ref[0], t_new)
    step(s, lambda pe: jnp.concatenate(
        [_dot_nt(pe[groups[hd]], v_ref[0, hd].astype(BF16)) for hd in range(n_heads)], axis=0))

    @pl.when(c == pl.num_programs(1) - 1)
    def _():
        sn = _dot_nt(qbd_ref[...], kn_ref[0]) + cq - _repeat_rows(ckn_ref[0][:, 0:t_new], t_new)
        rown = lax.broadcasted_iota(I32, sn.shape, 0)
        coln = lax.broadcasted_iota(I32, sn.shape, 1)
        sn = jnp.where(coln <= rown % t_new, sn, NEG_INF)

        def new_values(pe):
            full = _dot(pe, vn_ref[0])
            return jnp.concatenate([full[groups[hd], heads[hd]] for hd in range(n_heads)], axis=0)

        step(sn, new_values)
        o = acc_ref[...] / l_ref[...]
        o_ref[0] = jnp.concatenate([o[groups[hd]] for hd in range(n_heads)], axis=1).astype(o_ref.dtype)


def _fox_sample_attn(q, k_t, v_t, k_new, v_new, cum_q, cum_all, *, n_heads, chunk):
    B, t_new, w = q.shape
    hd_w = w // n_heads
    P = k_t.shape[3]
    rows = n_heads * t_new
    cache = pl.BlockSpec((1, n_heads, hd_w, chunk), lambda b, c: (b, 0, 0, c))
    new = pl.BlockSpec((1, t_new, w), lambda b, c: (b, 0, 0))
    return pl.pallas_call(
        functools.partial(_fox_sample_kernel, n_heads=n_heads, t_new=t_new),
        grid=(B, P // chunk),
        in_specs=[new, cache, cache, new, new,
                  pl.BlockSpec((1, rows, 1), lambda b, c: (b, 0, 0)),
                  pl.BlockSpec((1, n_heads, chunk), lambda b, c: (b, 0, c)),
                  pl.BlockSpec((1, n_heads, LANES), lambda b, c: (b, 0, P // LANES))],
        out_specs=new,
        out_shape=jax.ShapeDtypeStruct((B, t_new, w), BF16),
        scratch_shapes=[pltpu.VMEM((rows, w), BF16), pltpu.VMEM((rows, 1), F32),
                        pltpu.VMEM((rows, 1), F32), pltpu.VMEM((rows, hd_w), F32)],
        compiler_params=_cparams("parallel", "arbitrary"),
        name="fox_attn_sample",
    )(q, k_t, v_t, k_new, v_new, cum_q, cum_all, cum_all)


def _mla_sample_kernel(q_ref, wuk_ref, ckv_ref, kr_ref, ckvn_ref, miscn_ref, wuv_ref, o_ref,
                       ql_ref, qr_ref, m_ref, l_ref, acc_ref, *, n_heads, t_new, nope, rope, past):
    c = pl.program_id(1)

    @pl.when(c == 0)
    def _():
        for hd in range(n_heads):
            qh = q_ref[0, :, hd * LANES:(hd + 1) * LANES]
            rows = slice(hd * t_new, (hd + 1) * t_new)
            ql_ref[rows, :] = _dot(qh, wuk_ref[hd]).astype(BF16)
            qr_ref[rows, :] = qh[:, nope:nope + rope]
        m_ref[...] = jnp.full_like(m_ref, NEG_INF)
        l_ref[...] = jnp.zeros_like(l_ref)
        acc_ref[...] = jnp.zeros_like(acc_ref)

    ql = ql_ref[...]
    qr = qr_ref[...]
    ckv = ckv_ref[0].astype(BF16)
    s = _dot_nt(ql, ckv) + _dot_nt(qr, kr_ref[0].astype(BF16))
    _softmax_step(s, ckv, m_ref, l_ref, acc_ref)

    @pl.when(c == pl.num_programs(1) - 1)
    def _():
        ckvn = ckvn_ref[0].astype(BF16)
        krn = miscn_ref[0][:, MISC_ROPE:MISC_ROPE + rope].astype(BF16)
        sn = _dot_nt(ql, ckvn) + _dot_nt(qr, krn)
        row = lax.broadcasted_iota(I32, sn.shape, 0)
        col = lax.broadcasted_iota(I32, sn.shape, 1)
        sn = jnp.where((past + col) // CHUNK <= (past + row % t_new) // CHUNK, sn, NEG_INF)
        _softmax_step(sn, ckvn, m_ref, l_ref, acc_ref)
        lat = (acc_ref[...] / l_ref[...]).astype(BF16)
        o = _dot(lat, wuv_ref[...])
        o_ref[0] = _fold_head_rows(o, n_heads, t_new, o.shape[1] // n_heads).astype(o_ref.dtype)


def _mla_sample_attn(qm, wukp, ckv_cache, kr_cache, ckv_new, misc_new, wuv, *, n_heads, chunk, nope, rope):
    B, t_new, qw = qm.shape
    P, lat = ckv_cache.shape[1:]
    rows = n_heads * t_new
    vw = wuv.shape[1]
    return pl.pallas_call(
        functools.partial(_mla_sample_kernel, n_heads=n_heads, t_new=t_new, nope=nope, rope=rope, past=P),
        grid=(B, P // chunk),
        in_specs=[pl.BlockSpec((1, t_new, qw), lambda b, c: (b, 0, 0)),
                  _const_spec(wukp.shape),
                  pl.BlockSpec((1, chunk, lat), lambda b, c: (b, c, 0)),
                  pl.BlockSpec((1, chunk, rope), lambda b, c: (b, c, 0)),
                  pl.BlockSpec((1, t_new, lat), lambda b, c: (b, 0, 0)),
                  pl.BlockSpec((1, t_new, LANES), lambda b, c: (b, 0, 0)),
                  _const_spec(wuv.shape)],
        out_specs=pl.BlockSpec((1, t_new, vw), lambda b, c: (b, 0, 0)),
        out_shape=jax.ShapeDtypeStruct((B, t_new, vw), BF16),
        scratch_shapes=[pltpu.VMEM((rows, lat), BF16), pltpu.VMEM((rows, rope), BF16),
                        pltpu.VMEM((rows, 1), F32), pltpu.VMEM((rows, 1), F32), pltpu.VMEM((rows, lat), F32)],
        compiler_params=_cparams("parallel", "arbitrary"),
        name="mla_attn_sample",
    )(qm, wukp, ckv_cache, kr_cache, ckv_new, misc_new, wuv)


def _mix_part(x, fo, mo, ga, gb, wof_ref, wom_ref, wout_ref, gx_ref, wxq_ref, x_scale):
    a = _dot(fo, wof_ref[...])
    b = _dot(mo, wom_ref[...])
    merged = (ga.astype(F32) * a + gb.astype(F32) * b).astype(BF16)
    x1 = x + _dot(merged, wout_ref[...])
    xq = (_dot(_rms(x1, gx_ref[...]).astype(BF16), wxq_ref[...]) * x_scale).astype(BF16)
    return x1, xq


def _cross_part(xq, mk, mv, x_heads):
    hd_w = xq.shape[1] // x_heads
    outs = []
    for hd in range(x_heads):
        sl = slice(hd * hd_w, (hd + 1) * hd_w)
        s = _dot_nt(xq[:, sl], mk[:, sl])
        pe = jnp.exp2(s - jnp.max(s, axis=1, keepdims=True))
        o = _dot(pe.astype(BF16), mv[:, sl]) / jnp.sum(pe, axis=1, keepdims=True)
        outs.append(o.astype(BF16))
    return jnp.concatenate(outs, axis=1)


def _route_part(x1, ca, wxo_ref, gffn_ref, wrt_ref, br_ref, carry_ref,
                x2_ref, hf_ref, idx_ref, rank_ref, gate_ref, cnt_ref):
    tm = x1.shape[0]
    n_exp = wrt_ref.shape[1]
    x2 = x1 + _dot(ca, wxo_ref[...])
    x2_ref[...] = x2
    hf = _rms(x2, gffn_ref[...])
    hf_ref[...] = hf
    hf_hi = hf.astype(BF16)
    hf_lo = (hf - hf_hi.astype(F32)).astype(BF16)
    logits = (_dot_nt(wrt_ref[0], hf_hi) + _dot_nt(wrt_ref[0], hf_lo) + _dot_nt(wrt_ref[1], hf_hi)
              + br_ref[...])
    erow = lax.broadcasted_iota(I32, (n_exp, tm), 0).astype(F32)
    picked = jnp.zeros((n_exp, tm), F32)
    vals, onehots, idxs = [], [], []
    for k in range(TOP_K):
        mx = jnp.max(logits, axis=0, keepdims=True)
        idx = jnp.min(jnp.where(logits == mx, erow, float(n_exp)), axis=0, keepdims=True)
        sel = erow == idx
        vals.append(mx)
        onehots.append(sel)
        idxs.append(idx)
        picked = picked + sel.astype(F32)
        logits = jnp.where(sel, -jnp.inf, logits)
    ex = [jnp.exp(v - vals[0]) for v in vals]
    den = ex[0] + ex[1] + ex[2] + ex[3]
    for k in range(TOP_K):
        gate_ref[k:k + 1, :] = ex[k] / den
    r = lax.broadcasted_iota(I32, (tm, tm), 0)
    cidx = lax.broadcasted_iota(I32, (tm, tm), 1)
    upper = jnp.where(r < cidx, 1.0, 0.0).astype(BF16)
    before = _dot(picked.astype(BF16), upper) + carry_ref[...]
    for k in range(TOP_K):
        rank = jnp.sum(jnp.where(onehots[k], before, 0.0), axis=0, keepdims=True)
        idx_ref[k:k + 1, :] = idxs[k].astype(I32)
        rank_ref[k:k + 1, :] = rank.astype(I32)
    carry_ref[...] = carry_ref[...] + jnp.sum(picked, axis=1, keepdims=True)
    cnt_ref[...] = jnp.broadcast_to(carry_ref[...], cnt_ref.shape).astype(I32)


def _post_attn_prompt_kernel(x_ref, fo_ref, mo_ref, ga_ref, gb_ref, mk_ref, mv_ref,
                             wof_ref, wom_ref, wout_ref, gx_ref, wxq_ref, wxo_ref, gffn_ref, wrt_ref, br_ref,
                             x2_ref, hf_ref, idx_ref, rank_ref, gate_ref, cnt_ref, carry_ref,
                             *, x_heads, x_scale):
    @pl.when(pl.program_id(0) == 0)
    def _():
        carry_ref[...] = jnp.zeros_like(carry_ref)

    x1, xq = _mix_part(x_ref[...], fo_ref[...], mo_ref[...], ga_ref[...], gb_ref[...],
                       wof_ref, wom_ref, wout_ref, gx_ref, wxq_ref, x_scale)
    ca = _cross_part(xq, mk_ref[0].astype(BF16), mv_ref[0].astype(BF16), x_heads)
    _route_part(x1, ca, wxo_ref, gffn_ref, wrt_ref, br_ref, carry_ref,
                x2_ref, hf_ref, idx_ref, rank_ref, gate_ref, cnt_ref)


def _route_out(T, D, n_exp, tm):
    tok = lambda n: pl.BlockSpec((tm, n), lambda i: (i, 0))
    col = pl.BlockSpec((TOP_K, tm), lambda i: (0, i))
    sds = jax.ShapeDtypeStruct
    shapes = [sds((T, D), F32), sds((T, D), F32), sds((TOP_K, T), I32), sds((TOP_K, T), I32),
              sds((TOP_K, T), F32), sds((n_exp, LANES), I32)]
    specs = [tok(D), tok(D), col, col, col, pl.BlockSpec((n_exp, LANES), lambda i: (0, 0))]
    return shapes, specs


def _post_attn_prompt(x, fo, mo, ga, gb, mk, mv, w, *, tm, seq):
    T, D = x.shape
    n_exp = w['wrt'].shape[1]
    tiles_per_seq = seq // tm
    tok = lambda n: pl.BlockSpec((tm, n), lambda i: (i, 0))
    mem = pl.BlockSpec((1,) + mk.shape[1:], lambda i: (i // tiles_per_seq, 0, 0))
    weights = [w['wof'], w['wom'], w['wout'], w['gx'], w['wxq'], w['wxo'], w['gffn'], w['wrt'], w['br']]
    shapes, specs = _route_out(T, D, n_exp, tm)
    return pl.pallas_call(
        functools.partial(_post_attn_prompt_kernel, x_heads=w['x_heads'], x_scale=w['x_scale']),
        grid=(T // tm,),
        in_specs=[tok(D), tok(fo.shape[1]), tok(mo.shape[1]), tok(D), tok(D), mem, mem]
        + [_const_spec(a.shape) for a in weights],
        out_specs=specs,
        out_shape=shapes,
        scratch_shapes=[pltpu.VMEM((n_exp, 1), F32)],
        compiler_params=_cparams("arbitrary"),
        name="post_attn_prompt",
    )(x, fo, mo, ga, gb, mk, mv, *weights)


def _mix_sample_kernel(x_ref, fo_ref, mo_ref, ga_ref, gb_ref, wof_ref, wom_ref, wout_ref, gx_ref, wxq_ref,
                       x1_ref, xq_ref, *, x_scale):
    x1, xq = _mix_part(x_ref[...], fo_ref[...], mo_ref[...], ga_ref[...], gb_ref[...],
                       wof_ref, wom_ref, wout_ref, gx_ref, wxq_ref, x_scale)
    x1_ref[...] = x1
    xq_ref[...] = xq


def _cross_sample_kernel(xq_ref, mk_ref, mv_ref, o_ref, *, x_heads):
    o_ref[0] = _cross_part(xq_ref[0], mk_ref[0].astype(BF16), mv_ref[0].astype(BF16), x_heads)


def _route_sample_kernel(x1_ref, ca_ref, wxo_ref, gffn_ref, wrt_ref, br_ref,
                         x2_ref, hf_ref, idx_ref, rank_ref, gate_ref, cnt_ref, carry_ref):
    carry_ref[...] = jnp.zeros_like(carry_ref)
    _route_part(x1_ref[...], ca_ref[...], wxo_ref, gffn_ref, wrt_ref, br_ref, carry_ref,
                x2_ref, hf_ref, idx_ref, rank_ref, gate_ref, cnt_ref)


def _post_attn_sample(x, fo, mo, ga, gb, mk, mv, w, *, t_new):
    T, D = x.shape
    B = T // t_new
    n_exp = w['wrt'].shape[1]
    xw = w['wxq'].shape[1]
    params = pltpu.CompilerParams(vmem_limit_bytes=VMEM_LIMIT_BYTES)
    x1, xq = pl.pallas_call(
        functools.partial(_mix_sample_kernel, x_scale=w['x_scale']),
        out_shape=[jax.ShapeDtypeStruct((T, D), F32), jax.ShapeDtypeStruct((T, xw), BF16)],
        compiler_params=params,
        name="mix_sample",
    )(x, fo, mo, ga, gb, w['wof'], w['wom'], w['wout'], w['gx'], w['wxq'])
    ca = pl.pallas_call(
        functools.partial(_cross_sample_kernel, x_heads=w['x_heads']),
        grid=(B,),
        in_specs=[pl.BlockSpec((1, t_new, xw), lambda b: (b, 0, 0)),
                  pl.BlockSpec((1,) + mk.shape[1:], lambda b: (b, 0, 0)),
                  pl.BlockSpec((1,) + mv.shape[1:], lambda b: (b, 0, 0))],
        out_specs=pl.BlockSpec((1, t_new, xw), lambda b: (b, 0, 0)),
        out_shape=jax.ShapeDtypeStruct((B, t_new, xw), BF16),
        compiler_params=_cparams("parallel"),
        name="cross_sample",
    )(xq.reshape(B, t_new, xw), mk, mv)
    shapes, _ = _route_out(T, D, n_exp, T)
    return pl.pallas_call(
        _route_sample_kernel,
        out_shape=shapes,
        scratch_shapes=[pltpu.VMEM((n_exp, 1), F32)],
        compiler_params=params,
        name="route_sample",
    )(x1, ca.reshape(T, xw), w['wxo'], w['gffn'], w['wrt'], w['br'])


def _mem_kv_kernel(m_ref, g_ref, wk_ref, wv_ref, k_ref, v_ref):
    m = _rms(m_ref[...], g_ref[...]).astype(BF16)
    k_ref[...] = _dot(m, wk_ref[...])
    v_ref[...] = _dot(m, wv_ref[...])


def _mem_kv(mem, w, *, tm):
    T, D = mem.shape
    xw = w['wxk'].shape[1]
    tok = lambda n: pl.BlockSpec((tm, n), lambda i: (i, 0))
    return pl.pallas_call(
        _mem_kv_kernel,
        grid=(T // tm,),
        in_specs=[tok(D), _const_spec(w['gmem'].shape), _const_spec(w['wxk'].shape), _const_spec(w['wxv'].shape)],
        out_specs=[tok(xw), tok(xw)],
        out_shape=[jax.ShapeDtypeStruct((T, xw), F32)] * 2,
        compiler_params=_cparams("parallel"),
        name="mem_kv",
    )(mem, w['gmem'], w['wxk'], w['wxv'])


def _dest_kernel(start_ref, idx_ref, rank_ref, dest_ref):
    idx = idx_ref[...]
    dest = rank_ref[...]
    for e in range(start_ref.shape[0]):
        dest = dest + jnp.where(idx == e, start_ref[e], 0)
    dest_ref[...] = dest


def _dest_rows(pad_start, idx_t, rank_t):
    whole = pl.BlockSpec(idx_t.shape, lambda i, s: (0, 0))
    return pl.pallas_call(
        _dest_kernel,
        grid_spec=pltpu.PrefetchScalarGridSpec(num_scalar_prefetch=1, grid=(1,), in_specs=[whole, whole],
                                               out_specs=whole),
        out_shape=jax.ShapeDtypeStruct(idx_t.shape, I32),
        name="moe_dest",
    )(pad_start, idx_t, rank_t)


def _row_copy(src_ref, src_row, dst_ref, dst_row, sem):
    return pltpu.make_async_copy(src_ref.at[pl.ds(src_row, 1), :], dst_ref.at[pl.ds(dst_row, 1), :], sem)


def _zero_unassigned_rows(cnt_ref, start_ref, nu_ref, xs_ref, zero_ref, sem, bm):
    zero_ref[...] = jnp.zeros_like(zero_ref)
    half = zero_ref.shape[0]
    n_blocks = xs_ref.shape[0] // bm

    def half_block(first_row):
        return pltpu.make_async_copy(zero_ref, xs_ref.at[pl.ds(pl.multiple_of(first_row, half), half)], sem)

    def zero_block(first_row):
        half_block(first_row).start()
        half_block(first_row + half).start()

    def expert_last_block(e, n):
        cnt = cnt_ref[e]
        partial = (cnt & (bm - 1)) != 0

        @pl.when(partial)
        def _():
            zero_block(start_ref[e] + (cnt & -bm))

        return n + partial.astype(I32)

    n_zeroed = lax.fori_loop(0, cnt_ref.shape[0], expert_last_block, 0)

    def tail_block(j, _):
        zero_block(j * bm)
        return 0

    lax.fori_loop(nu_ref[0], n_blocks, tail_block, 0)

    def drain(i, _):
        half_block(0).wait()
        half_block(0).wait()
        return 0

    lax.fori_loop(0, n_zeroed + n_blocks - nu_ref[0], drain, 0)


def _dispatch_kernel(cnt_ref, start_ref, nu_ref, dest_ref, h_ref, xs_ref, zero_ref, sem, zero_sem, *, bm):
    tm = h_ref.shape[0]

    @pl.when(pl.program_id(0) == 0)
    def _():
        _zero_unassigned_rows(cnt_ref, start_ref, nu_ref, xs_ref, zero_ref, zero_sem, bm)

    def issue(t, _):
        for k in range(TOP_K):
            _row_copy(h_ref, t, xs_ref, dest_ref[k, t], sem).start()
        return 0

    lax.fori_loop(0, tm, issue, 0, unroll=ROW_DMA_UNROLL)

    def drain(t, _):
        for k in range(TOP_K):
            _row_copy(h_ref, 0, xs_ref, 0, sem).wait()
        return 0

    lax.fori_loop(0, tm, drain, 0, unroll=2 * ROW_DMA_UNROLL)


def _dispatch(counts, pad_start, n_used, dest_t, h, n_rows, *, tm, bm):
    T, D = h.shape
    return pl.pallas_call(
        functools.partial(_dispatch_kernel, bm=bm),
        grid_spec=pltpu.PrefetchScalarGridSpec(
            num_scalar_prefetch=3, grid=(T // tm,),
            in_specs=[pl.BlockSpec((TOP_K, tm), lambda i, c, s, n: (0, i), memory_space=pltpu.SMEM),
                      pl.BlockSpec((tm, D), lambda i, c, s, n: (i, 0))],
            out_specs=pl.BlockSpec(memory_space=pl.ANY),
            scratch_shapes=[pltpu.VMEM((bm // 2, D), h.dtype), pltpu.SemaphoreType.DMA,
                            pltpu.SemaphoreType.DMA]),
        out_shape=jax.ShapeDtypeStruct((n_rows, D), h.dtype),
        compiler_params=_cparams("arbitrary"),
        name="moe_dispatch",
    )(counts, pad_start, n_used, dest_t, h)


def _expert_kernel(be_ref, nu_ref, x_ref, wgu_ref, bgu_ref, wd_ref, bd_ref, y_ref, *, ff):
    used = pl.program_id(0) < nu_ref[0]

    @pl.when(jnp.logical_not(used))
    def _():
        y_ref[...] = jnp.zeros_like(y_ref)

    @pl.when(used)
    def _():
        gu = _dot(x_ref[...].astype(BF16), wgu_ref[0]) + bgu_ref[0]
        gate = jnp.minimum(gu[:, :ff], SWIGLU_LIMIT)
        up = jnp.clip(gu[:, ff:], -SWIGLU_LIMIT, SWIGLU_LIMIT)
        act = (up + 1.0) * (gate * _sigmoid(gate * SWIGLU_ALPHA))
        y_ref[...] = _dot(act.astype(BF16), wd_ref[0]) + bd_ref[0]


def _expert_ffn(block_e, n_used, xs, w, *, bm):
    D = xs.shape[1]
    ff = w['wd'].shape[1]
    rows = pl.BlockSpec((bm, D), lambda i, be, nu: (i, 0))
    exp = lambda shape: pl.BlockSpec((1,) + shape, lambda i, be, nu: (be[i], 0, 0))
    return pl.pallas_call(
        functools.partial(_expert_kernel, ff=ff),
        grid_spec=pltpu.PrefetchScalarGridSpec(
            num_scalar_prefetch=2, grid=(xs.shape[0] // bm,),
            in_specs=[rows, exp((D, 2 * ff)), exp((1, 2 * ff)), exp((ff, D)), exp((1, D))],
            out_specs=rows),
        out_shape=jax.ShapeDtypeStruct(xs.shape, F32),
        compiler_params=_cparams("arbitrary"),
        name="moe_expert_ffn",
    )(block_e, n_used, xs, w['wgu'], w['bgu'], w['wd'], w['bd'])


def _combine_kernel(dest_ref, dest_next_ref, gate_ref, x2_ref, gfin_ref, yb_ref, y_ref, buf_ref, sem):
    i = pl.program_id(0)
    tm = x2_ref.shape[0]
    slot = i % 2

    def gather(d_ref, s):
        def issue(t, _):
            for k in range(TOP_K):
                _row_copy(yb_ref, d_ref[k, t], buf_ref.at[s, k], t, sem.at[s]).start()
            return 0

        lax.fori_loop(0, tm, issue, 0, unroll=ROW_DMA_UNROLL)

    @pl.when(i == 0)
    def _():
        gather(dest_ref, 0)

    @pl.when(i + 1 < pl.num_programs(0))
    def _():
        gather(dest_next_ref, 1 - slot)

    def drain(t, _):
        for k in range(TOP_K):
            _row_copy(yb_ref, 0, buf_ref.at[slot, k], 0, sem.at[slot]).wait()
        return 0

    lax.fori_loop(0, tm, drain, 0, unroll=2 * ROW_DMA_UNROLL)
    gate = gate_ref[...]
    y = x2_ref[...]
    for k in range(TOP_K):
        y = y + gate[:, k:k + 1] * buf_ref[slot, k]
    y_ref[...] = _rms(y, gfin_ref[...])


def _combine(dest_t, gate_tok, x2, gfin, yb, *, tm):
    T, D = x2.shape
    last = T // tm - 1
    return pl.pallas_call(
        _combine_kernel,
        grid=(T // tm,),
        in_specs=[pl.BlockSpec((TOP_K, tm), lambda i: (0, i), memory_space=pltpu.SMEM),
                  pl.BlockSpec((TOP_K, tm), lambda i: (0, jnp.minimum(i + 1, last)), memory_space=pltpu.SMEM),
                  pl.BlockSpec((tm, TOP_K), lambda i: (i, 0)),
                  pl.BlockSpec((tm, D), lambda i: (i, 0)),
                  _const_spec(gfin.shape),
                  pl.BlockSpec(memory_space=pl.ANY)],
        out_specs=pl.BlockSpec((tm, D), lambda i: (i, 0)),
        out_shape=jax.ShapeDtypeStruct((T, D), F32),
        scratch_shapes=[pltpu.VMEM((2, TOP_K, tm, D), F32), pltpu.SemaphoreType.DMA((2,))],
        compiler_params=_cparams("arbitrary"),
        name="moe_combine",
    )(dest_t, dest_t, gate_tok, x2, gfin, yb)


def _moe_and_final_norm(x2, hf, idx_t, rank_t, gate_t, counts, w, gfin, *, bm, tm):
    T = x2.shape[0]
    n_exp = counts.shape[0]
    n_blocks = -(-(T * TOP_K + n_exp * (bm - 1)) // bm)
    padded = (counts + bm - 1) // bm * bm
    pad_end = jnp.cumsum(padded)
    pad_start = (pad_end - padded).astype(I32)
    n_used = (pad_end[n_exp - 1:] // bm).astype(I32)
    first_row = jnp.arange(n_blocks, dtype=I32) * bm
    block_e = jnp.minimum(jnp.sum(first_row[:, None] >= pad_end[None, :], axis=1), n_exp - 1).astype(I32)
    dest_t = _dest_rows(pad_start, idx_t, rank_t)
    xs = _dispatch(counts.astype(I32), pad_start, n_used, dest_t, hf, n_blocks * bm, tm=tm, bm=bm)
    yb = _expert_ffn(block_e, n_used, xs, w, bm=bm)
    return _combine(dest_t, gate_t.T, x2, gfin, yb, tm=tm)


def _rot_cols(wc):
    half = wc.shape[1] // 2
    return jnp.concatenate([-wc[:, half:], wc[:, :half]], axis=1)


def _prep_layer(p, dims):
    fw, nh, nope, rope, lat, vdim, x_heads, x_hd = dims
    D = p['w_in'].shape[0]
    w_in = p['w_in']
    o = 0
    wfox = w_in[:, o:o + 3 * fw]; o += 3 * fw
    wfl = w_in[:, o:o + nh]; o += nh
    qlora = p['g_q'].shape[0]
    wqc = w_in[:, o:o + qlora]; o += qlora
    wkvc = w_in[:, o:o + lat]; o += lat
    wkr = w_in[:, o:o + rope]; o += rope
    wga = w_in[:, o:o + D]; o += D
    wgb = w_in[:, o:o + D]

    def misc_cols(parts):
        out = jnp.zeros((D, LANES), F32)
        for off, cols in parts:
            out = out.at[:, off:off + cols.shape[1]].set(cols)
        return out

    wma = misc_cols([(MISC_LOGF, wfl), (MISC_CUM, wfl), (MISC_ROPE, wkr)])
    wmb = misc_cols([(MISC_ROPE, _rot_cols(wkr))])
    bf = jnp.zeros((1, LANES), F32).at[0, MISC_LOGF:MISC_LOGF + nh].set(p['b_f']).at[0, MISC_CUM:MISC_CUM + nh].set(p['b_f'])

    wuq = p['w_uq'].reshape(qlora, nh, nope + rope)
    wqa = jnp.zeros((qlora, nh, LANES), F32).at[:, :, :nope + rope].set(wuq)
    rot = jnp.concatenate([-wuq[:, :, nope + rope // 2:], wuq[:, :, nope:nope + rope // 2]], axis=2)
    wqb = jnp.zeros((qlora, nh, LANES), F32).at[:, :, nope:nope + rope].set(rot)
    wk = jnp.zeros((lat + LANES, nh, LANES), F32).at[:lat, :, :nope].set(p['w_uk'])
    place = jnp.zeros((LANES, nh, LANES), F32)
    j = jnp.arange(rope)
    place = place.at[MISC_ROPE + j, :, nope + j].set(1.0)
    wk = wk.at[lat:, :, :].set(place)
    wukp = jnp.zeros((nh, LANES, lat), F32).at[:, :nope, :].set(jnp.transpose(p['w_uk'], (1, 2, 0)))

    b = lambda a: a.astype(BF16)
    row = lambda a: a.reshape(1, -1).astype(F32)
    n_exp = p['w_router'].shape[1]
    wrt = p['w_router'].T.astype(F32)
    wrt_hi = b(wrt)
    wrt = jnp.stack([wrt_hi, b(wrt - wrt_hi.astype(F32))])
    return {
        'fox_width': fw, 'n_heads': nh, 'fox_scale': float(fw // nh) ** -0.5 * LOG2E,
        'x_heads': x_heads, 'x_scale': float(x_hd) ** -0.5 * LOG2E,
        'gmix': row(p['g_mix']), 'wfox': b(wfox), 'wma': b(wma), 'wmb': b(wmb), 'wqc': b(wqc), 'wkvc': b(wkvc),
        'wga': b(wga), 'wgb': b(wgb), 'bf': bf, 'gq': row(p['g_q']),
        'wqa': b(wqa.reshape(qlora, nh * LANES)), 'wqb': b(wqb.reshape(qlora, nh * LANES)),
        'gkv': row(p['g_kv']), 'wk': b(wk.reshape(lat + LANES, nh * LANES)),
        'wuv': b(p['w_uv'].reshape(lat, nh * vdim)), 'wukp': b(wukp),
        'wof': b(p['w_o_fox']), 'wom': b(p['w_o_mla']), 'wout': b(p['w_out']), 'gx': row(p['g_x']),
        'wxq': b(p['w_xq']), 'wxo': b(p['w_xo']), 'gffn': row(p['g_ffn']),
        'wrt': wrt, 'br': p['b_router'].reshape(n_exp, 1).astype(F32),
        'gmem': row(p['g_mem']), 'wxk': b(p['w_xk']), 'wxv': b(p['w_xv']),
        'wgu': b(p['w_gu']), 'bgu': p['b_gu'].reshape(n_exp, 1, -1).astype(F32),
        'wd': b(p['w_down']), 'bd': p['b_down'].reshape(n_exp, 1, -1).astype(F32),
    }


def _rope_tables(pos, nope, rope, q_scale):
    half = rope // 2
    inv_freq = ROPE_BASE ** (-jnp.arange(half, dtype=F32) / half)
    ang = pos.astype(F32)[:, None] * inv_freq[None, :]
    cos = jnp.concatenate([jnp.cos(ang)] * 2, axis=1)
    sin = jnp.concatenate([jnp.sin(ang)] * 2, axis=1)
    n = pos.shape[0]
    ck = jnp.zeros((n, LANES), F32).at[:, :MISC_ROPE].set(1.0).at[:, MISC_ROPE:MISC_ROPE + rope].set(cos)
    sk = jnp.zeros((n, LANES), F32).at[:, MISC_ROPE:MISC_ROPE + rope].set(sin)
    cq = jnp.zeros((n, LANES), F32).at[:, :nope].set(1.0).at[:, nope:nope + rope].set(cos) * q_scale
    sq = jnp.zeros((n, LANES), F32).at[:, nope:nope + rope].set(sin) * q_scale
    return ck, sk, cq, sq


def _pick_tile(n, target):
    t = min(n, target)
    while n % t:
        t //= 2
    return t


def _prompt_layer(x, mem, w, dims):
    fw, nh, nope, rope, lat, vdim, x_heads, x_hd = dims
    B, S, D = x.shape
    T = B * S
    tm = _pick_tile(S, PROJ_TILE)
    tables = _rope_tables(jnp.arange(S, dtype=I32), nope, rope, float(nope + rope) ** -0.5 * LOG2E)
    fq, fk, fv, fkb, fvb, misc, ckv, qm, ga, gb, km, vm = _in_proj(
        x.reshape(T, D), w, tables, tm=tm, tiles_per_seq=S // tm, prompt=True)

    blk = _pick_tile(S, ATTN_BLOCK)
    cum = misc[:, MISC_CUM:MISC_CUM + nh].reshape(B, S // blk, blk, nh)
    cum_row = jnp.swapaxes(cum, 2, 3)
    r3 = lambda a: a.reshape(B, S, a.shape[1])
    fox_o = _prompt_attn(r3(fq), r3(fkb), r3(fvb), r3(misc), cum_row, blk=blk, n_heads=nh, fox=True)
    mla_o = _prompt_attn(r3(qm), r3(km), r3(vm), None, None, blk=blk, n_heads=nh, fox=False)

    n_mem = mem.shape[1]
    mk, mv = _mem_kv(mem.reshape(B * n_mem, D), w, tm=_pick_tile(B * n_mem, 512))
    xw = mk.shape[1]
    routed = _post_attn_prompt(
        x.reshape(T, D), fox_o.reshape(T, fw), mla_o.reshape(T, nh * vdim), ga, gb,
        mk.reshape(B, n_mem, xw), mv.reshape(B, n_mem, xw), w, tm=tm, seq=S)
    per_head = lambda a: jnp.transpose(a.reshape(B, nh, fw // nh, S), (0, 3, 1, 2))
    caches = (per_head(fk), per_head(fv),
              misc[:, MISC_LOGF:MISC_LOGF + nh].reshape(B, S, nh), ckv.reshape(B, S, lat),
              misc[:, MISC_ROPE:MISC_ROPE + rope].reshape(B, S, rope),
              mk.reshape(B, n_mem, x_heads, x_hd), mv.reshape(B, n_mem, x_heads, x_hd))
    return routed, caches


def _sample_layer(x, c_fk, c_fv, c_logf, c_ckv, c_kr, c_mk, c_mv, w, dims):
    fw, nh, nope, rope, lat, vdim, x_heads, x_hd = dims
    B, t_new, D = x.shape
    P = c_fk.shape[1]
    T = B * t_new
    pos = P + jnp.tile(jnp.arange(t_new, dtype=I32), B)
    tables = _rope_tables(pos, nope, rope, float(nope + rope) ** -0.5 * LOG2E)
    fq, fk, fv, fkb, fvb, misc, ckv, qm, ga, gb = _in_proj(
        x.reshape(T, D), w, tables, tm=T, tiles_per_seq=1, prompt=False)

    logf_new = misc[:, MISC_LOGF:MISC_LOGF + nh].reshape(B, t_new, nh)
    lf = jnp.concatenate([c_logf.astype(F32), logf_new, jnp.zeros((B, LANES - t_new, nh), F32)], axis=1)
    cum_all = _lane_cumsum(jnp.swapaxes(lf, 1, 2).reshape(B * nh, P + LANES)).reshape(B, nh, P + LANES)
    cum_q = cum_all[:, :, P:P + t_new].reshape(B, nh * t_new, 1)

    chunk = _pick_tile(P, CACHE_CHUNK)
    r3 = lambda a: a.reshape(B, t_new, a.shape[1])
    fox_o = _fox_sample_attn(r3(fq), jnp.transpose(c_fk, (0, 2, 3, 1)), jnp.transpose(c_fv, (0, 2, 3, 1)),
                             r3(fkb), r3(fvb), cum_q, cum_all, n_heads=nh, chunk=chunk)
    mla_o = _mla_sample_attn(r3(qm), w['wukp'], c_ckv, c_kr, r3(ckv), r3(misc), w['wuv'],
                             n_heads=nh, chunk=chunk, nope=nope, rope=rope)
    n_mem = c_mk.shape[1]
    routed = _post_attn_sample(
        x.reshape(T, D), fox_o.reshape(T, fw), mla_o.reshape(T, nh * vdim), ga, gb,
        c_mk.reshape(B, n_mem, x_heads * x_hd), c_mv.reshape(B, n_mem, x_heads * x_hd), w, t_new=t_new)
    caches = (fk.reshape(B, t_new, nh, fw // nh), fv.reshape(B, t_new, nh, fw // nh), logf_new,
              ckv.reshape(B, t_new, lat), misc[:, MISC_ROPE:MISC_ROPE + rope].reshape(B, t_new, rope))
    return routed, caches


def kernel(x_prompt, x_sample, mem_prompt, cache_fox_k, cache_fox_v, cache_fox_logf, cache_mla_ckv,
           cache_mla_krope, cache_mem_k, cache_mem_v, g_mix, w_in, b_f, g_q, w_uq, g_kv, w_uk, w_uv,
           w_o_fox, w_o_mla, w_out, g_x, g_mem, w_xq, w_xk, w_xv, w_xo, g_ffn, w_router, b_router,
           w_gu, b_gu, w_down, b_down, g_final):
    depth = w_in.shape[0]
    assert depth == 1, "the fused final norm assumes a single layer"
    nh, fhd = cache_fox_k.shape[3:]
    lat = cache_mla_ckv.shape[3]
    rope = cache_mla_krope.shape[3]
    nope, vdim = w_uk.shape[3], w_uv.shape[3]
    x_heads, x_hd = cache_mem_k.shape[3:]
    dims = (nh * fhd, nh, nope, rope, lat, vdim, x_heads, x_hd)
    B, S, D = x_prompt.shape
    Bs, t_new, _ = x_sample.shape
    gfin = g_final.reshape(1, D).astype(F32)

    l = 0
    p = {'g_mix': g_mix[l], 'w_in': w_in[l], 'b_f': b_f[l], 'g_q': g_q[l], 'w_uq': w_uq[l],
         'g_kv': g_kv[l], 'w_uk': w_uk[l], 'w_uv': w_uv[l], 'w_o_fox': w_o_fox[l],
         'w_o_mla': w_o_mla[l], 'w_out': w_out[l], 'g_x': g_x[l], 'g_mem': g_mem[l],
         'w_xq': w_xq[l], 'w_xk': w_xk[l], 'w_xv': w_xv[l], 'w_xo': w_xo[l], 'g_ffn': g_ffn[l],
         'w_router': w_router[l], 'b_router': b_router[l], 'w_gu': w_gu[l], 'b_gu': b_gu[l],
         'w_down': w_down[l], 'b_down': b_down[l]}
    w = _prep_layer(p, dims)

    (*routed_p, cnt_p), pc = _prompt_layer(x_prompt, mem_prompt, w, dims)
    (*routed_s, cnt_s), sc = _sample_layer(
        x_sample, cache_fox_k[l], cache_fox_v[l], cache_fox_logf[l], cache_mla_ckv[l],
        cache_mla_krope[l], cache_mem_k[l], cache_mem_v[l], w, dims)
    y_prompt = _moe_and_final_norm(*routed_p, cnt_p[:, 0], w, gfin,
                                   bm=EXPERT_ROWS_PROMPT, tm=TOKEN_TILE).reshape(B, S, D)
    y_sample = _moe_and_final_norm(*routed_s, cnt_s[:, 0], w, gfin, bm=EXPERT_ROWS_SAMPLE,
                                   tm=_pick_tile(Bs * t_new, TOKEN_TILE)).reshape(Bs, t_new, D)
    return (y_prompt, y_sample) + tuple(a[None] for a in pc) + tuple(a[None] for a in sc)
```

```python
import functools
import math

import jax
import jax.numpy as jnp
from jax import lax
from jax.experimental import pallas as pl
from jax.experimental.pallas import tpu as pltpu

F32 = jnp.float32
BF16 = jnp.bfloat16
I32 = jnp.int32

CHUNK = 64
EPS = 1e-6
NEG_INF = -1e30
ROPE_BASE = 10000.0
TOP_K = 4
SWIGLU_LIMIT = 7.0
SWIGLU_ALPHA = 1.702
LOG2E = math.log2(math.e)

LANES = 128
SUBLANES = 8
VMEM_LIMIT_BYTES = 56 * 1024 * 1024

MISC_LOGF = 0
MISC_CUM = 8
MISC_ROPE = 16

PROJ_TILE = 512
TOKEN_TILE = 256
ATTN_BLOCK = 256
CACHE_CHUNK = 1024
EXPERT_ROWS_PROMPT = 512
EXPERT_ROWS_SAMPLE = 128
ROW_DMA_UNROLL = 4


def _cparams(*sem):
    return pltpu.CompilerParams(dimension_semantics=sem, vmem_limit_bytes=VMEM_LIMIT_BYTES)


def _const_spec(shape):
    nd = len(shape)
    return pl.BlockSpec(shape, lambda *_: (0,) * nd, pipeline_mode=pl.Buffered(1))


def _rms(x, g):
    return x * lax.rsqrt(jnp.mean(x * x, axis=-1, keepdims=True) + EPS) * g


def _sigmoid(x):
    return 1.0 / (1.0 + jnp.exp(-x))


def _log_sigmoid(x):
    return jnp.minimum(x, 0.0) - jnp.log1p(jnp.exp(-jnp.abs(x)))


def _prefix_sum(c, axis):
    n = c.shape[axis]
    pos = lax.broadcasted_iota(I32, c.shape, axis)
    s = 1
    while s < n:
        c = c + jnp.where(pos >= s, pltpu.roll(c, s, axis), 0.0)
        s *= 2
    return c


def _dot(a, b):
    return jnp.dot(a, b, preferred_element_type=F32)


def _dot_nt(a, b):
    return lax.dot_general(a, b, (((1,), (1,)), ((), ())), preferred_element_type=F32)


def _tile_lanes(t, n):
    return jnp.concatenate([t] * n, axis=1)


def _in_proj_kernel(x_ref, gmix_ref, wfox_ref, wma_ref, wmb_ref, wqc_ref, wkvc_ref, wga_ref, wgb_ref,
                    bf_ref, gq_ref, wqa_ref, wqb_ref, gkv_ref, wk_ref, wuv_ref,
                    ck_ref, sk_ref, cq_ref, sq_ref,
                    fq_ref, fk_ref, fv_ref, fkb_ref, fvb_ref, misc_ref, ckv_ref, qm_ref, ga_ref, gb_ref, *rest,
                    fox_width, n_heads, fox_scale, tiles_per_seq, prompt):
    if prompt:
        km_ref, vm_ref, carry_ref = rest
    tm = x_ref.shape[0]
    h = _rms(x_ref[...], gmix_ref[...]).astype(BF16)

    fq_ref[...] = (_dot(h, wfox_ref[:, 0:fox_width]) * fox_scale).astype(BF16)
    fk = _dot(h, wfox_ref[:, fox_width:2 * fox_width])
    fv = _dot(h, wfox_ref[:, 2 * fox_width:3 * fox_width])
    if prompt:
        fk_ref[0] = fk.T
        fv_ref[0] = fv.T
    else:
        fk_ref[...] = fk
        fv_ref[...] = fv
    fkb_ref[...] = fk.astype(BF16)
    fvb_ref[...] = fv.astype(BF16)

    lane = lax.broadcasted_iota(I32, (tm, LANES), 1)
    pre = _dot(h, wma_ref[...]) * ck_ref[...] + _dot(h, wmb_ref[...]) * sk_ref[...]
    misc = jnp.where(lane < MISC_ROPE, _log_sigmoid(pre + bf_ref[...]), pre)
    if prompt:
        i = pl.program_id(0)

        @pl.when(i % tiles_per_seq == 0)
        def _():
            carry_ref[...] = jnp.zeros_like(carry_ref)

        in_cum = (lane >= MISC_CUM) & (lane < MISC_ROPE)
        cum = _prefix_sum(jnp.where(in_cum, misc * LOG2E, 0.0), 0) + carry_ref[...]
        carry_ref[...] = cum[tm - 1:tm, :]
        misc = jnp.where(in_cum, cum, misc)
    misc_ref[...] = misc

    ckv = _rms(_dot(h, wkvc_ref[...]), gkv_ref[...])
    ckv_ref[...] = ckv

    qn = _rms(_dot(h, wqc_ref[...]), gq_ref[...]).astype(BF16)
    cq = _tile_lanes(cq_ref[...], n_heads)
    sq = _tile_lanes(sq_ref[...], n_heads)
    qm_ref[...] = (_dot(qn, wqa_ref[...]) * cq + _dot(qn, wqb_ref[...]) * sq).astype(BF16)

    if prompt:
        ckv_b = ckv.astype(BF16)
        km_ref[...] = _dot(jnp.concatenate([ckv_b, misc.astype(BF16)], axis=1), wk_ref[...]).astype(BF16)
        vm_ref[...] = _dot(ckv_b, wuv_ref[...]).astype(BF16)

    ga_ref[...] = _sigmoid(_dot(h, wga_ref[...])).astype(BF16)
    gb_ref[...] = _sigmoid(_dot(h, wgb_ref[...])).astype(BF16)


def _in_proj(x, w, tables, *, tm, tiles_per_seq, prompt):
    T, D = x.shape
    fw = w['fox_width']
    nh = w['n_heads']
    lat = w['wkvc'].shape[1]
    hw = nh * LANES
    ck, sk, cq, sq = tables
    tok = lambda n: pl.BlockSpec((tm, n), lambda i: (i, 0))
    if prompt:
        tab = pl.BlockSpec((tm, LANES), lambda i: (i % tiles_per_seq, 0))
    else:
        tab = tok(LANES)
    weights = [w['gmix'], w['wfox'], w['wma'], w['wmb'], w['wqc'], w['wkvc'], w['wga'], w['wgb'],
               w['bf'], w['gq'], w['wqa'], w['wqb'], w['gkv'], w['wk'], w['wuv']]
    sds = jax.ShapeDtypeStruct
    out_shape = [sds((T, fw), BF16), sds((T, fw), F32), sds((T, fw), F32),
                 sds((T, fw), BF16), sds((T, fw), BF16), sds((T, LANES), F32),
                 sds((T, lat), F32), sds((T, hw), BF16), sds((T, D), BF16), sds((T, D), BF16)]
    out_specs = [tok(fw), tok(fw), tok(fw), tok(fw), tok(fw), tok(LANES), tok(lat), tok(hw), tok(D), tok(D)]
    scratch = []
    if prompt:
        seq = tm * tiles_per_seq
        out_shape[1] = out_shape[2] = sds((T // seq, fw, seq), F32)
        out_specs[1] = out_specs[2] = pl.BlockSpec((1, fw, tm), lambda i: (i // tiles_per_seq, 0, i % tiles_per_seq))
        out_shape += [sds((T, hw), BF16), sds((T, w['wuv'].shape[1]), BF16)]
        out_specs += [tok(hw), tok(w['wuv'].shape[1])]
        scratch = [pltpu.VMEM((1, LANES), F32)]
    kern = functools.partial(_in_proj_kernel, fox_width=fw, n_heads=nh, fox_scale=w['fox_scale'],
                             tiles_per_seq=tiles_per_seq, prompt=prompt)
    return pl.pallas_call(
        kern,
        grid=(T // tm,),
        in_specs=[tok(D)] + [_const_spec(a.shape) for a in weights] + [tab] * 4,
        out_specs=out_specs,
        out_shape=out_shape,
        scratch_shapes=scratch,
        compiler_params=_cparams("arbitrary"),
        name="in_proj_prompt" if prompt else "in_proj_sample",
    )(x, *weights, ck, sk, cq, sq)


def _fold_lanes(x, op):
    out = x[:, 0:LANES]
    for j in range(1, x.shape[1] // LANES):
        out = op(out, x[:, j * LANES:(j + 1) * LANES])
    return out


def _prompt_attn_kernel(*refs, blk, n_heads, fox):
    if fox:
        q_ref, k_ref, v_ref, ccol_ref, crow_ref, o_ref, qs_ref, cq_ref, s_ref, m_ref, l_ref, acc_ref = refs
    else:
        q_ref, k_ref, v_ref, o_ref, s_ref, m_ref, l_ref, acc_ref = refs
    qi = pl.program_id(1)
    n_pairs = n_heads // 2
    half = LANES // 2
    lane = lax.broadcasted_iota(I32, (blk, LANES), 1)

    m_ref[...] = jnp.full_like(m_ref, NEG_INF)
    if fox:
        zero = jnp.zeros((), BF16)
        for p in range(n_pairs):
            q2 = q_ref[0, :, p * LANES:(p + 1) * LANES]
            qs_ref[p, 0:blk, :] = jnp.where(lane < half, q2, zero)
            qs_ref[p, blk:2 * blk, :] = jnp.where(lane >= half, q2, zero)
            for hh in range(2):
                col = MISC_CUM + 2 * p + hh
                cq_ref[p, hh * blk:(hh + 1) * blk, :] = jnp.broadcast_to(ccol_ref[0, :, col:col + 1], (blk, LANES))

    def scores(kb, diagonal):
        ks = pl.multiple_of(kb * blk, blk)
        for p in range(n_pairs):
            if fox:
                s = _dot_nt(qs_ref[p], k_ref[0, pl.ds(ks, blk), p * LANES:(p + 1) * LANES])
                ck = jnp.concatenate([jnp.broadcast_to(crow_ref[0, kb, 2 * p + hh:2 * p + hh + 1, :], (blk, blk))
                                      for hh in range(2)], axis=0)
                s = s + (_tile_lanes(cq_ref[p], blk // LANES) - ck)
            else:
                s = jnp.concatenate(
                    [_dot_nt(q_ref[0, :, hd * LANES:(hd + 1) * LANES], k_ref[0, pl.ds(ks, blk), hd * LANES:(hd + 1) * LANES])
                     for hd in (2 * p, 2 * p + 1)], axis=0)
            if diagonal:
                r = lax.broadcasted_iota(I32, (2 * blk, blk), 0)
                r = jnp.where(r >= blk, r - blk, r)
                c = lax.broadcasted_iota(I32, (2 * blk, blk), 1)
                mask = (c <= r) if fox else ((c // CHUNK) <= (r // CHUNK))
                s = jnp.where(mask, s, NEG_INF)
            s_ref[p, kb] = s
            m_ref[p] = jnp.maximum(m_ref[p], _fold_lanes(s, jnp.maximum))

    def score_body(kb, carry):
        scores(kb, False)
        return carry

    lax.fori_loop(0, qi, score_body, 0)
    scores(qi, True)

    for p in range(n_pairs):
        m_ref[p] = jnp.broadcast_to(jnp.max(m_ref[p], axis=1, keepdims=True), (2 * blk, LANES))
    l_ref[...] = jnp.zeros_like(l_ref)
    acc_ref[...] = jnp.zeros_like(acc_ref)

    def weigh_body(kb, carry):
        ks = pl.multiple_of(kb * blk, blk)
        for p in range(n_pairs):
            pe = jnp.exp2(s_ref[p, kb] - _tile_lanes(m_ref[p], blk // LANES))
            l_ref[p] = l_ref[p] + _fold_lanes(pe, jnp.add)
            acc_ref[p] = acc_ref[p] + _dot(pe.astype(BF16), v_ref[0, pl.ds(ks, blk), p * LANES:(p + 1) * LANES])
        return carry

    lax.fori_loop(0, qi + 1, weigh_body, 0)
    for p in range(n_pairs):
        o = acc_ref[p] / jnp.sum(l_ref[p], axis=1, keepdims=True)
        o_ref[0, :, p * LANES:(p + 1) * LANES] = jnp.where(lane < half, o[0:blk], o[blk:2 * blk]).astype(o_ref.dtype)


def _prompt_attn(q, k, v, cum_col, cum_row, *, blk, n_heads, fox):
    B, S, qw = q.shape
    vw = v.shape[2]
    n_pairs = n_heads // 2
    in_specs = [pl.BlockSpec((1, blk, qw), lambda b, i: (b, i, 0)),
                pl.BlockSpec((1, S, k.shape[2]), lambda b, i: (b, 0, 0)),
                pl.BlockSpec((1, S, vw), lambda b, i: (b, 0, 0))]
    args = [q, k, v]
    stat = pltpu.VMEM((n_pairs, 2 * blk, LANES), F32)
    scratch = []
    if fox:
        in_specs += [pl.BlockSpec((1, blk, LANES), lambda b, i: (b, i, 0)),
                     pl.BlockSpec((1,) + cum_row.shape[1:], lambda b, i: (b, 0, 0, 0))]
        args += [cum_col, cum_row]
        scratch = [pltpu.VMEM((n_pairs, 2 * blk, LANES), BF16), stat]
    scratch += [pltpu.VMEM((n_pairs, S // blk, 2 * blk, blk), F32), stat, stat, stat]
    return pl.pallas_call(
        functools.partial(_prompt_attn_kernel, blk=blk, n_heads=n_heads, fox=fox),
        grid=(B, S // blk),
        in_specs=in_specs,
        out_specs=pl.BlockSpec((1, blk, vw), lambda b, i: (b, i, 0)),
        out_shape=jax.ShapeDtypeStruct((B, S, vw), BF16),
        scratch_shapes=scratch,
        compiler_params=_cparams("parallel", "parallel"),
        name="fox_attn_prompt" if fox else "mla_attn_prompt",
    )(*args)


def _lane_cumsum_kernel(x_ref, o_ref):
    o_ref[...] = _prefix_sum(x_ref[...], 1) * LOG2E


def _lane_cumsum(x):
    return pl.pallas_call(
        _lane_cumsum_kernel,
        out_shape=jax.ShapeDtypeStruct(x.shape, F32),
        compiler_params=pltpu.CompilerParams(vmem_limit_bytes=VMEM_LIMIT_BYTES),
        name="logf_cumsum_sample",
    )(x)


def _head_rows(x, n_heads, head_lanes):
    t, w = x.shape
    xt = jnp.concatenate([x] * n_heads, axis=0)
    row = lax.broadcasted_iota(I32, xt.shape, 0)
    lane = lax.broadcasted_iota(I32, xt.shape, 1)
    return jnp.where(lane // head_lanes == row // t, xt, jnp.zeros((), x.dtype))


def _fold_head_rows(o, n_heads, t, head_lanes):
    row = lax.broadcasted_iota(I32, o.shape, 0)
    lane = lax.broadcasted_iota(I32, o.shape, 1)
    o = jnp.where(lane // head_lanes == row // t, o, 0.0)
    out = o[0:t]
    for hd in range(1, n_heads):
        out = out + o[hd * t:(hd + 1) * t]
    return out


def _repeat_rows(x, t):
    return jnp.concatenate([jnp.broadcast_to(x[r:r + 1], (t, x.shape[1])) for r in range(x.shape[0])], axis=0)


def _softmax_step(s, v, m_ref, l_ref, acc_ref):
    m = m_ref[...]
    m_new = jnp.maximum(m, jnp.max(s, axis=1, keepdims=True))
    alpha = jnp.exp2(m - m_new)
    pe = jnp.exp2(s - m_new)
    l_ref[...] = alpha * l_ref[...] + jnp.sum(pe, axis=1, keepdims=True)
    acc_ref[...] = alpha * acc_ref[...] + _dot(pe.astype(BF16), v)
    m_ref[...] = m_new


def _fox_sample_kernel(q_ref, k_ref, v_ref, kn_ref, vn_ref, cq_ref, ck_ref, ckn_ref, o_ref,
                       qbd_ref, m_ref, l_ref, acc_ref, *, n_heads, t_new):
    c = pl.program_id(1)
    hd_w = q_ref.shape[2] // n_heads
    heads = [slice(hd * hd_w, (hd + 1) * hd_w) for hd in range(n_heads)]
    groups = [slice(hd * t_new, (hd + 1) * t_new) for hd in range(n_heads)]
    q = q_ref[0]

    @pl.when(c == 0)
    def _():
        qbd_ref[...] = _head_rows(q, n_heads, hd_w)
        m_ref[...] = jnp.full_like(m_ref, NEG_INF)
        l_ref[...] = jnp.zeros_like(l_ref)
        acc_ref[...] = jnp.zeros_like(acc_ref)

    def step(s, pv_fn):
        m = m_ref[...]
        m_new = jnp.maximum(m, jnp.max(s, axis=1, keepdims=True))
        alpha = jnp.exp2(m - m_new)
        pe = jnp.exp2(s - m_new)
        l_ref[...] = alpha * l_ref[...] + jnp.sum(pe, axis=1, keepdims=True)
        acc_ref[...] = alpha * acc_ref[...] + pv_fn(pe.astype(BF16))
        m_ref[...] = m_new

    cq = cq_ref[0]
    s = jnp.concatenate([_dot(q[:, heads[hd]], k_ref[0, hd].astype(BF16)) for hd in range(n_heads)], axis=0)
    s = s + cq - _repeat_rows(ck_ref[0], t_new)
    step(s, lambda pe: jnp.concatenate(
        [_dot_nt(pe[groups[hd]], v_ref[0, hd].astype(BF16)) for hd in range(n_heads)], axis=0))

    @pl.when(c == pl.num_programs(1) - 1)
    def _():
        sn = _dot_nt(qbd_ref[...], kn_ref[0]) + cq - _repeat_rows(ckn_ref[0][:, 0:t_new], t_new)
        rown = lax.broadcasted_iota(I32, sn.shape, 0)
        coln = lax.broadcasted_iota(I32, sn.shape, 1)
        sn = jnp.where(coln <= rown % t_new, sn, NEG_INF)

        def new_values(pe):
            full = _dot(pe, vn_ref[0])
            return jnp.concatenate([full[groups[hd], heads[hd]] for hd in range(n_heads)], axis=0)

        step(sn, new_values)
        o = acc_ref[...] / l_ref[...]
        o_ref[0] = jnp.concatenate([o[groups[hd]] for hd in range(n_heads)], axis=1).astype(o_ref.dtype)


def _fox_sample_attn(q, k_t, v_t, k_new, v_new, cum_q, cum_all, *, n_heads, chunk):
    B, t_new, w = q.shape
    hd_w = w // n_heads
    P = k_t.shape[3]
    rows = n_heads * t_new
    cache = pl.BlockSpec((1, n_heads, hd_w, chunk), lambda b, c: (b, 0, 0, c))
    new = pl.BlockSpec((1, t_new, w), lambda b, c: (b, 0, 0))
    return pl.pallas_call(
        functools.partial(_fox_sample_kernel, n_heads=n_heads, t_new=t_new),
        grid=(B, P // chunk),
        in_specs=[new, cache, cache, new, new,
                  pl.BlockSpec((1, rows, 1), lambda b, c: (b, 0, 0)),
                  pl.BlockSpec((1, n_heads, chunk), lambda b, c: (b, 0, c)),
                  pl.BlockSpec((1, n_heads, LANES), lambda b, c: (b, 0, P // LANES))],
        out_specs=new,
        out_shape=jax.ShapeDtypeStruct((B, t_new, w), BF16),
        scratch_shapes=[pltpu.VMEM((rows, w), BF16), pltpu.VMEM((rows, 1), F32),
                        pltpu.VMEM((rows, 1), F32), pltpu.VMEM((rows, hd_w), F32)],
        compiler_params=_cparams("parallel", "arbitrary"),
        name="fox_attn_sample",
    )(q, k_t, v_t, k_new, v_new, cum_q, cum_all, cum_all)


def _mla_sample_kernel(q_ref, wuk_ref, ckv_ref, kr_ref, ckvn_ref, miscn_ref, wuv_ref, o_ref,
                       ql_ref, qr_ref, m_ref, l_ref, acc_ref, *, n_heads, t_new, nope, rope, past):
    c = pl.program_id(1)

    @pl.when(c == 0)
    def _():
        for hd in range(n_heads):
            qh = q_ref[0, :, hd * LANES:(hd + 1) * LANES]
            rows = slice(hd * t_new, (hd + 1) * t_new)
            ql_ref[rows, :] = _dot(qh, wuk_ref[hd]).astype(BF16)
            qr_ref[rows, :] = qh[:, nope:nope + rope]
        m_ref[...] = jnp.full_like(m_ref, NEG_INF)
        l_ref[...] = jnp.zeros_like(l_ref)
        acc_ref[...] = jnp.zeros_like(acc_ref)

    ql = ql_ref[...]
    qr = qr_ref[...]
    ckv = ckv_ref[0].astype(BF16)
    s = _dot_nt(ql, ckv) + _dot(qr, kr_ref[0].astype(BF16))
    _softmax_step(s, ckv, m_ref, l_ref, acc_ref)

    @pl.when(c == pl.num_programs(1) - 1)
    def _():
        ckvn = ckvn_ref[0].astype(BF16)
        krn = miscn_ref[0][:, MISC_ROPE:MISC_ROPE + rope].astype(BF16)
        sn = _dot_nt(ql, ckvn) + _dot_nt(qr, krn)
        row = lax.broadcasted_iota(I32, sn.shape, 0)
        col = lax.broadcasted_iota(I32, sn.shape, 1)
        sn = jnp.where((past + col) // CHUNK <= (past + row % t_new) // CHUNK, sn, NEG_INF)
        _softmax_step(sn, ckvn, m_ref, l_ref, acc_ref)
        lat = (acc_ref[...] / l_ref[...]).astype(BF16)
        o = _dot(lat, wuv_ref[...])
        o_ref[0] = _fold_head_rows(o, n_heads, t_new, o.shape[1] // n_heads).astype(o_ref.dtype)


def _mla_sample_attn(qm, wukp, ckv_cache, kr_cache, ckv_new, misc_new, wuv, *, n_heads, chunk, nope, rope):
    B, t_new, qw = qm.shape
    P, lat = ckv_cache.shape[1:]
    rows = n_heads * t_new
    vw = wuv.shape[1]
    return pl.pallas_call(
        functools.partial(_mla_sample_kernel, n_heads=n_heads, t_new=t_new, nope=nope, rope=rope, past=P),
        grid=(B, P // chunk),
        in_specs=[pl.BlockSpec((1, t_new, qw), lambda b, c: (b, 0, 0)),
                  _const_spec(wukp.shape),
                  pl.BlockSpec((1, chunk, lat), lambda b, c: (b, c, 0)),
                  pl.BlockSpec((1, rope, chunk), lambda b, c: (b, 0, c)),
                  pl.BlockSpec((1, t_new, lat), lambda b, c: (b, 0, 0)),
                  pl.BlockSpec((1, t_new, LANES), lambda b, c: (b, 0, 0)),
                  _const_spec(wuv.shape)],
        out_specs=pl.BlockSpec((1, t_new, vw), lambda b, c: (b, 0, 0)),
        out_shape=jax.ShapeDtypeStruct((B, t_new, vw), BF16),
        scratch_shapes=[pltpu.VMEM((rows, lat), BF16), pltpu.VMEM((rows, rope), BF16),
                        pltpu.VMEM((rows, 1), F32), pltpu.VMEM((rows, 1), F32), pltpu.VMEM((rows, lat), F32)],
        compiler_params=_cparams("parallel", "arbitrary"),
        name="mla_attn_sample",
    )(qm, wukp, ckv_cache, kr_cache, ckv_new, misc_new, wuv)


def _mix_part(x, fo, mo, ga, gb, wof_ref, wom_ref, wout_ref, gx_ref, wxq_ref, x_scale):
    a = _dot(fo, wof_ref[...])
    b = _dot(mo, wom_ref[...])
    merged = (ga.astype(F32) * a + gb.astype(F32) * b).astype(BF16)
    x1 = x + _dot(merged, wout_ref[...])
    xq = (_dot(_rms(x1, gx_ref[...]).astype(BF16), wxq_ref[...]) * x_scale).astype(BF16)
    return x1, xq


def _cross_part(xq, mk, mv, x_heads):
    hd_w = xq.shape[1] // x_heads
    outs = []
    for hd in range(x_heads):
        sl = slice(hd * hd_w, (hd + 1) * hd_w)
        s = _dot_nt(xq[:, sl], mk[:, sl])
        pe = jnp.exp2(s - jnp.max(s, axis=1, keepdims=True))
        o = _dot(pe.astype(BF16), mv[:, sl]) / jnp.sum(pe, axis=1, keepdims=True)
        outs.append(o.astype(BF16))
    return jnp.concatenate(outs, axis=1)


def _route_part(x1, ca, wxo_ref, gffn_ref, wrt_ref, br_ref, carry_ref,
                x2_ref, hf_ref, idx_ref, rank_ref, gate_ref, cnt_ref):
    tm = x1.shape[0]
    n_exp = wrt_ref.shape[1]
    x2 = x1 + _dot(ca, wxo_ref[...])
    x2_ref[...] = x2
    hf = _rms(x2, gffn_ref[...])
    hf_ref[...] = hf
    hf_hi = hf.astype(BF16)
    hf_lo = (hf - hf_hi.astype(F32)).astype(BF16)
    logits = (_dot_nt(wrt_ref[0], hf_hi) + _dot_nt(wrt_ref[0], hf_lo) + _dot_nt(wrt_ref[1], hf_hi)
              + br_ref[...])
    erow = lax.broadcasted_iota(I32, (n_exp, tm), 0).astype(F32)
    picked = jnp.zeros((n_exp, tm), F32)
    vals, onehots, idxs = [], [], []
    for k in range(TOP_K):
        mx = jnp.max(logits, axis=0, keepdims=True)
        idx = jnp.min(jnp.where(logits == mx, erow, float(n_exp)), axis=0, keepdims=True)
        sel = erow == idx
        vals.append(mx)
        onehots.append(sel)
        idxs.append(idx)
        picked = picked + sel.astype(F32)
        logits = jnp.where(sel, -jnp.inf, logits)
    ex = [jnp.exp(v - vals[0]) for v in vals]
    den = ex[0] + ex[1] + ex[2] + ex[3]
    for k in range(TOP_K):
        gate_ref[k:k + 1, :] = ex[k] / den
    r = lax.broadcasted_iota(I32, (tm, tm), 0)
    cidx = lax.broadcasted_iota(I32, (tm, tm), 1)
    upper = jnp.where(r < cidx, 1.0, 0.0).astype(BF16)
    before = _dot(picked.astype(BF16), upper) + carry_ref[...]
    for k in range(TOP_K):
        rank = jnp.sum(jnp.where(onehots[k], before, 0.0), axis=0, keepdims=True)
        idx_ref[k:k + 1, :] = idxs[k].astype(I32)
        rank_ref[k:k + 1, :] = rank.astype(I32)
    carry_ref[...] = carry_ref[...] + jnp.sum(picked, axis=1, keepdims=True)
    cnt_ref[...] = jnp.broadcast_to(carry_ref[...], cnt_ref.shape).astype(I32)


def _post_attn_prompt_kernel(x_ref, fo_ref, mo_ref, ga_ref, gb_ref, mk_ref, mv_ref,
                             wof_ref, wom_ref, wout_ref, gx_ref, wxq_ref, wxo_ref, gffn_ref, wrt_ref, br_ref,
                             x2_ref, hf_ref, idx_ref, rank_ref, gate_ref, cnt_ref, carry_ref,
                             *, x_heads, x_scale):
    @pl.when(pl.program_id(0) == 0)
    def _():
        carry_ref[...] = jnp.zeros_like(carry_ref)

    x1, xq = _mix_part(x_ref[...], fo_ref[...], mo_ref[...], ga_ref[...], gb_ref[...],
                       wof_ref, wom_ref, wout_ref, gx_ref, wxq_ref, x_scale)
    ca = _cross_part(xq, mk_ref[0].astype(BF16), mv_ref[0].astype(BF16), x_heads)
    _route_part(x1, ca, wxo_ref, gffn_ref, wrt_ref, br_ref, carry_ref,
                x2_ref, hf_ref, idx_ref, rank_ref, gate_ref, cnt_ref)


def _route_out(T, D, n_exp, tm):
    tok = lambda n: pl.BlockSpec((tm, n), lambda i: (i, 0))
    col = pl.BlockSpec((TOP_K, tm), lambda i: (0, i))
    sds = jax.ShapeDtypeStruct
    shapes = [sds((T, D), F32), sds((T, D), F32), sds((TOP_K, T), I32), sds((TOP_K, T), I32),
              sds((TOP_K, T), F32), sds((n_exp, LANES), I32)]
    specs = [tok(D), tok(D), col, col, col, pl.BlockSpec((n_exp, LANES), lambda i: (0, 0))]
    return shapes, specs


def _post_attn_prompt(x, fo, mo, ga, gb, mk, mv, w, *, tm, seq):
    T, D = x.shape
    n_exp = w['wrt'].shape[1]
    tiles_per_seq = seq // tm
    tok = lambda n: pl.BlockSpec((tm, n), lambda i: (i, 0))
    mem = pl.BlockSpec((1,) + mk.shape[1:], lambda i: (i // tiles_per_seq, 0, 0))
    weights = [w['wof'], w['wom'], w['wout'], w['gx'], w['wxq'], w['wxo'], w['gffn'], w['wrt'], w['br']]
    shapes, specs = _route_out(T, D, n_exp, tm)
    return pl.pallas_call(
        functools.partial(_post_attn_prompt_kernel, x_heads=w['x_heads'], x_scale=w['x_scale']),
        grid=(T // tm,),
        in_specs=[tok(D), tok(fo.shape[1]), tok(mo.shape[1]), tok(D), tok(D), mem, mem]
        + [_const_spec(a.shape) for a in weights],
        out_specs=specs,
        out_shape=shapes,
        scratch_shapes=[pltpu.VMEM((n_exp, 1), F32)],
        compiler_params=_cparams("arbitrary"),
        name="post_attn_prompt",
    )(x, fo, mo, ga, gb, mk, mv, *weights)


def _mix_sample_kernel(x_ref, fo_ref, mo_ref, ga_ref, gb_ref, wof_ref, wom_ref, wout_ref, gx_ref, wxq_ref,
                       x1_ref, xq_ref, *, x_scale):
    x1, xq = _mix_part(x_ref[...], fo_ref[...], mo_ref[...], ga_ref[...], gb_ref[...],
                       wof_ref, wom_ref, wout_ref, gx_ref, wxq_ref, x_scale)
    x1_ref[...] = x1
    xq_ref[...] = xq


def _cross_sample_kernel(xq_ref, mk_ref, mv_ref, o_ref, *, x_heads):
    o_ref[0] = _cross_part(xq_ref[0], mk_ref[0].astype(BF16), mv_ref[0].astype(BF16), x_heads)


def _route_sample_kernel(x1_ref, ca_ref, wxo_ref, gffn_ref, wrt_ref, br_ref,
                         x2_ref, hf_ref, idx_ref, rank_ref, gate_ref, cnt_ref, carry_ref):
    carry_ref[...] = jnp.zeros_like(carry_ref)
    _route_part(x1_ref[...], ca_ref[...], wxo_ref, gffn_ref, wrt_ref, br_ref, carry_ref,
                x2_ref, hf_ref, idx_ref, rank_ref, gate_ref, cnt_ref)


def _post_attn_sample(x, fo, mo, ga, gb, mk, mv, w, *, t_new):
    T, D = x.shape
    B = T // t_new
    n_exp = w['wrt'].shape[1]
    xw = w['wxq'].shape[1]
    params = pltpu.CompilerParams(vmem_limit_bytes=VMEM_LIMIT_BYTES)
    x1, xq = pl.pallas_call(
        functools.partial(_mix_sample_kernel, x_scale=w['x_scale']),
        out_shape=[jax.ShapeDtypeStruct((T, D), F32), jax.ShapeDtypeStruct((T, xw), BF16)],
        compiler_params=params,
        name="mix_sample",
    )(x, fo, mo, ga, gb, w['wof'], w['wom'], w['wout'], w['gx'], w['wxq'])
    ca = pl.pallas_call(
        functools.partial(_cross_sample_kernel, x_heads=w['x_heads']),
        grid=(B,),
        in_specs=[pl.BlockSpec((1, t_new, xw), lambda b: (b, 0, 0)),
                  pl.BlockSpec((1,) + mk.shape[1:], lambda b: (b, 0, 0)),
                  pl.BlockSpec((1,) + mv.shape[1:], lambda b: (b, 0, 0))],
        out_specs=pl.BlockSpec((1, t_new, xw), lambda b: (b, 0, 0)),
        out_shape=jax.ShapeDtypeStruct((B, t_new, xw), BF16),
        compiler_params=_cparams("parallel"),
        name="cross_sample",
    )(xq.reshape(B, t_new, xw), mk, mv)
    shapes, _ = _route_out(T, D, n_exp, T)
    return pl.pallas_call(
        _route_sample_kernel,
        out_shape=shapes,
        scratch_shapes=[pltpu.VMEM((n_exp, 1), F32)],
        compiler_params=params,
        name="route_sample",
    )(x1, ca.reshape(T, xw), w['wxo'], w['gffn'], w['wrt'], w['br'])


def _mem_kv_kernel(m_ref, g_ref, wk_ref, wv_ref, k_ref, v_ref):
    m = _rms(m_ref[...], g_ref[...]).astype(BF16)
    k_ref[...] = _dot(m, wk_ref[...])
    v_ref[...] = _dot(m, wv_ref[...])


def _mem_kv(mem, w, *, tm):
    T, D = mem.shape
    xw = w['wxk'].shape[1]
    tok = lambda n: pl.BlockSpec((tm, n), lambda i: (i, 0))
    return pl.pallas_call(
        _mem_kv_kernel,
        grid=(T // tm,),
        in_specs=[tok(D), _const_spec(w['gmem'].shape), _const_spec(w['wxk'].shape), _const_spec(w['wxv'].shape)],
        out_specs=[tok(xw), tok(xw)],
        out_shape=[jax.ShapeDtypeStruct((T, xw), F32)] * 2,
        compiler_params=_cparams("parallel"),
        name="mem_kv",
    )(mem, w['gmem'], w['wxk'], w['wxv'])


def _dest_kernel(start_ref, idx_ref, rank_ref, dest_ref):
    idx = idx_ref[...]
    dest = rank_ref[...]
    for e in range(start_ref.shape[0]):
        dest = dest + jnp.where(idx == e, start_ref[e], 0)
    dest_ref[...] = dest


def _dest_rows(pad_start, idx_t, rank_t):
    whole = pl.BlockSpec(idx_t.shape, lambda i, s: (0, 0))
    return pl.pallas_call(
        _dest_kernel,
        grid_spec=pltpu.PrefetchScalarGridSpec(num_scalar_prefetch=1, grid=(1,), in_specs=[whole, whole],
                                               out_specs=whole),
        out_shape=jax.ShapeDtypeStruct(idx_t.shape, I32),
        name="moe_dest",
    )(pad_start, idx_t, rank_t)


def _row_copy(src_ref, src_row, dst_ref, dst_row, sem):
    return pltpu.make_async_copy(src_ref.at[pl.ds(src_row, 1), :], dst_ref.at[pl.ds(dst_row, 1), :], sem)


def _zero_unassigned_rows(cnt_ref, start_ref, nu_ref, xs_ref, zero_ref, sem, bm):
    zero_ref[...] = jnp.zeros_like(zero_ref)
    half = zero_ref.shape[0]
    n_blocks = xs_ref.shape[0] // bm

    def half_block(first_row):
        return pltpu.make_async_copy(zero_ref, xs_ref.at[pl.ds(pl.multiple_of(first_row, half), half)], sem)

    def zero_block(first_row):
        half_block(first_row).start()
        half_block(first_row + half).start()

    def expert_last_block(e, n):
        cnt = cnt_ref[e]
        partial = (cnt & (bm - 1)) != 0

        @pl.when(partial)
        def _():
            zero_block(start_ref[e] + (cnt & -bm))

        return n + partial.astype(I32)

    n_zeroed = lax.fori_loop(0, cnt_ref.shape[0], expert_last_block, 0)

    def tail_block(j, _):
        zero_block(j * bm)
        return 0

    lax.fori_loop(nu_ref[0], n_blocks, tail_block, 0)

    def drain(i, _):
        half_block(0).wait()
        half_block(0).wait()
        return 0

    lax.fori_loop(0, n_zeroed + n_blocks - nu_ref[0], drain, 0)


def _dispatch_kernel(cnt_ref, start_ref, nu_ref, dest_ref, h_ref, xs_ref, zero_ref, sem, zero_sem, *, bm):
    tm = h_ref.shape[0]

    @pl.when(pl.program_id(0) == 0)
    def _():
        _zero_unassigned_rows(cnt_ref, start_ref, nu_ref, xs_ref, zero_ref, zero_sem, bm)

    def issue(t, _):
        for k in range(TOP_K):
            _row_copy(h_ref, t, xs_ref, dest_ref[k, t], sem).start()
        return 0

    lax.fori_loop(0, tm, issue, 0, unroll=ROW_DMA_UNROLL)

    def drain(t, _):
        for k in range(TOP_K):
            _row_copy(h_ref, 0, xs_ref, 0, sem).wait()
        return 0

    lax.fori_loop(0, tm, drain, 0, unroll=2 * ROW_DMA_UNROLL)


def _dispatch(counts, pad_start, n_used, dest_t, h, n_rows, *, tm, bm):
    T, D = h.shape
    return pl.pallas_call(
        functools.partial(_dispatch_kernel, bm=bm),
        grid_spec=pltpu.PrefetchScalarGridSpec(
            num_scalar_prefetch=3, grid=(T // tm,),
            in_specs=[pl.BlockSpec((TOP_K, tm), lambda i, c, s, n: (0, i), memory_space=pltpu.SMEM),
                      pl.BlockSpec((tm, D), lambda i, c, s, n: (i, 0))],
            out_specs=pl.BlockSpec(memory_space=pl.ANY),
            scratch_shapes=[pltpu.VMEM((bm // 2, D), h.dtype), pltpu.SemaphoreType.DMA,
                            pltpu.SemaphoreType.DMA]),
        out_shape=jax.ShapeDtypeStruct((n_rows, D), h.dtype),
        compiler_params=_cparams("arbitrary"),
        name="moe_dispatch",
    )(counts, pad_start, n_used, dest_t, h)


def _expert_kernel(be_ref, nu_ref, x_ref, wgu_ref, bgu_ref, wd_ref, bd_ref, y_ref, *, ff):
    used = pl.program_id(0) < nu_ref[0]

    @pl.when(jnp.logical_not(used))
    def _():
        y_ref[...] = jnp.zeros_like(y_ref)

    @pl.when(used)
    def _():
        gu = _dot(x_ref[...].astype(BF16), wgu_ref[0]) + bgu_ref[0]
        gate = jnp.minimum(gu[:, :ff], SWIGLU_LIMIT)
        up = jnp.clip(gu[:, ff:], -SWIGLU_LIMIT, SWIGLU_LIMIT)
        act = (up + 1.0) * (gate * _sigmoid(gate * SWIGLU_ALPHA))
        y_ref[...] = _dot(act.astype(BF16), wd_ref[0]) + bd_ref[0]


def _expert_ffn(block_e, n_used, xs, w, *, bm):
    D = xs.shape[1]
    ff = w['wd'].shape[1]
    rows = pl.BlockSpec((bm, D), lambda i, be, nu: (i, 0))
    exp = lambda shape: pl.BlockSpec((1,) + shape, lambda i, be, nu: (be[i], 0, 0))
    return pl.pallas_call(
        functools.partial(_expert_kernel, ff=ff),
        grid_spec=pltpu.PrefetchScalarGridSpec(
            num_scalar_prefetch=2, grid=(xs.shape[0] // bm,),
            in_specs=[rows, exp((D, 2 * ff)), exp((1, 2 * ff)), exp((ff, D)), exp((1, D))],
            out_specs=rows),
        out_shape=jax.ShapeDtypeStruct(xs.shape, F32),
        compiler_params=_cparams("arbitrary"),
        name="moe_expert_ffn",
    )(block_e, n_used, xs, w['wgu'], w['bgu'], w['wd'], w['bd'])


def _combine_kernel(dest_ref, dest_next_ref, gate_ref, x2_ref, gfin_ref, yb_ref, y_ref, buf_ref, sem):
    i = pl.program_id(0)
    tm = x2_ref.shape[0]
    slot = i % 2

    def gather(d_ref, s):
        def issue(t, _):
            for k in range(TOP_K):
                _row_copy(yb_ref, d_ref[k, t], buf_ref.at[s, k], t, sem.at[s]).start()
            return 0

        lax.fori_loop(0, tm, issue, 0, unroll=ROW_DMA_UNROLL)

    @pl.when(i == 0)
    def _():
        gather(dest_ref, 0)

    @pl.when(i + 1 < pl.num_programs(0))
    def _():
        gather(dest_next_ref, 1 - slot)

    def drain(t, _):
        for k in range(TOP_K):
            _row_copy(yb_ref, 0, buf_ref.at[slot, k], 0, sem.at[slot]).wait()
        return 0

    lax.fori_loop(0, tm, drain, 0, unroll=2 * ROW_DMA_UNROLL)
    gate = gate_ref[...]
    y = x2_ref[...]
    for k in range(TOP_K):
        y = y + gate[:, k:k + 1] * buf_ref[slot, k]
    y_ref[...] = _rms(y, gfin_ref[...])


def _combine(dest_t, gate_tok, x2, gfin, yb, *, tm):
    T, D = x2.shape
    last = T // tm - 1
    return pl.pallas_call(
        _combine_kernel,
        grid=(T // tm,),
        in_specs=[pl.BlockSpec((TOP_K, tm), lambda i: (0, i), memory_space=pltpu.SMEM),
                  pl.BlockSpec((TOP_K, tm), lambda i: (0, jnp.minimum(i + 1, last)), memory_space=pltpu.SMEM),
                  pl.BlockSpec((tm, TOP_K), lambda i: (i, 0)),
                  pl.BlockSpec((tm, D), lambda i: (i, 0)),
                  _const_spec(gfin.shape),
                  pl.BlockSpec(memory_space=pl.ANY)],
        out_specs=pl.BlockSpec((tm, D), lambda i: (i, 0)),
        out_shape=jax.ShapeDtypeStruct((T, D), F32),
        scratch_shapes=[pltpu.VMEM((2, TOP_K, tm, D), F32), pltpu.SemaphoreType.DMA((2,))],
        compiler_params=_cparams("arbitrary"),
        name="moe_combine",
    )(dest_t, dest_t, gate_tok, x2, gfin, yb)


def _moe_and_final_norm(x2, hf, idx_t, rank_t, gate_t, counts, w, gfin, *, bm, tm):
    T = x2.shape[0]
    n_exp = counts.shape[0]
    n_blocks = -(-(T * TOP_K + n_exp * (bm - 1)) // bm)
    padded = (counts + bm - 1) // bm * bm
    pad_end = jnp.cumsum(padded)
    pad_start = (pad_end - padded).astype(I32)
    n_used = (pad_end[n_exp - 1:] // bm).astype(I32)
    first_row = jnp.arange(n_blocks, dtype=I32) * bm
    block_e = jnp.minimum(jnp.sum(first_row[:, None] >= pad_end[None, :], axis=1), n_exp - 1).astype(I32)
    dest_t = _dest_rows(pad_start, idx_t, rank_t)
    xs = _dispatch(counts.astype(I32), pad_start, n_used, dest_t, hf, n_blocks * bm, tm=tm, bm=bm)
    yb = _expert_ffn(block_e, n_used, xs, w, bm=bm)
    return _combine(dest_t, gate_t.T, x2, gfin, yb, tm=tm)


def _rot_cols(wc):
    half = wc.shape[1] // 2
    return jnp.concatenate([-wc[:, half:], wc[:, :half]], axis=1)


def _prep_layer(p, dims):
    fw, nh, nope, rope, lat, vdim, x_heads, x_hd = dims
    D = p['w_in'].shape[0]
    w_in = p['w_in']
    o = 0
    wfox = w_in[:, o:o + 3 * fw]; o += 3 * fw
    wfl = w_in[:, o:o + nh]; o += nh
    qlora = p['g_q'].shape[0]
    wqc = w_in[:, o:o + qlora]; o += qlora
    wkvc = w_in[:, o:o + lat]; o += lat
    wkr = w_in[:, o:o + rope]; o += rope
    wga = w_in[:, o:o + D]; o += D
    wgb = w_in[:, o:o + D]

    def misc_cols(parts):
        out = jnp.zeros((D, LANES), F32)
        for off, cols in parts:
            out = out.at[:, off:off + cols.shape[1]].set(cols)
        return out

    wma = misc_cols([(MISC_LOGF, wfl), (MISC_CUM, wfl), (MISC_ROPE, wkr)])
    wmb = misc_cols([(MISC_ROPE, _rot_cols(wkr))])
    bf = jnp.zeros((1, LANES), F32).at[0, MISC_LOGF:MISC_LOGF + nh].set(p['b_f']).at[0, MISC_CUM:MISC_CUM + nh].set(p['b_f'])

    wuq = p['w_uq'].reshape(qlora, nh, nope + rope)
    wqa = jnp.zeros((qlora, nh, LANES), F32).at[:, :, :nope + rope].set(wuq)
    rot = jnp.concatenate([-wuq[:, :, nope + rope // 2:], wuq[:, :, nope:nope + rope // 2]], axis=2)
    wqb = jnp.zeros((qlora, nh, LANES), F32).at[:, :, nope:nope + rope].set(rot)
    wk = jnp.zeros((lat + LANES, nh, LANES), F32).at[:lat, :, :nope].set(p['w_uk'])
    place = jnp.zeros((LANES, nh, LANES), F32)
    j = jnp.arange(rope)
    place = place.at[MISC_ROPE + j, :, nope + j].set(1.0)
    wk = wk.at[lat:, :, :].set(place)
    wukp = jnp.zeros((nh, LANES, lat), F32).at[:, :nope, :].set(jnp.transpose(p['w_uk'], (1, 2, 0)))

    b = lambda a: a.astype(BF16)
    row = lambda a: a.reshape(1, -1).astype(F32)
    n_exp = p['w_router'].shape[1]
    wrt = p['w_router'].T.astype(F32)
    wrt_hi = b(wrt)
    wrt = jnp.stack([wrt_hi, b(wrt - wrt_hi.astype(F32))])
    return {
        'fox_width': fw, 'n_heads': nh, 'fox_scale': float(fw // nh) ** -0.5 * LOG2E,
        'x_heads': x_heads, 'x_scale': float(x_hd) ** -0.5 * LOG2E,
        'gmix': row(p['g_mix']), 'wfox': b(wfox), 'wma': b(wma), 'wmb': b(wmb), 'wqc': b(wqc), 'wkvc': b(wkvc),
        'wga': b(wga), 'wgb': b(wgb), 'bf': bf, 'gq': row(p['g_q']),
        'wqa': b(wqa.reshape(qlora, nh * LANES)), 'wqb': b(wqb.reshape(qlora, nh * LANES)),
        'gkv': row(p['g_kv']), 'wk': b(wk.reshape(lat + LANES, nh * LANES)),
        'wuv': b(p['w_uv'].reshape(lat, nh * vdim)), 'wukp': b(wukp),
        'wof': b(p['w_o_fox']), 'wom': b(p['w_o_mla']), 'wout': b(p['w_out']), 'gx': row(p['g_x']),
        'wxq': b(p['w_xq']), 'wxo': b(p['w_xo']), 'gffn': row(p['g_ffn']),
        'wrt': wrt, 'br': p['b_router'].reshape(n_exp, 1).astype(F32),
        'gmem': row(p['g_mem']), 'wxk': b(p['w_xk']), 'wxv': b(p['w_xv']),
        'wgu': b(p['w_gu']), 'bgu': p['b_gu'].reshape(n_exp, 1, -1).astype(F32),
        'wd': b(p['w_down']), 'bd': p['b_down'].reshape(n_exp, 1, -1).astype(F32),
    }


def _rope_tables(pos, nope, rope, q_scale):
    half = rope // 2
    inv_freq = ROPE_BASE ** (-jnp.arange(half, dtype=F32) / half)
    ang = pos.astype(F32)[:, None] * inv_freq[None, :]
    cos = jnp.concatenate([jnp.cos(ang)] * 2, axis=1)
    sin = jnp.concatenate([jnp.sin(ang)] * 2, axis=1)
    n = pos.shape[0]
    ck = jnp.zeros((n, LANES), F32).at[:, :MISC_ROPE].set(1.0).at[:, MISC_ROPE:MISC_ROPE + rope].set(cos)
    sk = jnp.zeros((n, LANES), F32).at[:, MISC_ROPE:MISC_ROPE + rope].set(sin)
    cq = jnp.zeros((n, LANES), F32).at[:, :nope].set(1.0).at[:, nope:nope + rope].set(cos) * q_scale
    sq = jnp.zeros((n, LANES), F32).at[:, nope:nope + rope].set(sin) * q_scale
    return ck, sk, cq, sq


def _pick_tile(n, target):
    t = min(n, target)
    while n % t:
        t //= 2
    return t


def _prompt_layer(x, mem, w, dims):
    fw, nh, nope, rope, lat, vdim, x_heads, x_hd = dims
    B, S, D = x.shape
    T = B * S
    tm = _pick_tile(S, PROJ_TILE)
    tables = _rope_tables(jnp.arange(S, dtype=I32), nope, rope, float(nope + rope) ** -0.5 * LOG2E)
    fq, fk, fv, fkb, fvb, misc, ckv, qm, ga, gb, km, vm = _in_proj(
        x.reshape(T, D), w, tables, tm=tm, tiles_per_seq=S // tm, prompt=True)

    blk = _pick_tile(S, ATTN_BLOCK)
    cum = misc[:, MISC_CUM:MISC_CUM + nh].reshape(B, S // blk, blk, nh)
    cum_row = jnp.swapaxes(cum, 2, 3)
    r3 = lambda a: a.reshape(B, S, a.shape[1])
    fox_o = _prompt_attn(r3(fq), r3(fkb), r3(fvb), r3(misc), cum_row, blk=blk, n_heads=nh, fox=True)
    mla_o = _prompt_attn(r3(qm), r3(km), r3(vm), None, None, blk=blk, n_heads=nh, fox=False)

    n_mem = mem.shape[1]
    mk, mv = _mem_kv(mem.reshape(B * n_mem, D), w, tm=_pick_tile(B * n_mem, 512))
    xw = mk.shape[1]
    routed = _post_attn_prompt(
        x.reshape(T, D), fox_o.reshape(T, fw), mla_o.reshape(T, nh * vdim), ga, gb,
        mk.reshape(B, n_mem, xw), mv.reshape(B, n_mem, xw), w, tm=tm, seq=S)
    per_head = lambda a: jnp.transpose(a.reshape(B, nh, fw // nh, S), (0, 3, 1, 2))
    caches = (per_head(fk), per_head(fv),
              misc[:, MISC_LOGF:MISC_LOGF + nh].reshape(B, S, nh), ckv.reshape(B, S, lat),
              misc[:, MISC_ROPE:MISC_ROPE + rope].reshape(B, S, rope),
              mk.reshape(B, n_mem, x_heads, x_hd), mv.reshape(B, n_mem, x_heads, x_hd))
    return routed, caches


def _sample_layer(x, c_fk, c_fv, c_logf, c_ckv, c_kr, c_mk, c_mv, w, dims):
    fw, nh, nope, rope, lat, vdim, x_heads, x_hd = dims
    B, t_new, D = x.shape
    P = c_fk.shape[1]
    T = B * t_new
    pos = P + jnp.tile(jnp.arange(t_new, dtype=I32), B)
    tables = _rope_tables(pos, nope, rope, float(nope + rope) ** -0.5 * LOG2E)
    fq, fk, fv, fkb, fvb, misc, ckv, qm, ga, gb = _in_proj(
        x.reshape(T, D), w, tables, tm=T, tiles_per_seq=1, prompt=False)

    logf_new = misc[:, MISC_LOGF:MISC_LOGF + nh].reshape(B, t_new, nh)
    lf = jnp.concatenate([c_logf.astype(F32), logf_new, jnp.zeros((B, LANES - t_new, nh), F32)], axis=1)
    cum_all = _lane_cumsum(jnp.swapaxes(lf, 1, 2).reshape(B * nh, P + LANES)).reshape(B, nh, P + LANES)
    cum_q = cum_all[:, :, P:P + t_new].reshape(B, nh * t_new, 1)

    chunk = _pick_tile(P, CACHE_CHUNK)
    r3 = lambda a: a.reshape(B, t_new, a.shape[1])
    fox_o = _fox_sample_attn(r3(fq), jnp.transpose(c_fk, (0, 2, 3, 1)), jnp.transpose(c_fv, (0, 2, 3, 1)),
                             r3(fkb), r3(fvb), cum_q, cum_all, n_heads=nh, chunk=chunk)
    mla_o = _mla_sample_attn(r3(qm), w['wukp'], c_ckv, jnp.swapaxes(c_kr, 1, 2), r3(ckv), r3(misc), w['wuv'],
                             n_heads=nh, chunk=chunk, nope=nope, rope=rope)
    n_mem = c_mk.shape[1]
    routed = _post_attn_sample(
        x.reshape(T, D), fox_o.reshape(T, fw), mla_o.reshape(T, nh * vdim), ga, gb,
        c_mk.reshape(B, n_mem, x_heads * x_hd), c_mv.reshape(B, n_mem, x_heads * x_hd), w, t_new=t_new)
    caches = (fk.reshape(B, t_new, nh, fw // nh), fv.reshape(B, t_new, nh, fw // nh), logf_new,
              ckv.reshape(B, t_new, lat), misc[:, MISC_ROPE:MISC_ROPE + rope].reshape(B, t_new, rope))
    return routed, caches


def kernel(x_prompt, x_sample, mem_prompt, cache_fox_k, cache_fox_v, cache_fox_logf, cache_mla_ckv,
           cache_mla_krope, cache_mem_k, cache_mem_v, g_mix, w_in, b_f, g_q, w_uq, g_kv, w_uk, w_uv,
           w_o_fox, w_o_mla, w_out, g_x, g_mem, w_xq, w_xk, w_xv, w_xo, g_ffn, w_router, b_router,
           w_gu, b_gu, w_down, b_down, g_final):
    depth = w_in.shape[0]
    assert depth == 1, "the fused final norm assumes a single layer"
    nh, fhd = cache_fox_k.shape[3:]
    lat = cache_mla_ckv.shape[3]
    rope = cache_mla_krope.shape[3]
    nope, vdim = w_uk.shape[3], w_uv.shape[3]
    x_heads, x_hd = cache_mem_k.shape[3:]
    dims = (nh * fhd, nh, nope, rope, lat, vdim, x_heads, x_hd)
    B, S, D = x_prompt.shape
    Bs, t_new, _ = x_sample.shape
    gfin = g_final.reshape(1, D).astype(F32)

    l = 0
    p = {'g_mix': g_mix[l], 'w_in': w_in[l], 'b_f': b_f[l], 'g_q': g_q[l], 'w_uq': w_uq[l],
         'g_kv': g_kv[l], 'w_uk': w_uk[l], 'w_uv': w_uv[l], 'w_o_fox': w_o_fox[l],
         'w_o_mla': w_o_mla[l], 'w_out': w_out[l], 'g_x': g_x[l], 'g_mem': g_mem[l],
         'w_xq': w_xq[l], 'w_xk': w_xk[l], 'w_xv': w_xv[l], 'w_xo': w_xo[l], 'g_ffn': g_ffn[l],
         'w_router': w_router[l], 'b_router': b_router[l], 'w_gu': w_gu[l], 'b_gu': b_gu[l],
         'w_down': w_down[l], 'b_down': b_down[l]}
    w = _prep_layer(p, dims)

    (*routed_p, cnt_p), pc = _prompt_layer(x_prompt, mem_prompt, w, dims)
    (*routed_s, cnt_s), sc = _sample_layer(
        x_sample, cache_fox_k[l], cache_fox_v[l], cache_fox_logf[l], cache_mla_ckv[l],
        cache_mla_krope[l], cache_mem_k[l], cache_mem_v[l], w, dims)
    y_prompt = _moe_and_final_norm(*routed_p, cnt_p[:, 0], w, gfin,
                                   bm=EXPERT_ROWS_PROMPT, tm=TOKEN_TILE).reshape(B, S, D)
    y_sample = _moe_and_final_norm(*routed_s, cnt_s[:, 0], w, gfin, bm=EXPERT_ROWS_SAMPLE,
                                   tm=_pick_tile(Bs * t_new, TOKEN_TILE)).reshape(Bs, t_new, D)
    return (y_prompt, y_sample) + tuple(a[None] for a in pc) + tuple(a[None] for a in sc)
```
